```python
import math
import jax
import jax.numpy as jnp
from jax import lax
import numpy as np

D_MODEL = 2048
BATCH = 8
SEQ = 2048
DEPTH = 1

BRANCH_WIDTH = D_MODEL // 2
N_BRANCHES = 2
HEAD_DIM = 128
ATTN_GROUPS = 3
WINDOWS = (128, 512, 2048)
DILATIONS = (1, 4, 16)
ATTN_HEADS = BRANCH_WIDTH // HEAD_DIM
QKV_WIDTH = ATTN_GROUPS * ATTN_HEADS * HEAD_DIM
ATTN_BLOCK = 128
NUM_BUCKETS = 32
MAX_DISTANCE = 2048
SGU_CHUNK = 128
SGU_GROUP_DIM = 128
SGU_GROUPS = BRANCH_WIDTH // SGU_GROUP_DIM
IN_COLS = 3 * QKV_WIDTH + 2 * BRANCH_WIDTH + N_BRANCHES * D_MODEL
N_EXPERTS = 32
TOP_K = 4
EXPERT_HIDDEN = D_MODEL
SWIGLU_ALPHA = 1.702
SWIGLU_LIMIT = 7.0
MOE_BLOCK = 512
LN_EPS = 1e-5

kernel_name = "hybrid_dilated_attn_gmlp_moe_deepnorm"


def layer_norm(x, g, b):
    xf = x.astype(jnp.float32)
    mu = jnp.mean(xf, axis=-1, keepdims=True)
    var = jnp.mean(jnp.square(xf - mu), axis=-1, keepdims=True)
    return ((xf - mu) * lax.rsqrt(var + LN_EPS) * g + b).astype(x.dtype)


def t5_bucket(dist):
    exact = NUM_BUCKETS // 2
    d = jnp.maximum(dist, 1).astype(jnp.float32)
    large = exact + (jnp.log(d / exact) / math.log(MAX_DISTANCE / exact)
                     * (NUM_BUCKETS - exact)).astype(jnp.int32)
    large = jnp.minimum(large, NUM_BUCKETS - 1)
    return jnp.where(dist < exact, dist, large)


def dilated_group_attention(q, k, v, bias_table, window, dilation):
    B, S, H, Dh = q.shape
    r = dilation
    span = window // dilation
    L = S // r
    nb = -(-L // ATTN_BLOCK)
    Lp = nb * ATTN_BLOCK

    def to_sub(t):
        t = t.reshape(B, L, r, H, Dh).transpose(0, 2, 1, 3, 4)
        t = jnp.pad(t, ((0, 0), (0, 0), (0, Lp - L), (0, 0), (0, 0)))
        return t.reshape(B, r, nb, ATTN_BLOCK, H, Dh)

    def with_prev(t):
        prev = jnp.pad(t[:, :, :-1], ((0, 0), (0, 0), (1, 0), (0, 0), (0, 0), (0, 0)))
        return jnp.concatenate([prev, t], axis=3)

    qb = to_sub(q)
    kw = with_prev(to_sub(k))
    vw = with_prev(to_sub(v))
    qi = jnp.arange(ATTN_BLOCK, dtype=jnp.int32)[:, None]
    kj = jnp.arange(2 * ATTN_BLOCK, dtype=jnp.int32)[None, :]
    sub_dist = qi + ATTN_BLOCK - kj
    band = (sub_dist >= 0) & (sub_dist <= span)
    bias = bias_table[t5_bucket(r * jnp.clip(sub_dist, 0, span))]
    bias = bias.transpose(2, 0, 1).astype(jnp.float32)
    has_prev = (jnp.arange(nb)[:, None, None] > 0) | (kj[None] >= ATTN_BLOCK)
    valid = band[None] & has_prev

    scores = jnp.einsum('brnqhd,brnkhd->brnhqk', qb, kw).astype(jnp.float32)
    scores = scores * (HEAD_DIM ** -0.5) + bias
    scores = jnp.where(valid[:, None], scores, -jnp.inf)
    m = jnp.max(scores, axis=-1, keepdims=True)
    p = jnp.exp(scores - m)
    denom = jnp.sum(p, axis=-1, keepdims=True)
    o = jnp.einsum('brnhqk,brnkhd->brnqhd', (p / denom).astype(v.dtype), vw)
    lse = (m + jnp.log(denom))[..., 0]
    o = o.reshape(B, r, Lp, H, Dh)[:, :, :L].transpose(0, 2, 1, 3, 4).reshape(B, S, H, Dh)
    lse = lse.transpose(0, 1, 2, 4, 3).reshape(B, r, Lp, H)[:, :, :L]
    lse = lse.transpose(0, 2, 1, 3).reshape(B, S, H)
    return o, lse


def dilated_attention(q, k, v, bias_table):
    outs, lses = [], []
    for g in range(ATTN_GROUPS):
        o, l = dilated_group_attention(q[:, :, g], k[:, :, g], v[:, :, g],
                                       bias_table[:, g], WINDOWS[g], DILATIONS[g])
        outs.append(o)
        lses.append(l)
    out = jnp.stack(outs, axis=0)
    lse = jnp.stack(lses, axis=0)
    w = jax.nn.softmax(lse, axis=0)
    return jnp.sum(w[..., None].astype(out.dtype) * out, axis=0)


def spatial_gating(z, ln_g, ln_b, w_s, b_s):
    B, S, _ = z.shape
    z = jax.nn.gelu(z)
    u, v = z[..., :BRANCH_WIDTH], z[..., BRANCH_WIDTH:]
    v = layer_norm(v, ln_g, ln_b)
    vc = v.reshape(B, S // SGU_CHUNK, SGU_CHUNK, SGU_GROUPS, SGU_GROUP_DIM)
    causal = jnp.tril(jnp.ones((SGU_CHUNK, SGU_CHUNK), dtype=bool))
    w = jnp.where(causal[None], w_s, 0.0).astype(v.dtype)
    mixed = jnp.einsum('gij,bcjgd->bcigd', w, vc) + b_s.T[None, None, :, :, None]
    return u * mixed.reshape(B, S, BRANCH_WIDTH)


def moe_ffn(h, w_router, b_router, w_gu, b_gu, w_down, b_down):
    B, S, D = h.shape
    T = B * S
    A = T * TOP_K
    ht = h.reshape(T, D)
    logits = (ht @ w_router + b_router).astype(jnp.float32)
    top_val, top_idx = lax.top_k(logits, TOP_K)
    gate = jax.nn.softmax(top_val, axis=-1)
    flat_e = top_idx.reshape(A).astype(jnp.int32)
    flat_tok = jnp.repeat(jnp.arange(T, dtype=jnp.int32), TOP_K)
    flat_gate = gate.reshape(A)
    order = jnp.argsort(flat_e)
    sorted_e = flat_e[order]
    counts = jnp.bincount(flat_e, length=N_EXPERTS).astype(jnp.int32)
    padded = (counts + MOE_BLOCK - 1) // MOE_BLOCK * MOE_BLOCK
    start = jnp.cumsum(counts) - counts
    pend = jnp.cumsum(padded)
    pstart = pend - padded
    rank = jnp.arange(A, dtype=jnp.int32) - start[sorted_e]
    dest = pstart[sorted_e] + rank
    n_blocks = (A + N_EXPERTS * MOE_BLOCK + MOE_BLOCK - 1) // MOE_BLOCK
    cap = n_blocks * MOE_BLOCK
    slot_tok = jnp.zeros((cap,), jnp.int32).at[dest].set(flat_tok[order])
    slot_gate = jnp.zeros((cap,), jnp.float32).at[dest].set(flat_gate[order])
    block_start = jnp.arange(n_blocks, dtype=jnp.int32) * MOE_BLOCK
    block_e = jnp.minimum(jnp.searchsorted(pend, block_start, side='right'), N_EXPERTS - 1)

    def expert_block(args):
        e, tok = args
        xb = ht[tok]
        hb = xb @ w_gu[e] + b_gu[e]
        glu = jnp.minimum(hb[:, :EXPERT_HIDDEN], SWIGLU_LIMIT)
        lin = jnp.clip(hb[:, EXPERT_HIDDEN:], -SWIGLU_LIMIT, SWIGLU_LIMIT)
        act = glu * jax.nn.sigmoid(SWIGLU_ALPHA * glu) * (lin + 1.0)
        return act @ w_down[e] + b_down[e]

    y = lax.map(expert_block, (block_e, slot_tok.reshape(n_blocks, MOE_BLOCK)))
    y = (y.reshape(cap, D) * slot_gate[:, None]).astype(h.dtype)
    out = jnp.zeros((T, D), h.dtype).at[slot_tok].add(y)
    return out.reshape(B, S, D)


def setup_inputs(seed: int = 0) -> dict:
    key = jax.random.key(seed)
    ks = jax.random.split(key, 24)
    f32 = jnp.float32
    L = DEPTH
    beta = (8.0 * DEPTH) ** -0.25

    def nrm(k, shape, scale):
        return jax.random.normal(k, shape, f32) * scale

    x = nrm(ks[0], (BATCH, SEQ, D_MODEL), 1.0)
    w_qk = nrm(ks[1], (L, D_MODEL, 2 * QKV_WIDTH), D_MODEL ** -0.5)
    w_v = nrm(ks[2], (L, D_MODEL, QKV_WIDTH), beta * D_MODEL ** -0.5)
    w_z = nrm(ks[3], (L, D_MODEL, 2 * BRANCH_WIDTH), D_MODEL ** -0.5)
    w_g = nrm(ks[4], (L, D_MODEL, N_BRANCHES * D_MODEL), D_MODEL ** -0.5)
    w_in = jnp.concatenate([w_qk, w_v, w_z, w_g], axis=-1)
    rel_bias = nrm(ks[5], (NUM_BUCKETS, ATTN_GROUPS * ATTN_HEADS), 0.5)
    sgu_ln_g = 1.0 + nrm(ks[6], (L, BRANCH_WIDTH), 0.05)
    sgu_ln_b = nrm(ks[7], (L, BRANCH_WIDTH), 0.02)
    sgu_w = nrm(ks[8], (L, SGU_GROUPS, SGU_CHUNK, SGU_CHUNK), SGU_CHUNK ** -0.5)
    sgu_b = 1.0 + nrm(ks[9], (L, SGU_GROUPS, SGU_CHUNK), 0.01)
    w_branch = nrm(ks[10], (L, N_BRANCHES, BRANCH_WIDTH, D_MODEL), beta * BRANCH_WIDTH ** -0.5)
    w_out = nrm(ks[11], (L, D_MODEL, D_MODEL), beta * D_MODEL ** -0.5)
    ln1_g = 1.0 + nrm(ks[12], (L, D_MODEL), 0.05)
    ln1_b = nrm(ks[13], (L, D_MODEL), 0.02)
    w_router = nrm(ks[14], (L, D_MODEL, N_EXPERTS), D_MODEL ** -0.5)
    b_router = nrm(ks[15], (L, N_EXPERTS), 0.01)
    w_gu = nrm(ks[16], (L, N_EXPERTS, D_MODEL, 2 * EXPERT_HIDDEN), beta * D_MODEL ** -0.5)
    b_gu = nrm(ks[17], (L, N_EXPERTS, 2 * EXPERT_HIDDEN), 0.01)
    w_down = nrm(ks[18], (L, N_EXPERTS, EXPERT_HIDDEN, D_MODEL), beta * EXPERT_HIDDEN ** -0.5)
    b_down = nrm(ks[19], (L, N_EXPERTS, D_MODEL), 0.01)
    ln2_g = 1.0 + nrm(ks[20], (L, D_MODEL), 0.05)
    ln2_b = nrm(ks[21], (L, D_MODEL), 0.02)
    return {"x": x, "w_in": w_in, "rel_bias": rel_bias, "sgu_ln_g": sgu_ln_g,
            "sgu_ln_b": sgu_ln_b, "sgu_w": sgu_w, "sgu_b": sgu_b, "w_branch": w_branch,
            "w_out": w_out, "ln1_g": ln1_g, "ln1_b": ln1_b, "w_router": w_router,
            "b_router": b_router, "w_gu": w_gu, "b_gu": b_gu, "w_down": w_down,
            "b_down": b_down, "ln2_g": ln2_g, "ln2_b": ln2_b}


def reference(x, w_in, rel_bias, sgu_ln_g, sgu_ln_b, sgu_w, sgu_b, w_branch, w_out,
              ln1_g, ln1_b, w_router, b_router, w_gu, b_gu, w_down, b_down, ln2_g, ln2_b):
    alpha = (2.0 * DEPTH) ** 0.25
    B, S, D = x.shape
    bias_table = rel_bias.reshape(NUM_BUCKETS, ATTN_GROUPS, ATTN_HEADS)
    o_k, o_v, o_z, o_g = QKV_WIDTH, 2 * QKV_WIDTH, 3 * QKV_WIDTH, 3 * QKV_WIDTH + 2 * BRANCH_WIDTH
    h = x
    for l in range(DEPTH):
        proj = h @ w_in[l]
        q = proj[..., :o_k].reshape(B, S, ATTN_GROUPS, ATTN_HEADS, HEAD_DIM)
        k = proj[..., o_k:o_v].reshape(B, S, ATTN_GROUPS, ATTN_HEADS, HEAD_DIM)
        v = proj[..., o_v:o_z].reshape(B, S, ATTN_GROUPS, ATTN_HEADS, HEAD_DIM)
        z = proj[..., o_z:o_g]
        gates = jax.nn.sigmoid(proj[..., o_g:].reshape(B, S, N_BRANCHES, D))
        y_attn = dilated_attention(q, k, v, bias_table).reshape(B, S, BRANCH_WIDTH)
        y_sgu = spatial_gating(z, sgu_ln_g[l], sgu_ln_b[l], sgu_w[l], sgu_b[l])
        branches = jnp.stack([y_attn, y_sgu], axis=2)
        y = jnp.einsum('bsnc,ncd->bsnd', branches, w_branch[l])
        mixed = jnp.sum(gates * y, axis=2) @ w_out[l]
        h = layer_norm(alpha * h + mixed, ln1_g[l], ln1_b[l])
        ffn = moe_ffn(h, w_router[l], b_router[l], w_gu[l], b_gu[l], w_down[l], b_down[l])
        h = layer_norm(alpha * h + ffn, ln2_g[l], ln2_b[l])
    return h
```

```python
import functools
import math

import numpy as np
import jax
import jax.numpy as jnp
from jax import lax
from jax.experimental import pallas as pl
from jax.experimental.pallas import tpu as pltpu

D_MODEL = 2048
BATCH = 8
SEQ = 2048
T = BATCH * SEQ
BRANCH_WIDTH = D_MODEL // 2
HEAD_DIM = 128
ATTN_GROUPS = 3
WINDOWS = (128, 512, 2048)
DILATIONS = (1, 4, 16)
ATTN_HEADS = BRANCH_WIDTH // HEAD_DIM
QKV_WIDTH = ATTN_GROUPS * ATTN_HEADS * HEAD_DIM
ATTN_BLOCK = 128
NUM_BUCKETS = 32
MAX_DISTANCE = 2048
SGU_CHUNK = 128
SGU_GROUPS = BRANCH_WIDTH // SGU_CHUNK
IN_COLS = 3 * QKV_WIDTH + 2 * BRANCH_WIDTH + 2 * D_MODEL
N_EXPERTS = 32
TOP_K = 4
SWIGLU_ALPHA = 1.702
SWIGLU_LIMIT = 7.0
LN_EPS = 1e-5
ALPHA = 2.0 ** 0.25

COL_U = 3 * QKV_WIDTH // 1024
COL_V = COL_U + 1
COL_GA = COL_U + 2
COL_GS = COL_GA + 2

BM = 512
NB = (T * TOP_K + N_EXPERTS * (BM - 1) + BM - 1) // BM
CAP = NB * BM

VMEM_LIMIT = 56 * 1024 * 1024

f32 = jnp.float32
bf16 = jnp.bfloat16
i32 = jnp.int32


def _layer_norm(x, g, b):
    mu = jnp.mean(x, axis=-1, keepdims=True)
    xc = x - mu
    var = jnp.mean(xc * xc, axis=-1, keepdims=True)
    return xc * lax.rsqrt(var + LN_EPS) * g + b


IP_TM = 1024
IP_TN = 1024


def _inproj_kernel(x_ref, w_ref, o_ref):
    j = pl.program_id(1)
    acc = jnp.dot(x_ref[...], w_ref[...], preferred_element_type=f32)

    @pl.when(j < COL_U)
    def _():
        o_ref[...] = acc

    @pl.when((j >= COL_U) & (j < COL_GA))
    def _():
        o_ref[...] = jax.nn.gelu(acc)

    @pl.when(j >= COL_GA)
    def _():
        o_ref[...] = jax.nn.sigmoid(acc)


def _inproj(xb, wb):
    return pl.pallas_call(
        _inproj_kernel,
        grid=(T // IP_TM, IN_COLS // IP_TN),
        in_specs=[pl.BlockSpec((IP_TM, D_MODEL), lambda i, j: (i, 0)),
                  pl.BlockSpec((D_MODEL, IP_TN), lambda i, j: (0, j))],
        out_specs=pl.BlockSpec((IP_TM, IP_TN), lambda i, j: (i, j)),
        out_shape=jax.ShapeDtypeStruct((T, IN_COLS), f32),
        compiler_params=pltpu.CompilerParams(
            dimension_semantics=("arbitrary", "arbitrary"), vmem_limit_bytes=VMEM_LIMIT),
        name="inproj",
    )(xb, wb)


def _attn_kernel(q0, q1, q2, k0, k1, k2, v0, v1, v2, bias_ref, o_ref, m_sc, l_sc, acc_sc):
    qs, ks, vs = (q0, q1, q2), (k0, k1, k2), (v0, v1, v2)
    scale = HEAD_DIM ** -0.5
    QB = ATTN_BLOCK

    def rows(start, n, r):
        return pl.ds(start, n, stride=r) if r > 1 else pl.ds(start, n)

    def tile(g, q_start, k_start, nk):
        r = DILATIONS[g]
        q = qs[g][rows(q_start, QB, r), :].astype(bf16)
        k = ks[g][rows(k_start, nk, r), :].astype(bf16)
        v = vs[g][rows(k_start, nk, r), :].astype(bf16)
        bias = bias_ref[g, :, 2 * QB - nk:]
        s = lax.dot_general(q, k, (((1,), (1,)), ((), ())), preferred_element_type=f32)
        s = s * scale + bias
        m = jnp.max(s, axis=-1, keepdims=True)
        p = jnp.exp(s - m)
        l = jnp.sum(p, axis=-1, keepdims=True)
        pv = jnp.dot(p.astype(bf16), v, preferred_element_type=f32)
        return m, l, pv

    def emit(g, q_start, m, l, pv):
        sl = rows(q_start, QB, DILATIONS[g])
        if g == 0:
            m_sc[sl, :] = m
            l_sc[sl, :] = l
            acc_sc[sl, :] = pv
            return
        m_old = m_sc[sl, :]
        m_new = jnp.maximum(m_old, m)
        a = jnp.exp(m_old - m_new)
        b = jnp.exp(m - m_new)
        l_new = a * l_sc[sl, :] + b * l
        acc_new = a * acc_sc[sl, :] + b * pv
        if g == ATTN_GROUPS - 1:
            acc_sc[sl, :] = acc_new / l_new
        else:
            m_sc[sl, :] = m_new
            l_sc[sl, :] = l_new
            acc_sc[sl, :] = acc_new

    for g in range(ATTN_GROUPS):
        r = DILATIONS[g]
        nb = SEQ // r // QB
        for c in range(r):
            emit(g, c, *tile(g, c, c, QB))
            if nb > 1:
                def body(n, carry, g=g, c=c, r=r):
                    q_start = pl.multiple_of(n * (QB * r), QB) + c
                    k_start = pl.multiple_of((n - 1) * (QB * r), QB) + c
                    emit(g, q_start, *tile(g, q_start, k_start, 2 * QB))
                    return carry
                lax.fori_loop(1, nb, body, 0)
    o_ref[...] = acc_sc[...].astype(o_ref.dtype)


def _attention(proj3, bias):
    hb = HEAD_DIM
    def col(kind, g):
        off = (kind * ATTN_GROUPS + g) * ATTN_HEADS
        return pl.BlockSpec((None, SEQ, hb), lambda b, h, off=off: (b, 0, off + h))
    in_specs = [col(kind, g) for kind in range(3) for g in range(ATTN_GROUPS)]
    in_specs.append(pl.BlockSpec((ATTN_GROUPS, None, ATTN_BLOCK, 2 * ATTN_BLOCK),
                                 lambda b, h: (0, h, 0, 0)))
    return pl.pallas_call(
        _attn_kernel,
        grid=(BATCH, ATTN_HEADS),
        in_specs=in_specs,
        out_specs=pl.BlockSpec((None, SEQ, hb), lambda b, h: (b, 0, h)),
        out_shape=jax.ShapeDtypeStruct((BATCH, SEQ, BRANCH_WIDTH), bf16),
        scratch_shapes=[pltpu.VMEM((SEQ, 1), f32), pltpu.VMEM((SEQ, 1), f32),
                        pltpu.VMEM((SEQ, hb), f32)],
        compiler_params=pltpu.CompilerParams(
            dimension_semantics=("arbitrary", "arbitrary"), vmem_limit_bytes=VMEM_LIMIT),
        name="attention",
    )(*([proj3] * 9), bias)


def _t5_bucket(dist):
    exact = NUM_BUCKETS // 2
    d = jnp.maximum(dist, 1).astype(f32)
    large = exact + (jnp.log(d / exact) / math.log(MAX_DISTANCE / exact)
                     * (NUM_BUCKETS - exact)).astype(i32)
    large = jnp.minimum(large, NUM_BUCKETS - 1)
    return jnp.where(dist < exact, dist, large)


def _attention_bias(rel_bias):
    qi = np.arange(ATTN_BLOCK, dtype=np.int32)[:, None]
    kj = np.arange(2 * ATTN_BLOCK, dtype=np.int32)[None, :]
    sub = qi + ATTN_BLOCK - kj
    table = rel_bias.reshape(NUM_BUCKETS, ATTN_GROUPS, ATTN_HEADS)
    out = []
    for g in range(ATTN_GROUPS):
        span = WINDOWS[g] // DILATIONS[g]
        band = (sub >= 0) & (sub <= span)
        bucket = _t5_bucket(jnp.asarray(DILATIONS[g] * np.clip(sub, 0, span), i32))
        b = table[:, g][bucket]
        b = jnp.where(band[:, :, None], b, -1e30)
        out.append(b.transpose(2, 0, 1))
    return jnp.stack(out, axis=0).astype(f32)


SGU_TM = 512


def _sgu_kernel(u_ref, v_ref, g_ref, b_ref, w_ref, bst_ref, o_ref):
    vn = _layer_norm(v_ref[...], g_ref[...], b_ref[...]).astype(bf16)
    ri = lax.broadcasted_iota(i32, (SGU_CHUNK, SGU_CHUNK), 0)
    ci = lax.broadcasted_iota(i32, (SGU_CHUNK, SGU_CHUNK), 1)
    causal = ci <= ri
    for g in range(SGU_GROUPS):
        wg = jnp.where(causal, w_ref[g], 0.0).astype(bf16)
        bias = bst_ref[:, g:g + 1]
        cs = slice(g * SGU_CHUNK, (g + 1) * SGU_CHUNK)
        for c in range(SGU_TM // SGU_CHUNK):
            rs = slice(c * SGU_CHUNK, (c + 1) * SGU_CHUNK)
            mixed = jnp.dot(wg, vn[rs, cs], preferred_element_type=f32) + bias
            o_ref[rs, cs] = (u_ref[rs, cs] * mixed).astype(o_ref.dtype)


def _sgu(proj, ln_g, ln_b, w_s, b_st):
    return pl.pallas_call(
        _sgu_kernel,
        grid=(T // SGU_TM,),
        in_specs=[pl.BlockSpec((SGU_TM, BRANCH_WIDTH), lambda i: (i, COL_U)),
                  pl.BlockSpec((SGU_TM, BRANCH_WIDTH), lambda i: (i, COL_V)),
                  pl.BlockSpec((1, BRANCH_WIDTH), lambda i: (0, 0)),
                  pl.BlockSpec((1, BRANCH_WIDTH), lambda i: (0, 0)),
                  pl.BlockSpec((SGU_GROUPS, SGU_CHUNK, SGU_CHUNK), lambda i: (0, 0, 0)),
                  pl.BlockSpec((SGU_CHUNK, SGU_GROUPS), lambda i: (0, 0))],
        out_specs=pl.BlockSpec((SGU_TM, BRANCH_WIDTH), lambda i: (i, 0)),
        out_shape=jax.ShapeDtypeStruct((T, BRANCH_WIDTH), bf16),
        compiler_params=pltpu.CompilerParams(
            dimension_semantics=("arbitrary",), vmem_limit_bytes=VMEM_LIMIT),
        name="sgu",
    )(proj, proj, ln_g, ln_b, w_s, b_st)


BR_TM = 1024
BR_TN = 1024


def _branch_kernel(ya_ref, ys_ref, ga_ref, gs_ref, wa_ref, ws_ref, o_ref):
    a = jnp.dot(ya_ref[...], wa_ref[...], preferred_element_type=f32)
    s = jnp.dot(ys_ref[...], ws_ref[...], preferred_element_type=f32)
    o_ref[...] = (ga_ref[...] * a + gs_ref[...] * s).astype(o_ref.dtype)


def _branch(ya, ys, proj, wa, ws):
    return pl.pallas_call(
        _branch_kernel,
        grid=(T // BR_TM, D_MODEL // BR_TN),
        in_specs=[pl.BlockSpec((BR_TM, BRANCH_WIDTH), lambda i, j: (i, 0)),
                  pl.BlockSpec((BR_TM, BRANCH_WIDTH), lambda i, j: (i, 0)),
                  pl.BlockSpec((BR_TM, BR_TN), lambda i, j: (i, COL_GA + j)),
                  pl.BlockSpec((BR_TM, BR_TN), lambda i, j: (i, COL_GS + j)),
                  pl.BlockSpec((BRANCH_WIDTH, BR_TN), lambda i, j: (0, j)),
                  pl.BlockSpec((BRANCH_WIDTH, BR_TN), lambda i, j: (0, j))],
        out_specs=pl.BlockSpec((BR_TM, BR_TN), lambda i, j: (i, j)),
        out_shape=jax.ShapeDtypeStruct((T, D_MODEL), bf16),
        compiler_params=pltpu.CompilerParams(
            dimension_semantics=("arbitrary", "arbitrary"), vmem_limit_bytes=VMEM_LIMIT),
        name="branch",
    )(ya, ys, proj, proj, wa, ws)


OL_TM = 256


def _out_ln_router_kernel(y_ref, x_ref, wo_ref, g_ref, b_ref, wr_ref, br_ref,
                          h_ref, idx_ref, gate_ref, cnt_ref):
    i = pl.program_id(0)
    mixed = jnp.dot(y_ref[...], wo_ref[...], preferred_element_type=f32)
    h = _layer_norm(ALPHA * x_ref[...] + mixed, g_ref[...], b_ref[...])
    h_ref[...] = h
    logits = jnp.dot(h, wr_ref[...], preferred_element_type=f32,
                     precision=lax.Precision.HIGHEST) + br_ref[...]
    lane = lax.broadcasted_iota(i32, (OL_TM, N_EXPERTS), 1).astype(f32)
    lane_k = lax.broadcasted_iota(i32, (OL_TM, TOP_K), 1)
    idx_out = jnp.zeros((OL_TM, TOP_K), f32)
    val_out = jnp.zeros((OL_TM, TOP_K), f32)
    picked = jnp.zeros((OL_TM, N_EXPERTS), f32)
    work = logits
    for k in range(TOP_K):
        m = jnp.max(work, axis=-1, keepdims=True)
        idx = jnp.min(jnp.where(work == m, lane, float(N_EXPERTS)), axis=-1, keepdims=True)
        sel = lane == idx
        idx_out = jnp.where(lane_k == k, idx, idx_out)
        val_out = jnp.where(lane_k == k, m, val_out)
        picked = picked + sel.astype(f32)
        work = jnp.where(sel, -jnp.inf, work)
    e = jnp.exp(val_out - jnp.max(val_out, axis=-1, keepdims=True))
    gate_ref[...] = e / jnp.sum(e, axis=-1, keepdims=True)
    idx_ref[...] = idx_out.astype(i32)

    @pl.when(i == 0)
    def _():
        cnt_ref[...] = jnp.zeros_like(cnt_ref)

    cnt_ref[...] += jnp.sum(picked, axis=0, keepdims=True)


def _out_ln_router(y, x2, wo, ln_g, ln_b, w_router, b_router):
    return pl.pallas_call(
        _out_ln_router_kernel,
        grid=(T // OL_TM,),
        in_specs=[pl.BlockSpec((OL_TM, D_MODEL), lambda i: (i, 0)),
                  pl.BlockSpec((OL_TM, D_MODEL), lambda i: (i, 0)),
                  pl.BlockSpec((D_MODEL, D_MODEL), lambda i: (0, 0)),
                  pl.BlockSpec((1, D_MODEL), lambda i: (0, 0)),
                  pl.BlockSpec((1, D_MODEL), lambda i: (0, 0)),
                  pl.BlockSpec((D_MODEL, N_EXPERTS), lambda i: (0, 0)),
                  pl.BlockSpec((1, N_EXPERTS), lambda i: (0, 0))],
        out_specs=[pl.BlockSpec((OL_TM, D_MODEL), lambda i: (i, 0)),
                   pl.BlockSpec((OL_TM, TOP_K), lambda i: (i, 0)),
                   pl.BlockSpec((OL_TM, TOP_K), lambda i: (i, 0)),
                   pl.BlockSpec((1, N_EXPERTS), lambda i: (0, 0))],
        out_shape=[jax.ShapeDtypeStruct((T, D_MODEL), f32),
                   jax.ShapeDtypeStruct((T, TOP_K), i32),
                   jax.ShapeDtypeStruct((T, TOP_K), f32),
                   jax.ShapeDtypeStruct((1, N_EXPERTS), f32)],
        compiler_params=pltpu.CompilerParams(
            dimension_semantics=("arbitrary",), vmem_limit_bytes=VMEM_LIMIT),
        name="out_ln_router",
    )(y, x2, wo, ln_g, ln_b, w_router, b_router)


RK_TM = 512


def _rank_kernel(idx_ref, pstart_ref, dest_ref, carry_sc):
    i = pl.program_id(0)

    @pl.when(i == 0)
    def _():
        carry_sc[...] = jnp.zeros_like(carry_sc)

    idx = idx_ref[...]
    lane = lax.broadcasted_iota(i32, (RK_TM, N_EXPERTS), 1)
    picked = jnp.zeros((RK_TM, N_EXPERTS), f32)
    for k in range(TOP_K):
        picked = picked + (lane == idx[:, k:k + 1]).astype(f32)
    ri = lax.broadcasted_iota(i32, (RK_TM, RK_TM), 0)
    ci = lax.broadcasted_iota(i32, (RK_TM, RK_TM), 1)
    lower = (ci < ri).astype(bf16)
    before = jnp.dot(lower, picked.astype(bf16), preferred_element_type=f32)
    slot = before + carry_sc[...] + pstart_ref[...]
    lane_k = lax.broadcasted_iota(i32, (RK_TM, TOP_K), 1)
    dest = jnp.zeros((RK_TM, TOP_K), f32)
    for k in range(TOP_K):
        d = jnp.sum(jnp.where(lane == idx[:, k:k + 1], slot, 0.0), axis=-1, keepdims=True)
        dest = jnp.where(lane_k == k, d, dest)
    dest_ref[...] = dest.astype(i32)
    carry_sc[...] += jnp.sum(picked, axis=0, keepdims=True)


def _rank(idx, pstart):
    return pl.pallas_call(
        _rank_kernel,
        grid=(T // RK_TM,),
        in_specs=[pl.BlockSpec((RK_TM, TOP_K), lambda i: (i, 0)),
                  pl.BlockSpec((1, N_EXPERTS), lambda i: (0, 0))],
        out_specs=pl.BlockSpec((RK_TM, TOP_K), lambda i: (i, 0)),
        out_shape=jax.ShapeDtypeStruct((T, TOP_K), i32),
        scratch_shapes=[pltpu.VMEM((1, N_EXPERTS), f32)],
        compiler_params=pltpu.CompilerParams(dimension_semantics=("arbitrary",)),
        name="rank",
    )(idx, pstart)


DP_TM = 512


def _dispatch_kernel(dest_ref, h_hbm, xs_in, xs_hbm, sem):
    del xs_in
    base = pl.program_id(0) * DP_TM

    def row_copy(t, k):
        return pltpu.make_async_copy(h_hbm.at[pl.ds(base + t, 1)],
                                     xs_hbm.at[pl.ds(dest_ref[t * TOP_K + k], 1)], sem.at[0])

    def issue(t, carry):
        for k in range(TOP_K):
            row_copy(t, k).start()
        return carry

    def drain(t, carry):
        for k in range(TOP_K):
            row_copy(t, k).wait()
        return carry

    lax.fori_loop(0, DP_TM, issue, 0)
    lax.fori_loop(0, DP_TM, drain, 0)


def _dispatch(h, dest_flat):
    xs0 = jnp.zeros((CAP, D_MODEL), f32)
    return pl.pallas_call(
        _dispatch_kernel,
        grid=(T // DP_TM,),
        in_specs=[pl.BlockSpec((DP_TM * TOP_K,), lambda i: (i,), memory_space=pltpu.SMEM),
                  pl.BlockSpec(memory_space=pl.ANY),
                  pl.BlockSpec(memory_space=pl.ANY)],
        out_specs=pl.BlockSpec(memory_space=pl.ANY),
        out_shape=jax.ShapeDtypeStruct((CAP, D_MODEL), f32),
        scratch_shapes=[pltpu.SemaphoreType.DMA((1,))],
        input_output_aliases={2: 0},
        compiler_params=pltpu.CompilerParams(dimension_semantics=("arbitrary",),
                                             has_side_effects=True),
        name="dispatch",
    )(dest_flat, h, xs0)


GU_TN = 512
DN_TN = 1024


def _block_ids(be_ref, nu_ref):
    i = pl.program_id(1)
    last = nu_ref[0] - 1
    ic = jnp.minimum(i, last)
    new_expert = (i == 0) | (be_ref[ic] != be_ref[jnp.maximum(ic - 1, 0)])
    return i, last, new_expert


def _expert_gu_kernel(be_ref, nu_ref, x_ref, wg_ref, wl_ref, bg_ref, bl_ref, o_ref, wg_sc, wl_sc):
    i, last, new_expert = _block_ids(be_ref, nu_ref)

    @pl.when(i <= last)
    def _():
        @pl.when(new_expert)
        def _():
            wg_sc[...] = wg_ref[...].astype(bf16)
            wl_sc[...] = wl_ref[...].astype(bf16)

        x = x_ref[...].astype(bf16)
        glu = jnp.dot(x, wg_sc[...], preferred_element_type=f32) + bg_ref[...]
        lin = jnp.dot(x, wl_sc[...], preferred_element_type=f32) + bl_ref[...]
        glu = jnp.minimum(glu, SWIGLU_LIMIT)
        lin = jnp.clip(lin, -SWIGLU_LIMIT, SWIGLU_LIMIT)
        o_ref[...] = (glu * jax.nn.sigmoid(SWIGLU_ALPHA * glu) * (lin + 1.0)).astype(o_ref.dtype)

    @pl.when(i > last)
    def _():
        o_ref[...] = jnp.zeros_like(o_ref)


def _expert_gu(block_e, n_used, xs, w_gu, b_gu):
    nj = D_MODEL // GU_TN

    def blk(j, i, be, nu):
        return jnp.minimum(i, nu[0] - 1)

    grid_spec = pltpu.PrefetchScalarGridSpec(
        num_scalar_prefetch=2,
        grid=(nj, NB),
        in_specs=[
            pl.BlockSpec((BM, D_MODEL), lambda j, i, be, nu: (blk(j, i, be, nu), 0)),
            pl.BlockSpec((None, D_MODEL, GU_TN), lambda j, i, be, nu: (be[blk(j, i, be, nu)], 0, j)),
            pl.BlockSpec((None, D_MODEL, GU_TN), lambda j, i, be, nu: (be[blk(j, i, be, nu)], 0, nj + j)),
            pl.BlockSpec((None, 1, GU_TN), lambda j, i, be, nu: (be[blk(j, i, be, nu)], 0, j)),
            pl.BlockSpec((None, 1, GU_TN), lambda j, i, be, nu: (be[blk(j, i, be, nu)], 0, nj + j)),
        ],
        out_specs=pl.BlockSpec((BM, GU_TN), lambda j, i, be, nu: (i, j)),
        scratch_shapes=[pltpu.VMEM((D_MODEL, GU_TN), bf16), pltpu.VMEM((D_MODEL, GU_TN), bf16)],
    )
    return pl.pallas_call(
        _expert_gu_kernel,
        grid_spec=grid_spec,
        out_shape=jax.ShapeDtypeStruct((CAP, D_MODEL), bf16),
        compiler_params=pltpu.CompilerParams(
            dimension_semantics=("arbitrary", "arbitrary"), vmem_limit_bytes=VMEM_LIMIT),
        name="expert_gu",
    )(block_e, n_used, xs, w_gu, w_gu, b_gu, b_gu)


def _expert_down_kernel(be_ref, nu_ref, a_ref, w_ref, b_ref, o_ref, w_sc):
    i, last, new_expert = _block_ids(be_ref, nu_ref)

    @pl.when(i <= last)
    def _():
        @pl.when(new_expert)
        def _():
            w_sc[...] = w_ref[...].astype(bf16)

        o_ref[...] = jnp.dot(a_ref[...], w_sc[...], preferred_element_type=f32) + b_ref[...]

    @pl.when(i > last)
    def _():
        o_ref[...] = jnp.zeros_like(o_ref)


def _expert_down(block_e, n_used, act, w_down, b_down):
    def blk(j, i, be, nu):
        return jnp.minimum(i, nu[0] - 1)

    grid_spec = pltpu.PrefetchScalarGridSpec(
        num_scalar_prefetch=2,
        grid=(D_MODEL // DN_TN, NB),
        in_specs=[
            pl.BlockSpec((BM, D_MODEL), lambda j, i, be, nu: (blk(j, i, be, nu), 0)),
            pl.BlockSpec((None, D_MODEL, DN_TN), lambda j, i, be, nu: (be[blk(j, i, be, nu)], 0, j)),
            pl.BlockSpec((None, 1, DN_TN), lambda j, i, be, nu: (be[blk(j, i, be, nu)], 0, j)),
        ],
        out_specs=pl.BlockSpec((BM, DN_TN), lambda j, i, be, nu: (i, j)),
        scratch_shapes=[pltpu.VMEM((D_MODEL, DN_TN), bf16)],
    )
    return pl.pallas_call(
        _expert_down_kernel,
        grid_spec=grid_spec,
        out_shape=jax.ShapeDtypeStruct((CAP, D_MODEL), f32),
        compiler_params=pltpu.CompilerParams(
            dimension_semantics=("arbitrary", "arbitrary"), vmem_limit_bytes=VMEM_LIMIT),
        name="expert_down",
    )(block_e, n_used, act, w_down, b_down)


CB_TM = 128


def _combine_kernel(dest_ref, y_hbm, gate_ref, h_ref, g_ref, b_ref, o_ref, buf, sem):
    def row_copy(t, k):
        return pltpu.make_async_copy(y_hbm.at[pl.ds(dest_ref[t * TOP_K + k], 1)],
                                     buf.at[k, pl.ds(t, 1)], sem.at[0])

    def issue(t, carry):
        for k in range(TOP_K):
            row_copy(t, k).start()
        return carry

    def drain(t, carry):
        for k in range(TOP_K):
            row_copy(t, k).wait()
        return carry

    lax.fori_loop(0, CB_TM, issue, 0)
    lax.fori_loop(0, CB_TM, drain, 0)
    gate = gate_ref[...]
    ffn = gate[:, 0:1] * buf[0]
    for k in range(1, TOP_K):
        ffn = ffn + gate[:, k:k + 1] * buf[k]
    o_ref[...] = _layer_norm(ALPHA * h_ref[...] + ffn, g_ref[...], b_ref[...])


def _combine(y, dest_flat, gate, h, ln_g, ln_b):
    return pl.pallas_call(
        _combine_kernel,
        grid=(T // CB_TM,),
        in_specs=[pl.BlockSpec((CB_TM * TOP_K,), lambda i: (i,), memory_space=pltpu.SMEM),
                  pl.BlockSpec(memory_space=pl.ANY),
                  pl.BlockSpec((CB_TM, TOP_K), lambda i: (i, 0)),
                  pl.BlockSpec((CB_TM, D_MODEL), lambda i: (i, 0)),
                  pl.BlockSpec((1, D_MODEL), lambda i: (0, 0)),
                  pl.BlockSpec((1, D_MODEL), lambda i: (0, 0))],
        out_specs=pl.BlockSpec((CB_TM, D_MODEL), lambda i: (i, 0)),
        out_shape=jax.ShapeDtypeStruct((T, D_MODEL), f32),
        scratch_shapes=[pltpu.VMEM((TOP_K, CB_TM, D_MODEL), f32),
                        pltpu.SemaphoreType.DMA((1,))],
        compiler_params=pltpu.CompilerParams(dimension_semantics=("arbitrary",)),
        name="combine",
    )(dest_flat, y, gate, h, ln_g, ln_b)


def _expert_layout(counts):
    nblk = (counts + BM - 1) // BM
    bend = jnp.cumsum(nblk)
    pstart = (bend - nblk) * BM
    block_e = jnp.minimum(jnp.searchsorted(bend, jnp.arange(NB, dtype=i32), side='right'),
                          N_EXPERTS - 1).astype(i32)
    return pstart, block_e, bend[-1:].astype(i32)


def kernel(x, w_in, rel_bias, sgu_ln_g, sgu_ln_b, sgu_w, sgu_b, w_branch, w_out, ln1_g, ln1_b,
           w_router, b_router, w_gu, b_gu, w_down, b_down, ln2_g, ln2_b):
    x2 = x.reshape(T, D_MODEL)
    proj = _inproj(x2.astype(bf16), w_in[0].astype(bf16))
    y_attn = _attention(proj.reshape(BATCH, SEQ, IN_COLS), _attention_bias(rel_bias))
    y_sgu = _sgu(proj, sgu_ln_g, sgu_ln_b, sgu_w[0], sgu_b[0].T)
    y = _branch(y_attn.reshape(T, BRANCH_WIDTH), y_sgu, proj,
                w_branch[0, 0].astype(bf16), w_branch[0, 1].astype(bf16))
    h, idx, gate, counts = _out_ln_router(y, x2, w_out[0].astype(bf16), ln1_g, ln1_b,
                                          w_router[0], b_router)
    pstart, block_e, n_used = _expert_layout(counts[0].astype(i32))
    dest = _rank(idx, pstart.astype(f32)[None, :]).reshape(T * TOP_K)
    xs = _dispatch(h, dest)
    act = _expert_gu(block_e, n_used, xs, w_gu.reshape(N_EXPERTS, D_MODEL, 2 * D_MODEL),
                     b_gu.reshape(N_EXPERTS, 1, 2 * D_MODEL))
    ye = _expert_down(block_e, n_used, act, w_down.reshape(N_EXPERTS, D_MODEL, D_MODEL),
                      b_down.reshape(N_EXPERTS, 1, D_MODEL))
    out = _combine(ye, dest, gate, h, ln2_g, ln2_b)
    return out.reshape(BATCH, SEQ, D_MODEL)
```

```python
import functools
import math

import numpy as np
import jax
import jax.numpy as jnp
from jax import lax
from jax.experimental import pallas as pl
from jax.experimental.pallas import tpu as pltpu

D_MODEL = 2048
BATCH = 8
SEQ = 2048
T = BATCH * SEQ
BRANCH_WIDTH = D_MODEL // 2
HEAD_DIM = 128
ATTN_GROUPS = 3
WINDOWS = (128, 512, 2048)
DILATIONS = (1, 4, 16)
ATTN_HEADS = BRANCH_WIDTH // HEAD_DIM
QKV_WIDTH = ATTN_GROUPS * ATTN_HEADS * HEAD_DIM
ATTN_BLOCK = 128
NUM_BUCKETS = 32
MAX_DISTANCE = 2048
SGU_CHUNK = 128
SGU_GROUPS = BRANCH_WIDTH // SGU_CHUNK
IN_COLS = 3 * QKV_WIDTH + 2 * BRANCH_WIDTH + 2 * D_MODEL
N_EXPERTS = 32
TOP_K = 4
SWIGLU_ALPHA = 1.702
SWIGLU_LIMIT = 7.0
LN_EPS = 1e-5
ALPHA = 2.0 ** 0.25

COL_U = 3 * QKV_WIDTH // 1024
COL_V = COL_U + 1
COL_GA = COL_U + 2
COL_GS = COL_GA + 2

BM = 512
NB = (T * TOP_K + N_EXPERTS * (BM - 1) + BM - 1) // BM
CAP = NB * BM

VMEM_LIMIT = 56 * 1024 * 1024

f32 = jnp.float32
bf16 = jnp.bfloat16
i32 = jnp.int32


def _layer_norm(x, g, b):
    mu = jnp.mean(x, axis=-1, keepdims=True)
    xc = x - mu
    var = jnp.mean(xc * xc, axis=-1, keepdims=True)
    return xc * lax.rsqrt(var + LN_EPS) * g + b


IP_TM = 1024
IP_TN = 1024


def _inproj_kernel(x_ref, w_ref, o_ref):
    j = pl.program_id(1)
    acc = jnp.dot(x_ref[...], w_ref[...], preferred_element_type=f32)

    @pl.when(j < COL_U)
    def _():
        o_ref[...] = acc

    @pl.when((j >= COL_U) & (j < COL_GA))
    def _():
        o_ref[...] = jax.nn.gelu(acc)

    @pl.when(j >= COL_GA)
    def _():
        o_ref[...] = jax.nn.sigmoid(acc)


def _inproj(xb, wb):
    return pl.pallas_call(
        _inproj_kernel,
        grid=(T // IP_TM, IN_COLS // IP_TN),
        in_specs=[pl.BlockSpec((IP_TM, D_MODEL), lambda i, j: (i, 0)),
                  pl.BlockSpec((D_MODEL, IP_TN), lambda i, j: (0, j))],
        out_specs=pl.BlockSpec((IP_TM, IP_TN), lambda i, j: (i, j)),
        out_shape=jax.ShapeDtypeStruct((T, IN_COLS), f32),
        compiler_params=pltpu.CompilerParams(
            dimension_semantics=("arbitrary", "arbitrary"), vmem_limit_bytes=VMEM_LIMIT),
        name="inproj",
    )(xb, wb)


def _attn_kernel(q0, q1, q2, k0, k1, k2, v0, v1, v2, bias_ref, o_ref, m_sc, l_sc, acc_sc):
    qs, ks, vs = (q0, q1, q2), (k0, k1, k2), (v0, v1, v2)
    scale = HEAD_DIM ** -0.5
    QB = ATTN_BLOCK

    def rows(start, n, r):
        return pl.ds(start, n, stride=r) if r > 1 else pl.ds(start, n)

    def tile(g, q_start, k_start, nk):
        r = DILATIONS[g]
        q = qs[g][rows(q_start, QB, r), :].astype(bf16)
        k = ks[g][rows(k_start, nk, r), :].astype(bf16)
        v = vs[g][rows(k_start, nk, r), :].astype(bf16)
        bias = bias_ref[g, :, 2 * QB - nk:]
        s = lax.dot_general(q, k, (((1,), (1,)), ((), ())), preferred_element_type=f32)
        s = s * scale + bias
        m = jnp.max(s, axis=-1, keepdims=True)
        p = jnp.exp(s - m)
        l = jnp.sum(p, axis=-1, keepdims=True)
        pv = jnp.dot(p.astype(bf16), v, preferred_element_type=f32)
        return jnp.broadcast_to(m, pv.shape), jnp.broadcast_to(l, pv.shape), pv

    def emit(g, q_start, m, l, pv):
        sl = rows(q_start, QB, DILATIONS[g])
        if g == 0:
            m_sc[sl, :] = m
            l_sc[sl, :] = l
            acc_sc[sl, :] = pv
            return
        m_old = m_sc[sl, :]
        m_new = jnp.maximum(m_old, m)
        a = jnp.exp(m_old - m_new)
        b = jnp.exp(m - m_new)
        l_new = a * l_sc[sl, :] + b * l
        acc_new = a * acc_sc[sl, :] + b * pv
        if g == ATTN_GROUPS - 1:
            acc_sc[sl, :] = acc_new / l_new
        else:
            m_sc[sl, :] = m_new
            l_sc[sl, :] = l_new
            acc_sc[sl, :] = acc_new

    for g in range(ATTN_GROUPS):
        r = DILATIONS[g]
        nb = SEQ // r // QB
        for c in range(r):
            emit(g, c, *tile(g, c, c, QB))
            for n in range(1, nb):
                q_start = c + n * QB * r
                emit(g, q_start, *tile(g, q_start, q_start - QB * r, 2 * QB))
    o_ref[...] = acc_sc[...].astype(o_ref.dtype)


def _attention(proj3, bias):
    hb = HEAD_DIM
    def col(kind, g):
        off = (kind * ATTN_GROUPS + g) * ATTN_HEADS
        return pl.BlockSpec((None, SEQ, hb), lambda b, h, off=off: (b, 0, off + h))
    in_specs = [col(kind, g) for kind in range(3) for g in range(ATTN_GROUPS)]
    in_specs.append(pl.BlockSpec((ATTN_GROUPS, None, ATTN_BLOCK, 2 * ATTN_BLOCK),
                                 lambda b, h: (0, h, 0, 0)))
    return pl.pallas_call(
        _attn_kernel,
        grid=(BATCH, ATTN_HEADS),
        in_specs=in_specs,
        out_specs=pl.BlockSpec((None, SEQ, hb), lambda b, h: (b, 0, h)),
        out_shape=jax.ShapeDtypeStruct((BATCH, SEQ, BRANCH_WIDTH), bf16),
        scratch_shapes=[pltpu.VMEM((SEQ, hb), f32)] * 3,
        compiler_params=pltpu.CompilerParams(
            dimension_semantics=("arbitrary", "arbitrary"), vmem_limit_bytes=VMEM_LIMIT),
        name="attention",
    )(*([proj3] * 9), bias)


def _t5_bucket(dist):
    exact = NUM_BUCKETS // 2
    d = jnp.maximum(dist, 1).astype(f32)
    large = exact + (jnp.log(d / exact) / math.log(MAX_DISTANCE / exact)
                     * (NUM_BUCKETS - exact)).astype(i32)
    large = jnp.minimum(large, NUM_BUCKETS - 1)
    return jnp.where(dist < exact, dist, large)


def _attention_bias(rel_bias):
    qi = np.arange(ATTN_BLOCK, dtype=np.int32)[:, None]
    kj = np.arange(2 * ATTN_BLOCK, dtype=np.int32)[None, :]
    sub = qi + ATTN_BLOCK - kj
    table = rel_bias.reshape(NUM_BUCKETS, ATTN_GROUPS, ATTN_HEADS)
    out = []
    for g in range(ATTN_GROUPS):
        span = WINDOWS[g] // DILATIONS[g]
        band = (sub >= 0) & (sub <= span)
        bucket = _t5_bucket(jnp.asarray(DILATIONS[g] * np.clip(sub, 0, span), i32))
        onehot = jax.nn.one_hot(bucket.reshape(-1), NUM_BUCKETS, dtype=f32)
        b = jnp.dot(onehot, table[:, g], precision=lax.Precision.HIGHEST)
        b = jnp.where(band[:, :, None], b.reshape(ATTN_BLOCK, 2 * ATTN_BLOCK, ATTN_HEADS), -1e30)
        out.append(b.transpose(2, 0, 1))
    return jnp.stack(out, axis=0).astype(f32)


SGU_TM = 512


def _sgu_kernel(u_ref, v_ref, g_ref, b_ref, w_ref, bst_ref, o_ref):
    vn = _layer_norm(v_ref[...], g_ref[...], b_ref[...]).astype(bf16)
    ri = lax.broadcasted_iota(i32, (SGU_CHUNK, SGU_CHUNK), 0)
    ci = lax.broadcasted_iota(i32, (SGU_CHUNK, SGU_CHUNK), 1)
    causal = ci <= ri
    for g in range(SGU_GROUPS):
        wg = jnp.where(causal, w_ref[g], 0.0).astype(bf16)
        bias = bst_ref[:, g:g + 1]
        cs = slice(g * SGU_CHUNK, (g + 1) * SGU_CHUNK)
        for c in range(SGU_TM // SGU_CHUNK):
            rs = slice(c * SGU_CHUNK, (c + 1) * SGU_CHUNK)
            mixed = jnp.dot(wg, vn[rs, cs], preferred_element_type=f32) + bias
            o_ref[rs, cs] = (u_ref[rs, cs] * mixed).astype(o_ref.dtype)


def _sgu(proj, ln_g, ln_b, w_s, b_st):
    return pl.pallas_call(
        _sgu_kernel,
        grid=(T // SGU_TM,),
        in_specs=[pl.BlockSpec((SGU_TM, BRANCH_WIDTH), lambda i: (i, COL_U)),
                  pl.BlockSpec((SGU_TM, BRANCH_WIDTH), lambda i: (i, COL_V)),
                  pl.BlockSpec((1, BRANCH_WIDTH), lambda i: (0, 0)),
                  pl.BlockSpec((1, BRANCH_WIDTH), lambda i: (0, 0)),
                  pl.BlockSpec((SGU_GROUPS, SGU_CHUNK, SGU_CHUNK), lambda i: (0, 0, 0)),
                  pl.BlockSpec((SGU_CHUNK, SGU_GROUPS), lambda i: (0, 0))],
        out_specs=pl.BlockSpec((SGU_TM, BRANCH_WIDTH), lambda i: (i, 0)),
        out_shape=jax.ShapeDtypeStruct((T, BRANCH_WIDTH), bf16),
        compiler_params=pltpu.CompilerParams(
            dimension_semantics=("arbitrary",), vmem_limit_bytes=VMEM_LIMIT),
        name="sgu",
    )(proj, proj, ln_g, ln_b, w_s, b_st)


BR_TM = 1024
BR_TN = 1024


def _branch_kernel(ya_ref, ys_ref, ga_ref, gs_ref, wa_ref, ws_ref, o_ref):
    a = jnp.dot(ya_ref[...], wa_ref[...], preferred_element_type=f32)
    s = jnp.dot(ys_ref[...], ws_ref[...], preferred_element_type=f32)
    o_ref[...] = (ga_ref[...] * a + gs_ref[...] * s).astype(o_ref.dtype)


def _branch(ya, ys, proj, wa, ws):
    return pl.pallas_call(
        _branch_kernel,
        grid=(T // BR_TM, D_MODEL // BR_TN),
        in_specs=[pl.BlockSpec((BR_TM, BRANCH_WIDTH), lambda i, j: (i, 0)),
                  pl.BlockSpec((BR_TM, BRANCH_WIDTH), lambda i, j: (i, 0)),
                  pl.BlockSpec((BR_TM, BR_TN), lambda i, j: (i, COL_GA + j)),
                  pl.BlockSpec((BR_TM, BR_TN), lambda i, j: (i, COL_GS + j)),
                  pl.BlockSpec((BRANCH_WIDTH, BR_TN), lambda i, j: (0, j)),
                  pl.BlockSpec((BRANCH_WIDTH, BR_TN), lambda i, j: (0, j))],
        out_specs=pl.BlockSpec((BR_TM, BR_TN), lambda i, j: (i, j)),
        out_shape=jax.ShapeDtypeStruct((T, D_MODEL), bf16),
        compiler_params=pltpu.CompilerParams(
            dimension_semantics=("arbitrary", "arbitrary"), vmem_limit_bytes=VMEM_LIMIT),
        name="branch",
    )(ya, ys, proj, proj, wa, ws)


OL_TM = 256


def _out_ln_router_kernel(y_ref, x_ref, wo_ref, g_ref, b_ref, wr_ref, br_ref,
                          h_ref, idx_ref, gate_ref, cnt_ref):
    i = pl.program_id(0)
    mixed = jnp.dot(y_ref[...], wo_ref[...], preferred_element_type=f32)
    h = _layer_norm(ALPHA * x_ref[...] + mixed, g_ref[...], b_ref[...])
    h_ref[...] = h
    logits = jnp.dot(h, wr_ref[...], preferred_element_type=f32,
                     precision=lax.Precision.HIGHEST) + br_ref[...]
    lane = lax.broadcasted_iota(i32, (OL_TM, N_EXPERTS), 1).astype(f32)
    lane_k = lax.broadcasted_iota(i32, (OL_TM, TOP_K), 1)
    idx_out = jnp.zeros((OL_TM, TOP_K), f32)
    val_out = jnp.zeros((OL_TM, TOP_K), f32)
    picked = jnp.zeros((OL_TM, N_EXPERTS), f32)
    work = logits
    for k in range(TOP_K):
        m = jnp.max(work, axis=-1, keepdims=True)
        idx = jnp.min(jnp.where(work == m, lane, float(N_EXPERTS)), axis=-1, keepdims=True)
        sel = lane == idx
        idx_out = jnp.where(lane_k == k, idx, idx_out)
        val_out = jnp.where(lane_k == k, m, val_out)
        picked = picked + sel.astype(f32)
        work = jnp.where(sel, -jnp.inf, work)
    e = jnp.exp(val_out - jnp.max(val_out, axis=-1, keepdims=True))
    gate_ref[...] = e / jnp.sum(e, axis=-1, keepdims=True)
    idx_ref[...] = idx_out.astype(i32)

    @pl.when(i == 0)
    def _():
        cnt_ref[...] = jnp.zeros_like(cnt_ref)

    cnt_ref[...] += jnp.sum(picked, axis=0, keepdims=True)


def _out_ln_router(y, x2, wo, ln_g, ln_b, w_router, b_router):
    return pl.pallas_call(
        _out_ln_router_kernel,
        grid=(T // OL_TM,),
        in_specs=[pl.BlockSpec((OL_TM, D_MODEL), lambda i: (i, 0)),
                  pl.BlockSpec((OL_TM, D_MODEL), lambda i: (i, 0)),
                  pl.BlockSpec((D_MODEL, D_MODEL), lambda i: (0, 0)),
                  pl.BlockSpec((1, D_MODEL), lambda i: (0, 0)),
                  pl.BlockSpec((1, D_MODEL), lambda i: (0, 0)),
                  pl.BlockSpec((D_MODEL, N_EXPERTS), lambda i: (0, 0)),
                  pl.BlockSpec((1, N_EXPERTS), lambda i: (0, 0))],
        out_specs=[pl.BlockSpec((OL_TM, D_MODEL), lambda i: (i, 0)),
                   pl.BlockSpec((OL_TM, TOP_K), lambda i: (i, 0)),
                   pl.BlockSpec((OL_TM, TOP_K), lambda i: (i, 0)),
                   pl.BlockSpec((1, N_EXPERTS), lambda i: (0, 0))],
        out_shape=[jax.ShapeDtypeStruct((T, D_MODEL), f32),
                   jax.ShapeDtypeStruct((T, TOP_K), i32),
                   jax.ShapeDtypeStruct((T, TOP_K), f32),
                   jax.ShapeDtypeStruct((1, N_EXPERTS), f32)],
        compiler_params=pltpu.CompilerParams(
            dimension_semantics=("arbitrary",), vmem_limit_bytes=VMEM_LIMIT),
        name="out_ln_router",
    )(y, x2, wo, ln_g, ln_b, w_router, b_router)


RK_TM = 512


def _rank_kernel(idx_ref, pstart_ref, dest_ref, carry_sc):
    i = pl.program_id(0)

    @pl.when(i == 0)
    def _():
        carry_sc[...] = jnp.zeros_like(carry_sc)

    idx = idx_ref[...]
    lane = lax.broadcasted_iota(i32, (RK_TM, N_EXPERTS), 1)
    picked = jnp.zeros((RK_TM, N_EXPERTS), f32)
    for k in range(TOP_K):
        picked = picked + (lane == idx[:, k:k + 1]).astype(f32)
    ri = lax.broadcasted_iota(i32, (RK_TM, RK_TM), 0)
    ci = lax.broadcasted_iota(i32, (RK_TM, RK_TM), 1)
    lower = (ci < ri).astype(bf16)
    before = jnp.dot(lower, picked.astype(bf16), preferred_element_type=f32)
    slot = before + carry_sc[...] + pstart_ref[...]
    lane_k = lax.broadcasted_iota(i32, (RK_TM, TOP_K), 1)
    dest = jnp.zeros((RK_TM, TOP_K), f32)
    for k in range(TOP_K):
        d = jnp.sum(jnp.where(lane == idx[:, k:k + 1], slot, 0.0), axis=-1, keepdims=True)
        dest = jnp.where(lane_k == k, d, dest)
    dest_ref[...] = dest.astype(i32)
    carry_sc[...] += jnp.sum(picked, axis=0, keepdims=True)


def _rank(idx, pstart):
    return pl.pallas_call(
        _rank_kernel,
        grid=(T // RK_TM,),
        in_specs=[pl.BlockSpec((RK_TM, TOP_K), lambda i: (i, 0)),
                  pl.BlockSpec((1, N_EXPERTS), lambda i: (0, 0))],
        out_specs=pl.BlockSpec((RK_TM, TOP_K), lambda i: (i, 0)),
        out_shape=jax.ShapeDtypeStruct((T, TOP_K), i32),
        scratch_shapes=[pltpu.VMEM((1, N_EXPERTS), f32)],
        compiler_params=pltpu.CompilerParams(dimension_semantics=("arbitrary",)),
        name="rank",
    )(idx, pstart)


DP_TM = 512


def _dispatch_kernel(dest_ref, h_ref, xs_in, xs_hbm, sem):
    del xs_in

    def row_copy(t, k):
        return pltpu.make_async_copy(h_ref.at[pl.ds(t, 1)],
                                     xs_hbm.at[pl.ds(dest_ref[t * TOP_K + k], 1)], sem.at[0])

    def issue(t, carry):
        for k in range(TOP_K):
            row_copy(t, k).start()
        return carry

    def drain(t, carry):
        for k in range(TOP_K):
            row_copy(t, k).wait()
        return carry

    lax.fori_loop(0, DP_TM, issue, 0)
    lax.fori_loop(0, DP_TM, drain, 0)


def _dispatch(h, dest_flat):
    xs0 = jnp.zeros((CAP, D_MODEL), f32)
    return pl.pallas_call(
        _dispatch_kernel,
        grid=(T // DP_TM,),
        in_specs=[pl.BlockSpec((DP_TM * TOP_K,), lambda i: (i,), memory_space=pltpu.SMEM),
                  pl.BlockSpec((DP_TM, D_MODEL), lambda i: (i, 0)),
                  pl.BlockSpec(memory_space=pl.ANY)],
        out_specs=pl.BlockSpec(memory_space=pl.ANY),
        out_shape=jax.ShapeDtypeStruct((CAP, D_MODEL), f32),
        scratch_shapes=[pltpu.SemaphoreType.DMA((1,))],
        input_output_aliases={2: 0},
        compiler_params=pltpu.CompilerParams(dimension_semantics=("arbitrary",)),
        name="dispatch",
    )(dest_flat, h, xs0)


GU_TN = 512
DN_TN = 1024


def _block_ids(be_ref, nu_ref):
    i = pl.program_id(1)
    last = nu_ref[0] - 1
    ic = jnp.minimum(i, last)
    new_expert = (i == 0) | (be_ref[ic] != be_ref[jnp.maximum(ic - 1, 0)])
    return i, last, new_expert


def _expert_gu_kernel(be_ref, nu_ref, x_ref, wg_ref, wl_ref, bg_ref, bl_ref, o_ref, wg_sc, wl_sc):
    i, last, new_expert = _block_ids(be_ref, nu_ref)

    @pl.when(i <= last)
    def _():
        @pl.when(new_expert)
        def _():
            wg_sc[...] = wg_ref[...].astype(bf16)
            wl_sc[...] = wl_ref[...].astype(bf16)

        x = x_ref[...].astype(bf16)
        glu = jnp.dot(x, wg_sc[...], preferred_element_type=f32) + bg_ref[...]
        lin = jnp.dot(x, wl_sc[...], preferred_element_type=f32) + bl_ref[...]
        glu = jnp.minimum(glu, SWIGLU_LIMIT)
        lin = jnp.clip(lin, -SWIGLU_LIMIT, SWIGLU_LIMIT)
        o_ref[...] = (glu * jax.nn.sigmoid(SWIGLU_ALPHA * glu) * (lin + 1.0)).astype(o_ref.dtype)

    @pl.when(i > last)
    def _():
        o_ref[...] = jnp.zeros_like(o_ref)


def _expert_gu(block_e, n_used, xs, w_gu, b_gu):
    nj = D_MODEL // GU_TN

    def blk(j, i, be, nu):
        return jnp.minimum(i, nu[0] - 1)

    grid_spec = pltpu.PrefetchScalarGridSpec(
        num_scalar_prefetch=2,
        grid=(nj, NB),
        in_specs=[
            pl.BlockSpec((BM, D_MODEL), lambda j, i, be, nu: (blk(j, i, be, nu), 0)),
            pl.BlockSpec((None, D_MODEL, GU_TN), lambda j, i, be, nu: (be[blk(j, i, be, nu)], 0, j)),
            pl.BlockSpec((None, D_MODEL, GU_TN), lambda j, i, be, nu: (be[blk(j, i, be, nu)], 0, nj + j)),
            pl.BlockSpec((None, 1, GU_TN), lambda j, i, be, nu: (be[blk(j, i, be, nu)], 0, j)),
            pl.BlockSpec((None, 1, GU_TN), lambda j, i, be, nu: (be[blk(j, i, be, nu)], 0, nj + j)),
        ],
        out_specs=pl.BlockSpec((BM, GU_TN), lambda j, i, be, nu: (i, j)),
        scratch_shapes=[pltpu.VMEM((D_MODEL, GU_TN), bf16), pltpu.VMEM((D_MODEL, GU_TN), bf16)],
    )
    return pl.pallas_call(
        _expert_gu_kernel,
        grid_spec=grid_spec,
        out_shape=jax.ShapeDtypeStruct((CAP, D_MODEL), bf16),
        compiler_params=pltpu.CompilerParams(
            dimension_semantics=("arbitrary", "arbitrary"), vmem_limit_bytes=VMEM_LIMIT),
        name="expert_gu",
    )(block_e, n_used, xs, w_gu, w_gu, b_gu, b_gu)


def _expert_down_kernel(be_ref, nu_ref, a_ref, w_ref, b_ref, o_ref, w_sc):
    i, last, new_expert = _block_ids(be_ref, nu_ref)

    @pl.when(i <= last)
    def _():
        @pl.when(new_expert)
        def _():
            w_sc[...] = w_ref[...].astype(bf16)

        o_ref[...] = jnp.dot(a_ref[...], w_sc[...], preferred_element_type=f32) + b_ref[...]

    @pl.when(i > last)
    def _():
        o_ref[...] = jnp.zeros_like(o_ref)


def _expert_down(block_e, n_used, act, w_down, b_down):
    def blk(j, i, be, nu):
        return jnp.minimum(i, nu[0] - 1)

    grid_spec = pltpu.PrefetchScalarGridSpec(
        num_scalar_prefetch=2,
        grid=(D_MODEL // DN_TN, NB),
        in_specs=[
            pl.BlockSpec((BM, D_MODEL), lambda j, i, be, nu: (blk(j, i, be, nu), 0)),
            pl.BlockSpec((None, D_MODEL, DN_TN), lambda j, i, be, nu: (be[blk(j, i, be, nu)], 0, j)),
            pl.BlockSpec((None, 1, DN_TN), lambda j, i, be, nu: (be[blk(j, i, be, nu)], 0, j)),
        ],
        out_specs=pl.BlockSpec((BM, DN_TN), lambda j, i, be, nu: (i, j)),
        scratch_shapes=[pltpu.VMEM((D_MODEL, DN_TN), bf16)],
    )
    return pl.pallas_call(
        _expert_down_kernel,
        grid_spec=grid_spec,
        out_shape=jax.ShapeDtypeStruct((CAP, D_MODEL), f32),
        compiler_params=pltpu.CompilerParams(
            dimension_semantics=("arbitrary", "arbitrary"), vmem_limit_bytes=VMEM_LIMIT),
        name="expert_down",
    )(block_e, n_used, act, w_down, b_down)


CB_TM = 128


def _combine_kernel(dest_ref, y_hbm, gate_ref, h_ref, g_ref, b_ref, o_ref, buf, sem):
    def row_copy(t, k):
        return pltpu.make_async_copy(y_hbm.at[pl.ds(dest_ref[t * TOP_K + k], 1)],
                                     buf.at[k, pl.ds(t, 1)], sem.at[0])

    def issue(t, carry):
        for k in range(TOP_K):
            row_copy(t, k).start()
        return carry

    def drain(t, carry):
        for k in range(TOP_K):
            row_copy(t, k).wait()
        return carry

    lax.fori_loop(0, CB_TM, issue, 0)
    lax.fori_loop(0, CB_TM, drain, 0)
    gate = gate_ref[...]
    ffn = gate[:, 0:1] * buf[0]
    for k in range(1, TOP_K):
        ffn = ffn + gate[:, k:k + 1] * buf[k]
    o_ref[...] = _layer_norm(ALPHA * h_ref[...] + ffn, g_ref[...], b_ref[...])


def _combine(y, dest_flat, gate, h, ln_g, ln_b):
    return pl.pallas_call(
        _combine_kernel,
        grid=(T // CB_TM,),
        in_specs=[pl.BlockSpec((CB_TM * TOP_K,), lambda i: (i,), memory_space=pltpu.SMEM),
                  pl.BlockSpec(memory_space=pl.ANY),
                  pl.BlockSpec((CB_TM, TOP_K), lambda i: (i, 0)),
                  pl.BlockSpec((CB_TM, D_MODEL), lambda i: (i, 0)),
                  pl.BlockSpec((1, D_MODEL), lambda i: (0, 0)),
                  pl.BlockSpec((1, D_MODEL), lambda i: (0, 0))],
        out_specs=pl.BlockSpec((CB_TM, D_MODEL), lambda i: (i, 0)),
        out_shape=jax.ShapeDtypeStruct((T, D_MODEL), f32),
        scratch_shapes=[pltpu.VMEM((TOP_K, CB_TM, D_MODEL), f32),
                        pltpu.SemaphoreType.DMA((1,))],
        compiler_params=pltpu.CompilerParams(dimension_semantics=("arbitrary",)),
        name="combine",
    )(dest_flat, y, gate, h, ln_g, ln_b)


def _expert_layout(counts):
    nblk = (counts + BM - 1) // BM
    bend = jnp.cumsum(nblk)
    pstart = (bend - nblk) * BM
    owner = jnp.sum((bend[None, :] <= jnp.arange(NB, dtype=i32)[:, None]).astype(i32), axis=1)
    block_e = jnp.minimum(owner, N_EXPERTS - 1)
    return pstart, block_e, bend[-1:].astype(i32)


def kernel(x, w_in, rel_bias, sgu_ln_g, sgu_ln_b, sgu_w, sgu_b, w_branch, w_out, ln1_g, ln1_b,
           w_router, b_router, w_gu, b_gu, w_down, b_down, ln2_g, ln2_b):
    x2 = x.reshape(T, D_MODEL)
    proj = _inproj(x2.astype(bf16), w_in[0].astype(bf16))
    y_attn = _attention(proj.reshape(BATCH, SEQ, IN_COLS), _attention_bias(rel_bias))
    y_sgu = _sgu(proj, sgu_ln_g, sgu_ln_b, sgu_w[0], sgu_b[0].T)
    y = _branch(y_attn.reshape(T, BRANCH_WIDTH), y_sgu, proj,
                w_branch[0, 0].astype(bf16), w_branch[0, 1].astype(bf16))
    h, idx, gate, counts = _out_ln_router(y, x2, w_out[0].astype(bf16), ln1_g, ln1_b,
                                          w_router[0], b_router)
    pstart, block_e, n_used = _expert_layout(counts[0].astype(i32))
    dest = _rank(idx, pstart.astype(f32)[None, :]).reshape(T * TOP_K)
    xs = _dispatch(h, dest)
    act = _expert_gu(block_e, n_used, xs, w_gu.reshape(N_EXPERTS, D_MODEL, 2 * D_MODEL),
                     b_gu.reshape(N_EXPERTS, 1, 2 * D_MODEL))
    ye = _expert_down(block_e, n_used, act, w_down.reshape(N_EXPERTS, D_MODEL, D_MODEL),
                      b_down.reshape(N_EXPERTS, 1, D_MODEL))
    out = _combine(ye, dest, gate, h, ln2_g, ln2_b)
    return out.reshape(BATCH, SEQ, D_MODEL)
```

```python
import functools
import math

import numpy as np
import jax
import jax.numpy as jnp
from jax import lax
from jax.experimental import pallas as pl
from jax.experimental.pallas import tpu as pltpu

D_MODEL = 2048
BATCH = 8
SEQ = 2048
T = BATCH * SEQ
BRANCH_WIDTH = D_MODEL // 2
HEAD_DIM = 128
ATTN_GROUPS = 3
WINDOWS = (128, 512, 2048)
DILATIONS = (1, 4, 16)
ATTN_HEADS = BRANCH_WIDTH // HEAD_DIM
QKV_WIDTH = ATTN_GROUPS * ATTN_HEADS * HEAD_DIM
ATTN_BLOCK = 128
NUM_BUCKETS = 32
MAX_DISTANCE = 2048
SGU_CHUNK = 128
SGU_GROUPS = BRANCH_WIDTH // SGU_CHUNK
IN_COLS = 3 * QKV_WIDTH + 2 * BRANCH_WIDTH + 2 * D_MODEL
N_EXPERTS = 32
TOP_K = 4
SWIGLU_ALPHA = 1.702
SWIGLU_LIMIT = 7.0
LN_EPS = 1e-5
ALPHA = 2.0 ** 0.25

COL_U = 3 * QKV_WIDTH // 1024
COL_V = COL_U + 1
COL_GA = COL_U + 2
COL_GS = COL_GA + 2

BM = 512
NB = (T * TOP_K + N_EXPERTS * (BM - 1) + BM - 1) // BM
CAP = NB * BM

VMEM_LIMIT = 56 * 1024 * 1024

f32 = jnp.float32
bf16 = jnp.bfloat16
i32 = jnp.int32


def _layer_norm(x, g, b):
    mu = jnp.mean(x, axis=-1, keepdims=True)
    xc = x - mu
    var = jnp.mean(xc * xc, axis=-1, keepdims=True)
    return xc * lax.rsqrt(var + LN_EPS) * g + b


IP_TM = 1024
IP_TN = 1024


IP_CHUNK = 256


def _inproj_kernel(x_ref, w_ref, o_ref, xb_sc):
    j = pl.program_id(1)

    @pl.when(j == 0)
    def _():
        xb_sc[...] = x_ref[...].astype(bf16)

    def project(epilogue):
        for c in range(0, IP_TN, IP_CHUNK):
            acc = jnp.dot(xb_sc[...], w_ref[:, c:c + IP_CHUNK], preferred_element_type=f32)
            o_ref[:, c:c + IP_CHUNK] = epilogue(acc)

    @pl.when(j < COL_U)
    def _():
        project(lambda a: a)

    @pl.when((j >= COL_U) & (j < COL_GA))
    def _():
        project(jax.nn.gelu)

    @pl.when(j >= COL_GA)
    def _():
        project(jax.nn.sigmoid)


def _inproj(x2, wb):
    return pl.pallas_call(
        _inproj_kernel,
        grid=(T // IP_TM, IN_COLS // IP_TN),
        in_specs=[pl.BlockSpec((IP_TM, D_MODEL), lambda i, j: (i, 0)),
                  pl.BlockSpec((D_MODEL, IP_TN), lambda i, j: (0, j))],
        out_specs=pl.BlockSpec((IP_TM, IP_TN), lambda i, j: (i, j)),
        out_shape=jax.ShapeDtypeStruct((T, IN_COLS), f32),
        scratch_shapes=[pltpu.VMEM((IP_TM, D_MODEL), bf16)],
        compiler_params=pltpu.CompilerParams(
            dimension_semantics=("arbitrary", "arbitrary"), vmem_limit_bytes=VMEM_LIMIT),
        name="inproj",
    )(x2, wb)


def _attn_kernel(q0, q1, q2, k0, k1, k2, v0, v1, v2, bias_ref, o_ref, m_sc, l_sc, acc_sc):
    qs, ks, vs = (q0, q1, q2), (k0, k1, k2), (v0, v1, v2)
    scale = HEAD_DIM ** -0.5
    QB = ATTN_BLOCK

    def rows(start, n, r):
        return pl.ds(start, n, stride=r) if r > 1 else pl.ds(start, n)

    def tile(g, q_start, k_start, nk):
        r = DILATIONS[g]
        q = qs[g][rows(q_start, QB, r), :].astype(bf16)
        k = ks[g][rows(k_start, nk, r), :].astype(bf16)
        v = vs[g][rows(k_start, nk, r), :].astype(bf16)
        bias = bias_ref[g, :, 2 * QB - nk:]
        s = lax.dot_general(q, k, (((1,), (1,)), ((), ())), preferred_element_type=f32)
        s = s * scale + bias
        m = jnp.max(s, axis=-1, keepdims=True)
        p = jnp.exp(s - m)
        l = jnp.sum(p, axis=-1, keepdims=True)
        pv = jnp.dot(p.astype(bf16), v, preferred_element_type=f32)
        return jnp.broadcast_to(m, pv.shape), jnp.broadcast_to(l, pv.shape), pv

    def emit(g, q_start, m, l, pv):
        sl = rows(q_start, QB, DILATIONS[g])
        if g == 0:
            m_sc[sl, :] = m
            l_sc[sl, :] = l
            acc_sc[sl, :] = pv
            return
        m_old = m_sc[sl, :]
        m_new = jnp.maximum(m_old, m)
        a = jnp.exp(m_old - m_new)
        b = jnp.exp(m - m_new)
        l_new = a * l_sc[sl, :] + b * l
        acc_new = a * acc_sc[sl, :] + b * pv
        if g == ATTN_GROUPS - 1:
            acc_sc[sl, :] = acc_new / l_new
        else:
            m_sc[sl, :] = m_new
            l_sc[sl, :] = l_new
            acc_sc[sl, :] = acc_new

    for g in range(ATTN_GROUPS):
        r = DILATIONS[g]
        nb = SEQ // r // QB
        for c in range(r):
            emit(g, c, *tile(g, c, c, QB))
            for n in range(1, nb):
                q_start = c + n * QB * r
                emit(g, q_start, *tile(g, q_start, q_start - QB * r, 2 * QB))
    o_ref[...] = acc_sc[...].astype(o_ref.dtype)


def _attention(proj3, bias):
    hb = HEAD_DIM
    def col(kind, g):
        off = (kind * ATTN_GROUPS + g) * ATTN_HEADS
        return pl.BlockSpec((None, SEQ, hb), lambda b, h, off=off: (b, 0, off + h))
    in_specs = [col(kind, g) for kind in range(3) for g in range(ATTN_GROUPS)]
    in_specs.append(pl.BlockSpec((ATTN_GROUPS, None, ATTN_BLOCK, 2 * ATTN_BLOCK),
                                 lambda b, h: (0, h, 0, 0)))
    return pl.pallas_call(
        _attn_kernel,
        grid=(BATCH, ATTN_HEADS),
        in_specs=in_specs,
        out_specs=pl.BlockSpec((None, SEQ, hb), lambda b, h: (b, 0, h)),
        out_shape=jax.ShapeDtypeStruct((BATCH, SEQ, BRANCH_WIDTH), bf16),
        scratch_shapes=[pltpu.VMEM((SEQ, hb), f32)] * 3,
        compiler_params=pltpu.CompilerParams(
            dimension_semantics=("arbitrary", "arbitrary"), vmem_limit_bytes=VMEM_LIMIT),
        name="attention",
    )(*([proj3] * 9), bias)


def _t5_bucket(dist):
    exact = NUM_BUCKETS // 2
    d = jnp.maximum(dist, 1).astype(f32)
    large = exact + (jnp.log(d / exact) / math.log(MAX_DISTANCE / exact)
                     * (NUM_BUCKETS - exact)).astype(i32)
    large = jnp.minimum(large, NUM_BUCKETS - 1)
    return jnp.where(dist < exact, dist, large)


def _attention_bias(rel_bias):
    qi = np.arange(ATTN_BLOCK, dtype=np.int32)[:, None]
    kj = np.arange(2 * ATTN_BLOCK, dtype=np.int32)[None, :]
    sub = qi + ATTN_BLOCK - kj
    table = rel_bias.reshape(NUM_BUCKETS, ATTN_GROUPS, ATTN_HEADS)
    out = []
    for g in range(ATTN_GROUPS):
        span = WINDOWS[g] // DILATIONS[g]
        band = (sub >= 0) & (sub <= span)
        bucket = _t5_bucket(jnp.asarray(DILATIONS[g] * np.clip(sub, 0, span), i32))
        onehot = jax.nn.one_hot(bucket.reshape(-1), NUM_BUCKETS, dtype=f32)
        b = jnp.dot(onehot, table[:, g], precision=lax.Precision.HIGHEST)
        b = jnp.where(band[:, :, None], b.reshape(ATTN_BLOCK, 2 * ATTN_BLOCK, ATTN_HEADS), -1e30)
        out.append(b.transpose(2, 0, 1))
    return jnp.stack(out, axis=0).astype(f32)


SGU_TM = 512


def _sgu_kernel(u_ref, v_ref, g_ref, b_ref, w_ref, bst_ref, o_ref):
    vn = _layer_norm(v_ref[...], g_ref[...], b_ref[...]).astype(bf16)
    ri = lax.broadcasted_iota(i32, (SGU_CHUNK, SGU_CHUNK), 0)
    ci = lax.broadcasted_iota(i32, (SGU_CHUNK, SGU_CHUNK), 1)
    causal = ci <= ri
    for g in range(SGU_GROUPS):
        wg = jnp.where(causal, w_ref[g], 0.0).astype(bf16)
        bias = bst_ref[:, g:g + 1]
        cs = slice(g * SGU_CHUNK, (g + 1) * SGU_CHUNK)
        for c in range(SGU_TM // SGU_CHUNK):
            rs = slice(c * SGU_CHUNK, (c + 1) * SGU_CHUNK)
            mixed = jnp.dot(wg, vn[rs, cs], preferred_element_type=f32) + bias
            o_ref[rs, cs] = (u_ref[rs, cs] * mixed).astype(o_ref.dtype)


def _sgu(proj, ln_g, ln_b, w_s, b_st):
    return pl.pallas_call(
        _sgu_kernel,
        grid=(T // SGU_TM,),
        in_specs=[pl.BlockSpec((SGU_TM, BRANCH_WIDTH), lambda i: (i, COL_U)),
                  pl.BlockSpec((SGU_TM, BRANCH_WIDTH), lambda i: (i, COL_V)),
                  pl.BlockSpec((1, BRANCH_WIDTH), lambda i: (0, 0)),
                  pl.BlockSpec((1, BRANCH_WIDTH), lambda i: (0, 0)),
                  pl.BlockSpec((SGU_GROUPS, SGU_CHUNK, SGU_CHUNK), lambda i: (0, 0, 0)),
                  pl.BlockSpec((SGU_CHUNK, SGU_GROUPS), lambda i: (0, 0))],
        out_specs=pl.BlockSpec((SGU_TM, BRANCH_WIDTH), lambda i: (i, 0)),
        out_shape=jax.ShapeDtypeStruct((T, BRANCH_WIDTH), bf16),
        compiler_params=pltpu.CompilerParams(
            dimension_semantics=("arbitrary",), vmem_limit_bytes=VMEM_LIMIT),
        name="sgu",
    )(proj, proj, ln_g, ln_b, w_s, b_st)


BR_TM = 1024
BR_TN = 1024


def _branch_kernel(ya_ref, ys_ref, ga_ref, gs_ref, wa_ref, ws_ref, o_ref):
    a = jnp.dot(ya_ref[...], wa_ref[...], preferred_element_type=f32)
    s = jnp.dot(ys_ref[...], ws_ref[...], preferred_element_type=f32)
    o_ref[...] = (ga_ref[...] * a + gs_ref[...] * s).astype(o_ref.dtype)


def _branch(ya, ys, proj, wa, ws):
    return pl.pallas_call(
        _branch_kernel,
        grid=(T // BR_TM, D_MODEL // BR_TN),
        in_specs=[pl.BlockSpec((BR_TM, BRANCH_WIDTH), lambda i, j: (i, 0)),
                  pl.BlockSpec((BR_TM, BRANCH_WIDTH), lambda i, j: (i, 0)),
                  pl.BlockSpec((BR_TM, BR_TN), lambda i, j: (i, COL_GA + j)),
                  pl.BlockSpec((BR_TM, BR_TN), lambda i, j: (i, COL_GS + j)),
                  pl.BlockSpec((BRANCH_WIDTH, BR_TN), lambda i, j: (0, j)),
                  pl.BlockSpec((BRANCH_WIDTH, BR_TN), lambda i, j: (0, j))],
        out_specs=pl.BlockSpec((BR_TM, BR_TN), lambda i, j: (i, j)),
        out_shape=jax.ShapeDtypeStruct((T, D_MODEL), bf16),
        compiler_params=pltpu.CompilerParams(
            dimension_semantics=("arbitrary", "arbitrary"), vmem_limit_bytes=VMEM_LIMIT),
        name="branch",
    )(ya, ys, proj, proj, wa, ws)


OL_TM = 512


def _out_ln_router_kernel(y_ref, x_ref, wo_ref, g_ref, b_ref, wr_ref, br_ref,
                          h_ref, idx_ref, gate_ref, cnt_ref):
    i = pl.program_id(0)
    mixed = jnp.dot(y_ref[...], wo_ref[...], preferred_element_type=f32)
    h = _layer_norm(ALPHA * x_ref[...] + mixed, g_ref[...], b_ref[...])
    h_ref[...] = h
    h_hi = h.astype(bf16)
    h_lo = (h - h_hi.astype(f32)).astype(bf16)
    t = jnp.dot(h_hi, wr_ref[...], preferred_element_type=f32)
    u = jnp.dot(h_lo, wr_ref[:, :N_EXPERTS], preferred_element_type=f32)
    logits = t[:, :N_EXPERTS] + (t[:, N_EXPERTS:] + u) + br_ref[...]
    lane = lax.broadcasted_iota(i32, (OL_TM, N_EXPERTS), 1).astype(f32)
    lane_k = lax.broadcasted_iota(i32, (OL_TM, TOP_K), 1)
    idx_out = jnp.zeros((OL_TM, TOP_K), f32)
    val_out = jnp.zeros((OL_TM, TOP_K), f32)
    picked = jnp.zeros((OL_TM, N_EXPERTS), f32)
    work = logits
    for k in range(TOP_K):
        m = jnp.max(work, axis=-1, keepdims=True)
        idx = jnp.min(jnp.where(work == m, lane, float(N_EXPERTS)), axis=-1, keepdims=True)
        sel = lane == idx
        idx_out = jnp.where(lane_k == k, idx, idx_out)
        val_out = jnp.where(lane_k == k, m, val_out)
        picked = picked + sel.astype(f32)
        work = jnp.where(sel, -jnp.inf, work)
    e = jnp.exp(val_out - jnp.max(val_out, axis=-1, keepdims=True))
    gate_ref[...] = e / jnp.sum(e, axis=-1, keepdims=True)
    idx_ref[...] = idx_out.astype(i32)

    @pl.when(i == 0)
    def _():
        cnt_ref[...] = jnp.zeros_like(cnt_ref)

    cnt_ref[...] += jnp.sum(picked, axis=0, keepdims=True)


def _out_ln_router(y, x2, wo, ln_g, ln_b, w_router, b_router):
    return pl.pallas_call(
        _out_ln_router_kernel,
        grid=(T // OL_TM,),
        in_specs=[pl.BlockSpec((OL_TM, D_MODEL), lambda i: (i, 0)),
                  pl.BlockSpec((OL_TM, D_MODEL), lambda i: (i, 0)),
                  pl.BlockSpec((D_MODEL, D_MODEL), lambda i: (0, 0)),
                  pl.BlockSpec((1, D_MODEL), lambda i: (0, 0)),
                  pl.BlockSpec((1, D_MODEL), lambda i: (0, 0)),
                  pl.BlockSpec((D_MODEL, 2 * N_EXPERTS), lambda i: (0, 0)),
                  pl.BlockSpec((1, N_EXPERTS), lambda i: (0, 0))],
        out_specs=[pl.BlockSpec((OL_TM, D_MODEL), lambda i: (i, 0)),
                   pl.BlockSpec((OL_TM, TOP_K), lambda i: (i, 0)),
                   pl.BlockSpec((OL_TM, TOP_K), lambda i: (i, 0)),
                   pl.BlockSpec((1, N_EXPERTS), lambda i: (0, 0))],
        out_shape=[jax.ShapeDtypeStruct((T, D_MODEL), f32),
                   jax.ShapeDtypeStruct((T, TOP_K), i32),
                   jax.ShapeDtypeStruct((T, TOP_K), f32),
                   jax.ShapeDtypeStruct((1, N_EXPERTS), f32)],
        compiler_params=pltpu.CompilerParams(
            dimension_semantics=("arbitrary",), vmem_limit_bytes=VMEM_LIMIT),
        name="out_ln_router",
    )(y, x2, wo, ln_g, ln_b, w_router, b_router)


RK_TM = 512


def _rank_kernel(idx_ref, pstart_ref, dest_ref, carry_sc):
    i = pl.program_id(0)

    @pl.when(i == 0)
    def _():
        carry_sc[...] = jnp.zeros_like(carry_sc)

    idx = idx_ref[...]
    lane = lax.broadcasted_iota(i32, (RK_TM, N_EXPERTS), 1)
    picked = jnp.zeros((RK_TM, N_EXPERTS), f32)
    for k in range(TOP_K):
        picked = picked + (lane == idx[:, k:k + 1]).astype(f32)
    ri = lax.broadcasted_iota(i32, (RK_TM, RK_TM), 0)
    ci = lax.broadcasted_iota(i32, (RK_TM, RK_TM), 1)
    lower = (ci < ri).astype(bf16)
    before = jnp.dot(lower, picked.astype(bf16), preferred_element_type=f32)
    slot = before + carry_sc[...] + pstart_ref[...]
    lane_k = lax.broadcasted_iota(i32, (RK_TM, TOP_K), 1)
    dest = jnp.zeros((RK_TM, TOP_K), f32)
    for k in range(TOP_K):
        d = jnp.sum(jnp.where(lane == idx[:, k:k + 1], slot, 0.0), axis=-1, keepdims=True)
        dest = jnp.where(lane_k == k, d, dest)
    dest_ref[...] = dest.astype(i32)
    carry_sc[...] += jnp.sum(picked, axis=0, keepdims=True)


def _rank(idx, pstart):
    return pl.pallas_call(
        _rank_kernel,
        grid=(T // RK_TM,),
        in_specs=[pl.BlockSpec((RK_TM, TOP_K), lambda i: (i, 0)),
                  pl.BlockSpec((1, N_EXPERTS), lambda i: (0, 0))],
        out_specs=pl.BlockSpec((RK_TM, TOP_K), lambda i: (i, 0)),
        out_shape=jax.ShapeDtypeStruct((T, TOP_K), i32),
        scratch_shapes=[pltpu.VMEM((1, N_EXPERTS), f32)],
        compiler_params=pltpu.CompilerParams(dimension_semantics=("arbitrary",)),
        name="rank",
    )(idx, pstart)


DP_TM = 256
HALF = D_MODEL // 2


def _pack_bf16_pairs(x):
    bits = lax.bitcast_convert_type(x.astype(bf16).astype(f32), i32)
    return bits[:, HALF:] | lax.shift_right_logical(bits[:, :HALF], 16)


def _unpack_bf16_pairs(w):
    lo = lax.bitcast_convert_type(lax.shift_left(w, 16), f32)
    hi = lax.bitcast_convert_type(w & jnp.int32(-65536), f32)
    return jnp.concatenate([lo, hi], axis=1).astype(bf16)


def _dispatch_kernel(dest_ref, h_ref, xs_in, xs_hbm, pack_sc, sem):
    del xs_in
    pack_sc[...] = _pack_bf16_pairs(h_ref[...])

    def issue(t, carry):
        for k in range(TOP_K):
            pltpu.make_async_copy(pack_sc.at[pl.ds(t, 1)],
                                  xs_hbm.at[pl.ds(dest_ref[t * TOP_K + k], 1)], sem.at[0]).start()
        return carry

    lax.fori_loop(0, DP_TM, issue, 0)
    for _ in range(TOP_K):
        pltpu.make_async_copy(pack_sc, xs_hbm.at[pl.ds(0, DP_TM)], sem.at[0]).wait()


def _dispatch(h, dest_flat):
    xs0 = jnp.zeros((CAP, HALF), i32)
    return pl.pallas_call(
        _dispatch_kernel,
        grid=(T // DP_TM,),
        in_specs=[pl.BlockSpec((DP_TM * TOP_K,), lambda i: (i,), memory_space=pltpu.SMEM),
                  pl.BlockSpec((DP_TM, D_MODEL), lambda i: (i, 0)),
                  pl.BlockSpec(memory_space=pl.ANY)],
        out_specs=pl.BlockSpec(memory_space=pl.ANY),
        out_shape=jax.ShapeDtypeStruct((CAP, HALF), i32),
        scratch_shapes=[pltpu.VMEM((DP_TM, HALF), i32), pltpu.SemaphoreType.DMA((1,))],
        input_output_aliases={2: 0},
        compiler_params=pltpu.CompilerParams(dimension_semantics=("arbitrary",)),
        name="dispatch",
    )(dest_flat, h, xs0)


GU_TN = 512
DN_TN = 1024


def _block_ids(be_ref, nu_ref):
    i = pl.program_id(1)
    last = jnp.maximum(nu_ref[0], 1) - 1
    ic = jnp.minimum(i, last)
    new_expert = (i == 0) | (be_ref[ic] != be_ref[jnp.maximum(ic - 1, 0)])
    return i, last, new_expert


def _expert_gu_kernel(be_ref, nu_ref, x_ref, wg_ref, wl_ref, bg_ref, bl_ref, o_ref, wg_sc, wl_sc):
    i, last, new_expert = _block_ids(be_ref, nu_ref)

    @pl.when(i <= last)
    def _():
        @pl.when(new_expert)
        def _():
            wg_sc[...] = wg_ref[...].astype(bf16)
            wl_sc[...] = wl_ref[...].astype(bf16)

        x = _unpack_bf16_pairs(x_ref[...])
        glu = jnp.dot(x, wg_sc[...], preferred_element_type=f32) + bg_ref[...]
        lin = jnp.dot(x, wl_sc[...], preferred_element_type=f32) + bl_ref[...]
        glu = jnp.minimum(glu, SWIGLU_LIMIT)
        lin = jnp.clip(lin, -SWIGLU_LIMIT, SWIGLU_LIMIT)
        o_ref[...] = (glu * jax.nn.sigmoid(SWIGLU_ALPHA * glu) * (lin + 1.0)).astype(o_ref.dtype)

    @pl.when(i > last)
    def _():
        o_ref[...] = jnp.zeros_like(o_ref)


def _expert_gu(block_e, n_used, xs, w_gu, b_gu):
    nj = D_MODEL // GU_TN

    def blk(j, i, be, nu):
        return jnp.minimum(i, jnp.maximum(nu[0], 1) - 1)

    grid_spec = pltpu.PrefetchScalarGridSpec(
        num_scalar_prefetch=2,
        grid=(nj, NB),
        in_specs=[
            pl.BlockSpec((BM, HALF), lambda j, i, be, nu: (blk(j, i, be, nu), 0)),
            pl.BlockSpec((None, D_MODEL, GU_TN), lambda j, i, be, nu: (be[blk(j, i, be, nu)], 0, j)),
            pl.BlockSpec((None, D_MODEL, GU_TN), lambda j, i, be, nu: (be[blk(j, i, be, nu)], 0, nj + j)),
            pl.BlockSpec((None, 1, GU_TN), lambda j, i, be, nu: (be[blk(j, i, be, nu)], 0, j)),
            pl.BlockSpec((None, 1, GU_TN), lambda j, i, be, nu: (be[blk(j, i, be, nu)], 0, nj + j)),
        ],
        out_specs=pl.BlockSpec((BM, GU_TN), lambda j, i, be, nu: (i, j)),
        scratch_shapes=[pltpu.VMEM((D_MODEL, GU_TN), bf16), pltpu.VMEM((D_MODEL, GU_TN), bf16)],
    )
    return pl.pallas_call(
        _expert_gu_kernel,
        grid_spec=grid_spec,
        out_shape=jax.ShapeDtypeStruct((CAP, D_MODEL), bf16),
        compiler_params=pltpu.CompilerParams(
            dimension_semantics=("arbitrary", "arbitrary"), vmem_limit_bytes=VMEM_LIMIT),
        name="expert_gu",
    )(block_e, n_used, xs, w_gu, w_gu, b_gu, b_gu)


def _expert_down_kernel(be_ref, nu_ref, a_ref, w_ref, b_ref, o_ref, w_sc):
    i, last, new_expert = _block_ids(be_ref, nu_ref)

    @pl.when(i <= last)
    def _():
        @pl.when(new_expert)
        def _():
            w_sc[...] = w_ref[...].astype(bf16)

        o_ref[...] = jnp.dot(a_ref[...], w_sc[...], preferred_element_type=f32) + b_ref[...]

    @pl.when(i > last)
    def _():
        o_ref[...] = jnp.zeros_like(o_ref)


def _expert_down(block_e, n_used, act, w_down, b_down):
    def blk(j, i, be, nu):
        return jnp.minimum(i, jnp.maximum(nu[0], 1) - 1)

    grid_spec = pltpu.PrefetchScalarGridSpec(
        num_scalar_prefetch=2,
        grid=(D_MODEL // DN_TN, NB),
        in_specs=[
            pl.BlockSpec((BM, D_MODEL), lambda j, i, be, nu: (blk(j, i, be, nu), 0)),
            pl.BlockSpec((None, D_MODEL, DN_TN), lambda j, i, be, nu: (be[blk(j, i, be, nu)], 0, j)),
            pl.BlockSpec((None, 1, DN_TN), lambda j, i, be, nu: (be[blk(j, i, be, nu)], 0, j)),
        ],
        out_specs=pl.BlockSpec((BM, DN_TN), lambda j, i, be, nu: (i, j)),
        scratch_shapes=[pltpu.VMEM((D_MODEL, DN_TN), bf16)],
    )
    return pl.pallas_call(
        _expert_down_kernel,
        grid_spec=grid_spec,
        out_shape=jax.ShapeDtypeStruct((CAP, D_MODEL), f32),
        compiler_params=pltpu.CompilerParams(
            dimension_semantics=("arbitrary", "arbitrary"), vmem_limit_bytes=VMEM_LIMIT),
        name="expert_down",
    )(block_e, n_used, act, w_down, b_down)


CB_TM = 128


def _combine_kernel(dcur_ref, dnext_ref, y_hbm, gate_ref, h_ref, g_ref, b_ref, o_ref, buf, sem):
    i = pl.program_id(0)
    slot = i % 2

    def gather(d_ref, s):
        def issue(t, carry):
            for k in range(TOP_K):
                pltpu.make_async_copy(y_hbm.at[pl.ds(d_ref[t * TOP_K + k], 1)],
                                      buf.at[s, k, pl.ds(t, 1)], sem.at[s]).start()
            return carry
        lax.fori_loop(0, CB_TM, issue, 0)

    @pl.when(i == 0)
    def _():
        gather(dcur_ref, 0)

    @pl.when(i + 1 < pl.num_programs(0))
    def _():
        gather(dnext_ref, 1 - slot)

    for k in range(TOP_K):
        pltpu.make_async_copy(y_hbm.at[pl.ds(0, CB_TM)], buf.at[slot, k], sem.at[slot]).wait()
    gate = gate_ref[...]
    ffn = gate[:, 0:1] * buf[slot, 0]
    for k in range(1, TOP_K):
        ffn = ffn + gate[:, k:k + 1] * buf[slot, k]
    o_ref[...] = _layer_norm(ALPHA * h_ref[...] + ffn, g_ref[...], b_ref[...])


def _combine(y, dest_flat, gate, h, ln_g, ln_b):
    n_tiles = T // CB_TM
    return pl.pallas_call(
        _combine_kernel,
        grid=(n_tiles,),
        in_specs=[pl.BlockSpec((CB_TM * TOP_K,), lambda i: (i,), memory_space=pltpu.SMEM),
                  pl.BlockSpec((CB_TM * TOP_K,), lambda i: (jnp.minimum(i + 1, n_tiles - 1),),
                               memory_space=pltpu.SMEM),
                  pl.BlockSpec(memory_space=pl.ANY),
                  pl.BlockSpec((CB_TM, TOP_K), lambda i: (i, 0)),
                  pl.BlockSpec((CB_TM, D_MODEL), lambda i: (i, 0)),
                  pl.BlockSpec((1, D_MODEL), lambda i: (0, 0)),
                  pl.BlockSpec((1, D_MODEL), lambda i: (0, 0))],
        out_specs=pl.BlockSpec((CB_TM, D_MODEL), lambda i: (i, 0)),
        out_shape=jax.ShapeDtypeStruct((T, D_MODEL), f32),
        scratch_shapes=[pltpu.VMEM((2, TOP_K, CB_TM, D_MODEL), f32),
                        pltpu.SemaphoreType.DMA((2,))],
        compiler_params=pltpu.CompilerParams(dimension_semantics=("arbitrary",),
                                             vmem_limit_bytes=VMEM_LIMIT),
        name="combine",
    )(dest_flat, dest_flat, y, gate, h, ln_g, ln_b)


def _expert_layout(counts):
    nblk = (counts + BM - 1) // BM
    bend = jnp.cumsum(nblk)
    pstart = (bend - nblk) * BM
    owner = jnp.sum((bend[None, :] <= jnp.arange(NB, dtype=i32)[:, None]).astype(i32), axis=1)
    block_e = jnp.minimum(owner, N_EXPERTS - 1)
    return pstart, block_e, bend[-1:].astype(i32)


def kernel(x, w_in, rel_bias, sgu_ln_g, sgu_ln_b, sgu_w, sgu_b, w_branch, w_out, ln1_g, ln1_b,
           w_router, b_router, w_gu, b_gu, w_down, b_down, ln2_g, ln2_b):
    x2 = x.reshape(T, D_MODEL)
    proj = _inproj(x2, w_in[0].astype(bf16))
    y_attn = _attention(proj.reshape(BATCH, SEQ, IN_COLS), _attention_bias(rel_bias))
    y_sgu = _sgu(proj, sgu_ln_g, sgu_ln_b, sgu_w[0], sgu_b[0].T)
    y = _branch(y_attn.reshape(T, BRANCH_WIDTH), y_sgu, proj,
                w_branch[0, 0].astype(bf16), w_branch[0, 1].astype(bf16))
    wr_hi = w_router[0].astype(bf16)
    wr_lo = (w_router[0] - wr_hi.astype(f32)).astype(bf16)
    h, idx, gate, counts = _out_ln_router(y, x2, w_out[0].astype(bf16), ln1_g, ln1_b,
                                          jnp.concatenate([wr_hi, wr_lo], axis=1), b_router)
    pstart, block_e, n_used = _expert_layout(counts[0].astype(i32))
    dest = _rank(idx, pstart.astype(f32)[None, :]).reshape(T * TOP_K)
    xs = _dispatch(h, dest)
    act = _expert_gu(block_e, n_used, xs, w_gu.reshape(N_EXPERTS, D_MODEL, 2 * D_MODEL),
                     b_gu.reshape(N_EXPERTS, 1, 2 * D_MODEL))
    ye = _expert_down(block_e, n_used, act, w_down.reshape(N_EXPERTS, D_MODEL, D_MODEL),
                      b_down.reshape(N_EXPERTS, 1, D_MODEL))
    out = _combine(ye, dest, gate, h, ln2_g, ln2_b)
    return out.reshape(BATCH, SEQ, D_MODEL)
```

```python
import functools
import math

import numpy as np
import jax
import jax.numpy as jnp
from jax import lax
from jax.experimental import pallas as pl
from jax.experimental.pallas import tpu as pltpu

D_MODEL = 2048
BATCH = 8
SEQ = 2048
T = BATCH * SEQ
BRANCH_WIDTH = D_MODEL // 2
HEAD_DIM = 128
ATTN_GROUPS = 3
WINDOWS = (128, 512, 2048)
DILATIONS = (1, 4, 16)
ATTN_HEADS = BRANCH_WIDTH // HEAD_DIM
QKV_WIDTH = ATTN_GROUPS * ATTN_HEADS * HEAD_DIM
ATTN_BLOCK = 128
NUM_BUCKETS = 32
MAX_DISTANCE = 2048
SGU_CHUNK = 128
SGU_GROUPS = BRANCH_WIDTH // SGU_CHUNK
IN_COLS = 3 * QKV_WIDTH + 2 * BRANCH_WIDTH + 2 * D_MODEL
N_EXPERTS = 32
TOP_K = 4
SWIGLU_ALPHA = 1.702
SWIGLU_LIMIT = 7.0
LN_EPS = 1e-5
ALPHA = 2.0 ** 0.25
LOG2E = math.log2(math.e)
ATTN_CHUNK = 4

COL_U = 3 * QKV_WIDTH // 1024
COL_V = COL_U + 1
COL_GA = COL_U + 2
COL_GS = COL_GA + 2

BM = 512
NB = (T * TOP_K + N_EXPERTS * (BM - 1) + BM - 1) // BM
CAP = NB * BM

VMEM_LIMIT = 56 * 1024 * 1024

f32 = jnp.float32
bf16 = jnp.bfloat16
i32 = jnp.int32


def _layer_norm(x, g, b):
    mu = jnp.mean(x, axis=-1, keepdims=True)
    xc = x - mu
    var = jnp.mean(xc * xc, axis=-1, keepdims=True)
    return xc * lax.rsqrt(var + LN_EPS) * g + b


IP_TM = 1024
IP_TN = 1024


IP_CHUNK = 256


def _inproj_kernel(x_ref, w_ref, o_ref, xb_sc):
    j = pl.program_id(1)

    @pl.when(j == 0)
    def _():
        xb_sc[...] = x_ref[...].astype(bf16)

    def project(epilogue):
        for c in range(0, IP_TN, IP_CHUNK):
            acc = jnp.dot(xb_sc[...], w_ref[:, c:c + IP_CHUNK], preferred_element_type=f32)
            o_ref[:, c:c + IP_CHUNK] = epilogue(acc)

    @pl.when(j < COL_U)
    def _():
        project(lambda a: a)

    @pl.when((j >= COL_U) & (j < COL_GA))
    def _():
        project(jax.nn.gelu)

    @pl.when(j >= COL_GA)
    def _():
        project(jax.nn.sigmoid)


def _inproj(x2, wb):
    return pl.pallas_call(
        _inproj_kernel,
        grid=(T // IP_TM, IN_COLS // IP_TN),
        in_specs=[pl.BlockSpec((IP_TM, D_MODEL), lambda i, j: (i, 0)),
                  pl.BlockSpec((D_MODEL, IP_TN), lambda i, j: (0, j))],
        out_specs=pl.BlockSpec((IP_TM, IP_TN), lambda i, j: (i, j)),
        out_shape=jax.ShapeDtypeStruct((T, IN_COLS), f32),
        scratch_shapes=[pltpu.VMEM((IP_TM, D_MODEL), bf16)],
        compiler_params=pltpu.CompilerParams(
            dimension_semantics=("arbitrary", "arbitrary"), vmem_limit_bytes=VMEM_LIMIT),
        name="inproj",
    )(x2, wb)


def _attn_kernel(q0, q1, q2, k0, k1, k2, v0, v1, v2, bias_ref, o_ref,
                 m0, m1, m2, l0, l1, l2, a0, a1, a2, out_sc):
    qs, ks, vs = (q0, q1, q2), (k0, k1, k2), (v0, v1, v2)
    ms, ls, accs = (m0, m1, m2), (l0, l1, l2), (a0, a1, a2)
    scale = HEAD_DIM ** -0.5 * LOG2E
    QB = ATTN_BLOCK

    def rows(start, n, r):
        return pl.ds(start, n, stride=r) if r > 1 else pl.ds(start, n)

    def tiles(g, specs):
        r = DILATIONS[g]
        nk = specs[0][2]
        bias = bias_ref[g, :, 2 * QB - nk:]
        s, v = [], []
        for q_start, k_start, _ in specs:
            q = qs[g][rows(q_start, QB, r), :].astype(bf16)
            k = ks[g][rows(k_start, nk, r), :].astype(bf16)
            v.append(vs[g][rows(k_start, nk, r), :].astype(bf16))
            s.append(lax.dot_general(q, k, (((1,), (1,)), ((), ())), preferred_element_type=f32))
        s = jnp.stack(s) * scale + bias[None]
        m = jnp.max(s, axis=-1, keepdims=True)
        p = jnp.exp2(s - m)
        l = jnp.sum(p, axis=-1, keepdims=True)
        p = p.astype(bf16)
        out = []
        for t in range(len(specs)):
            pv = jnp.dot(p[t], v[t], preferred_element_type=f32)
            out.append((jnp.broadcast_to(m[t], pv.shape), jnp.broadcast_to(l[t], pv.shape), pv))
        return out

    for g in range(ATTN_GROUPS):
        r = DILATIONS[g]
        sub_len = SEQ // r
        first, rest = [], []
        for c in range(r):
            for n in range(sub_len // QB):
                q_start = c + n * QB * r
                dst = pl.ds(c * sub_len + n * QB, QB)
                if n == 0:
                    first.append(((q_start, q_start, QB), dst))
                else:
                    rest.append(((q_start, q_start - QB * r, 2 * QB), dst))
        for todo in (first, rest):
            for at in range(0, len(todo), ATTN_CHUNK):
                chunk = todo[at:at + ATTN_CHUNK]
                for (_, dst), (m, l, pv) in zip(chunk, tiles(g, [spec for spec, _ in chunk])):
                    ms[g][dst, :] = m
                    ls[g][dst, :] = l
                    accs[g][dst, :] = pv

    r_max = DILATIONS[-1]
    for c in range(r_max):
        sel = []
        for g in range(ATTN_GROUPS):
            r = DILATIONS[g]
            sel.append(rows((c % r) * (SEQ // r) + c // r, SEQ // r_max, r_max // r))
        m_g = [ms[g][sel[g], :] for g in range(ATTN_GROUPS)]
        m_all = jnp.maximum(jnp.maximum(m_g[0], m_g[1]), m_g[2])
        w_g = [jnp.exp2(m - m_all) for m in m_g]
        den = sum(w * ls[g][sel[g], :] for g, w in enumerate(w_g))
        num = sum(w * accs[g][sel[g], :] for g, w in enumerate(w_g))
        out_sc[rows(c, SEQ // r_max, r_max), :] = num / den
    o_ref[...] = out_sc[...].astype(o_ref.dtype)


def _attention(proj3, bias):
    hb = HEAD_DIM
    def col(kind, g):
        off = (kind * ATTN_GROUPS + g) * ATTN_HEADS
        return pl.BlockSpec((None, SEQ, hb), lambda b, h, off=off: (b, 0, off + h))
    in_specs = [col(kind, g) for kind in range(3) for g in range(ATTN_GROUPS)]
    in_specs.append(pl.BlockSpec((ATTN_GROUPS, None, ATTN_BLOCK, 2 * ATTN_BLOCK),
                                 lambda b, h: (0, h, 0, 0)))
    return pl.pallas_call(
        _attn_kernel,
        grid=(BATCH, ATTN_HEADS),
        in_specs=in_specs,
        out_specs=pl.BlockSpec((None, SEQ, hb), lambda b, h: (b, 0, h)),
        out_shape=jax.ShapeDtypeStruct((BATCH, SEQ, BRANCH_WIDTH), bf16),
        scratch_shapes=[pltpu.VMEM((SEQ, hb), f32)] * (3 * ATTN_GROUPS + 1),
        compiler_params=pltpu.CompilerParams(
            dimension_semantics=("arbitrary", "arbitrary"), vmem_limit_bytes=VMEM_LIMIT),
        name="attention",
    )(*([proj3] * 9), bias)


def _t5_bucket(dist):
    exact = NUM_BUCKETS // 2
    d = jnp.maximum(dist, 1).astype(f32)
    large = exact + (jnp.log(d / exact) / math.log(MAX_DISTANCE / exact)
                     * (NUM_BUCKETS - exact)).astype(i32)
    large = jnp.minimum(large, NUM_BUCKETS - 1)
    return jnp.where(dist < exact, dist, large)


def _attention_bias(rel_bias):
    qi = np.arange(ATTN_BLOCK, dtype=np.int32)[:, None]
    kj = np.arange(2 * ATTN_BLOCK, dtype=np.int32)[None, :]
    sub = qi + ATTN_BLOCK - kj
    table = rel_bias.reshape(NUM_BUCKETS, ATTN_GROUPS, ATTN_HEADS)
    out = []
    for g in range(ATTN_GROUPS):
        span = WINDOWS[g] // DILATIONS[g]
        band = (sub >= 0) & (sub <= span)
        bucket = _t5_bucket(jnp.asarray(DILATIONS[g] * np.clip(sub, 0, span), i32))
        onehot = jax.nn.one_hot(bucket.reshape(-1), NUM_BUCKETS, dtype=f32)
        b = jnp.dot(onehot, table[:, g], precision=lax.Precision.HIGHEST)
        b = jnp.where(band[:, :, None], b.reshape(ATTN_BLOCK, 2 * ATTN_BLOCK, ATTN_HEADS), -1e30)
        out.append(b.transpose(2, 0, 1))
    return jnp.stack(out, axis=0).astype(f32) * LOG2E


SGU_TM = 512


def _sgu_kernel(u_ref, v_ref, g_ref, b_ref, w_ref, bst_ref, o_ref):
    vn = _layer_norm(v_ref[...], g_ref[...], b_ref[...]).astype(bf16)
    ri = lax.broadcasted_iota(i32, (SGU_CHUNK, SGU_CHUNK), 0)
    ci = lax.broadcasted_iota(i32, (SGU_CHUNK, SGU_CHUNK), 1)
    causal = ci <= ri
    for g in range(SGU_GROUPS):
        wg = jnp.where(causal, w_ref[g], 0.0).astype(bf16)
        bias = bst_ref[:, g:g + 1]
        cs = slice(g * SGU_CHUNK, (g + 1) * SGU_CHUNK)
        for c in range(SGU_TM // SGU_CHUNK):
            rs = slice(c * SGU_CHUNK, (c + 1) * SGU_CHUNK)
            mixed = jnp.dot(wg, vn[rs, cs], preferred_element_type=f32) + bias
            o_ref[rs, cs] = (u_ref[rs, cs] * mixed).astype(o_ref.dtype)


def _sgu(proj, ln_g, ln_b, w_s, b_st):
    return pl.pallas_call(
        _sgu_kernel,
        grid=(T // SGU_TM,),
        in_specs=[pl.BlockSpec((SGU_TM, BRANCH_WIDTH), lambda i: (i, COL_U)),
                  pl.BlockSpec((SGU_TM, BRANCH_WIDTH), lambda i: (i, COL_V)),
                  pl.BlockSpec((1, BRANCH_WIDTH), lambda i: (0, 0)),
                  pl.BlockSpec((1, BRANCH_WIDTH), lambda i: (0, 0)),
                  pl.BlockSpec((SGU_GROUPS, SGU_CHUNK, SGU_CHUNK), lambda i: (0, 0, 0)),
                  pl.BlockSpec((SGU_CHUNK, SGU_GROUPS), lambda i: (0, 0))],
        out_specs=pl.BlockSpec((SGU_TM, BRANCH_WIDTH), lambda i: (i, 0)),
        out_shape=jax.ShapeDtypeStruct((T, BRANCH_WIDTH), bf16),
        compiler_params=pltpu.CompilerParams(
            dimension_semantics=("arbitrary",), vmem_limit_bytes=VMEM_LIMIT),
        name="sgu",
    )(proj, proj, ln_g, ln_b, w_s, b_st)


BR_TM = 1024
BR_TN = 1024


def _branch_kernel(ya_ref, ys_ref, ga_ref, gs_ref, wa_ref, ws_ref, o_ref):
    a = jnp.dot(ya_ref[...], wa_ref[...], preferred_element_type=f32)
    s = jnp.dot(ys_ref[...], ws_ref[...], preferred_element_type=f32)
    o_ref[...] = (ga_ref[...] * a + gs_ref[...] * s).astype(o_ref.dtype)


def _branch(ya, ys, proj, wa, ws):
    return pl.pallas_call(
        _branch_kernel,
        grid=(T // BR_TM, D_MODEL // BR_TN),
        in_specs=[pl.BlockSpec((BR_TM, BRANCH_WIDTH), lambda i, j: (i, 0)),
                  pl.BlockSpec((BR_TM, BRANCH_WIDTH), lambda i, j: (i, 0)),
                  pl.BlockSpec((BR_TM, BR_TN), lambda i, j: (i, COL_GA + j)),
                  pl.BlockSpec((BR_TM, BR_TN), lambda i, j: (i, COL_GS + j)),
                  pl.BlockSpec((BRANCH_WIDTH, BR_TN), lambda i, j: (0, j)),
                  pl.BlockSpec((BRANCH_WIDTH, BR_TN), lambda i, j: (0, j))],
        out_specs=pl.BlockSpec((BR_TM, BR_TN), lambda i, j: (i, j)),
        out_shape=jax.ShapeDtypeStruct((T, D_MODEL), bf16),
        compiler_params=pltpu.CompilerParams(
            dimension_semantics=("arbitrary", "arbitrary"), vmem_limit_bytes=VMEM_LIMIT),
        name="branch",
    )(ya, ys, proj, proj, wa, ws)


OL_TM = 512


def _out_ln_router_kernel(y_ref, x_ref, wo_ref, g_ref, b_ref, wr_ref, br_ref,
                          h_ref, idx_ref, gate_ref, cnt_ref):
    i = pl.program_id(0)
    mixed = jnp.dot(y_ref[...], wo_ref[...], preferred_element_type=f32)
    h = _layer_norm(ALPHA * x_ref[...] + mixed, g_ref[...], b_ref[...])
    h_ref[...] = h
    h_hi = h.astype(bf16)
    h_lo = (h - h_hi.astype(f32)).astype(bf16)
    t = jnp.dot(h_hi, wr_ref[...], preferred_element_type=f32)
    u = jnp.dot(h_lo, wr_ref[:, :N_EXPERTS], preferred_element_type=f32)
    logits = t[:, :N_EXPERTS] + (t[:, N_EXPERTS:] + u) + br_ref[...]
    lane = lax.broadcasted_iota(i32, (OL_TM, N_EXPERTS), 1).astype(f32)
    lane_k = lax.broadcasted_iota(i32, (OL_TM, TOP_K), 1)
    idx_out = jnp.zeros((OL_TM, TOP_K), f32)
    val_out = jnp.zeros((OL_TM, TOP_K), f32)
    picked = jnp.zeros((OL_TM, N_EXPERTS), f32)
    work = logits
    for k in range(TOP_K):
        m = jnp.max(work, axis=-1, keepdims=True)
        idx = jnp.min(jnp.where(work == m, lane, float(N_EXPERTS)), axis=-1, keepdims=True)
        sel = lane == idx
        idx_out = jnp.where(lane_k == k, idx, idx_out)
        val_out = jnp.where(lane_k == k, m, val_out)
        picked = picked + sel.astype(f32)
        work = jnp.where(sel, -jnp.inf, work)
    e = jnp.exp(val_out - jnp.max(val_out, axis=-1, keepdims=True))
    gate_ref[...] = e / jnp.sum(e, axis=-1, keepdims=True)
    idx_ref[...] = idx_out.astype(i32)

    @pl.when(i == 0)
    def _():
        cnt_ref[...] = jnp.zeros_like(cnt_ref)

    cnt_ref[...] += jnp.sum(picked, axis=0, keepdims=True)


def _out_ln_router(y, x2, wo, ln_g, ln_b, w_router, b_router):
    return pl.pallas_call(
        _out_ln_router_kernel,
        grid=(T // OL_TM,),
        in_specs=[pl.BlockSpec((OL_TM, D_MODEL), lambda i: (i, 0)),
                  pl.BlockSpec((OL_TM, D_MODEL), lambda i: (i, 0)),
                  pl.BlockSpec((D_MODEL, D_MODEL), lambda i: (0, 0)),
                  pl.BlockSpec((1, D_MODEL), lambda i: (0, 0)),
                  pl.BlockSpec((1, D_MODEL), lambda i: (0, 0)),
                  pl.BlockSpec((D_MODEL, 2 * N_EXPERTS), lambda i: (0, 0)),
                  pl.BlockSpec((1, N_EXPERTS), lambda i: (0, 0))],
        out_specs=[pl.BlockSpec((OL_TM, D_MODEL), lambda i: (i, 0)),
                   pl.BlockSpec((OL_TM, TOP_K), lambda i: (i, 0)),
                   pl.BlockSpec((OL_TM, TOP_K), lambda i: (i, 0)),
                   pl.BlockSpec((1, N_EXPERTS), lambda i: (0, 0))],
        out_shape=[jax.ShapeDtypeStruct((T, D_MODEL), f32),
                   jax.ShapeDtypeStruct((T, TOP_K), i32),
                   jax.ShapeDtypeStruct((T, TOP_K), f32),
                   jax.ShapeDtypeStruct((1, N_EXPERTS), f32)],
        compiler_params=pltpu.CompilerParams(
            dimension_semantics=("arbitrary",), vmem_limit_bytes=VMEM_LIMIT),
        name="out_ln_router",
    )(y, x2, wo, ln_g, ln_b, w_router, b_router)


RK_TM = 512


def _rank_kernel(idx_ref, pstart_ref, dest_ref, carry_sc):
    i = pl.program_id(0)

    @pl.when(i == 0)
    def _():
        carry_sc[...] = jnp.zeros_like(carry_sc)

    idx = idx_ref[...]
    lane = lax.broadcasted_iota(i32, (RK_TM, N_EXPERTS), 1)
    picked = jnp.zeros((RK_TM, N_EXPERTS), f32)
    for k in range(TOP_K):
        picked = picked + (lane == idx[:, k:k + 1]).astype(f32)
    ri = lax.broadcasted_iota(i32, (RK_TM, RK_TM), 0)
    ci = lax.broadcasted_iota(i32, (RK_TM, RK_TM), 1)
    lower = (ci < ri).astype(bf16)
    before = jnp.dot(lower, picked.astype(bf16), preferred_element_type=f32)
    slot = before + carry_sc[...] + pstart_ref[...]
    lane_k = lax.broadcasted_iota(i32, (RK_TM, TOP_K), 1)
    dest = jnp.zeros((RK_TM, TOP_K), f32)
    for k in range(TOP_K):
        d = jnp.sum(jnp.where(lane == idx[:, k:k + 1], slot, 0.0), axis=-1, keepdims=True)
        dest = jnp.where(lane_k == k, d, dest)
    dest_ref[...] = dest.astype(i32)
    carry_sc[...] += jnp.sum(picked, axis=0, keepdims=True)


def _rank(idx, pstart):
    return pl.pallas_call(
        _rank_kernel,
        grid=(T // RK_TM,),
        in_specs=[pl.BlockSpec((RK_TM, TOP_K), lambda i: (i, 0)),
                  pl.BlockSpec((1, N_EXPERTS), lambda i: (0, 0))],
        out_specs=pl.BlockSpec((RK_TM, TOP_K), lambda i: (i, 0)),
        out_shape=jax.ShapeDtypeStruct((T, TOP_K), i32),
        scratch_shapes=[pltpu.VMEM((1, N_EXPERTS), f32)],
        compiler_params=pltpu.CompilerParams(dimension_semantics=("arbitrary",)),
        name="rank",
    )(idx, pstart)


DP_TM = 256
HALF = D_MODEL // 2


def _pack_bf16_pairs(x):
    bits = lax.bitcast_convert_type(x.astype(bf16).astype(f32), i32)
    return bits[:, HALF:] | lax.shift_right_logical(bits[:, :HALF], 16)


def _unpack_bf16_pairs(w):
    lo = lax.bitcast_convert_type(lax.shift_left(w, 16), f32)
    hi = lax.bitcast_convert_type(w & jnp.int32(-65536), f32)
    return jnp.concatenate([lo, hi], axis=1).astype(bf16)


def _dispatch_kernel(dest_ref, h_ref, xs_in, xs_hbm, pack_sc, sem):
    del xs_in
    pack_sc[...] = _pack_bf16_pairs(h_ref[...])

    def issue(t, carry):
        for k in range(TOP_K):
            pltpu.make_async_copy(pack_sc.at[pl.ds(t, 1)],
                                  xs_hbm.at[pl.ds(dest_ref[t * TOP_K + k], 1)], sem.at[0]).start()
        return carry

    lax.fori_loop(0, DP_TM, issue, 0)
    for _ in range(TOP_K):
        pltpu.make_async_copy(pack_sc, xs_hbm.at[pl.ds(0, DP_TM)], sem.at[0]).wait()


def _dispatch(h, dest_flat):
    xs0 = jnp.zeros((CAP, HALF), i32)
    return pl.pallas_call(
        _dispatch_kernel,
        grid=(T // DP_TM,),
        in_specs=[pl.BlockSpec((DP_TM * TOP_K,), lambda i: (i,), memory_space=pltpu.SMEM),
                  pl.BlockSpec((DP_TM, D_MODEL), lambda i: (i, 0)),
                  pl.BlockSpec(memory_space=pl.ANY)],
        out_specs=pl.BlockSpec(memory_space=pl.ANY),
        out_shape=jax.ShapeDtypeStruct((CAP, HALF), i32),
        scratch_shapes=[pltpu.VMEM((DP_TM, HALF), i32), pltpu.SemaphoreType.DMA((1,))],
        input_output_aliases={2: 0},
        compiler_params=pltpu.CompilerParams(dimension_semantics=("arbitrary",)),
        name="dispatch",
    )(dest_flat, h, xs0)


GU_TN = 512
DN_TN = 1024


def _stream_expert_weights(bs_ref, se_ref, meta_ref, w_hbm, col_offsets, tn, wbuf, sem, caches):
    j, i, nj = pl.program_id(0), pl.program_id(1), pl.num_programs(0)
    n_seg = jnp.maximum(meta_ref[1], 1)
    seg = bs_ref[i]
    first_block = (i == 0) | (seg != bs_ref[jnp.maximum(i - 1, 0)])

    def copies(s, jj, slot):
        expert = se_ref[s]
        out = []
        for c, off in enumerate(col_offsets):
            start = off + jj * tn
            if not isinstance(start, int):
                start = pl.multiple_of(start, tn)
            out.append(pltpu.make_async_copy(w_hbm.at[expert, :, pl.ds(start, tn)],
                                             wbuf.at[slot, c], sem.at[slot]))
        return out

    @pl.when(first_block)
    def _():
        seq = j * n_seg + seg
        slot = seq % 2

        @pl.when(seq == 0)
        def _():
            for cp in copies(0, 0, 0):
                cp.start()

        more = seg + 1 < n_seg

        @pl.when(more | (j + 1 < nj))
        def _():
            for cp in copies(jnp.where(more, seg + 1, 0), jnp.where(more, j, j + 1), 1 - slot):
                cp.start()

        for cp in copies(seg, j, slot):
            cp.wait()
        for c, cache in enumerate(caches):
            cache[...] = wbuf[slot, c].astype(bf16)


def _expert_gu_kernel(be_ref, bs_ref, se_ref, meta_ref, x_ref, w_hbm, bg_ref, bl_ref, o_ref,
                      wbuf, wg_sc, wl_sc, sem):
    del be_ref
    i = pl.program_id(1)
    last = jnp.maximum(meta_ref[0], 1) - 1

    @pl.when(i <= last)
    def _():
        _stream_expert_weights(bs_ref, se_ref, meta_ref, w_hbm, (0, D_MODEL), GU_TN,
                               wbuf, sem, (wg_sc, wl_sc))
        x = _unpack_bf16_pairs(x_ref[...])
        glu = jnp.dot(x, wg_sc[...], preferred_element_type=f32) + bg_ref[...]
        lin = jnp.dot(x, wl_sc[...], preferred_element_type=f32) + bl_ref[...]
        glu = jnp.minimum(glu, SWIGLU_LIMIT)
        lin = jnp.clip(lin, -SWIGLU_LIMIT, SWIGLU_LIMIT)
        o_ref[...] = (glu * jax.nn.sigmoid(SWIGLU_ALPHA * glu) * (lin + 1.0)).astype(o_ref.dtype)

    @pl.when(i > last)
    def _():
        o_ref[...] = jnp.zeros_like(o_ref)


def _used_block(j, i, be, bs, se, meta):
    return jnp.minimum(i, jnp.maximum(meta[0], 1) - 1)


def _expert_gu(layout, xs, w_gu, b_gu):
    nj = D_MODEL // GU_TN
    blk = _used_block
    grid_spec = pltpu.PrefetchScalarGridSpec(
        num_scalar_prefetch=4,
        grid=(nj, NB),
        in_specs=[
            pl.BlockSpec((BM, HALF), lambda *a: (blk(*a), 0)),
            pl.BlockSpec(memory_space=pl.ANY),
            pl.BlockSpec((None, 1, GU_TN), lambda *a: (a[2][blk(*a)], 0, a[0])),
            pl.BlockSpec((None, 1, GU_TN), lambda *a: (a[2][blk(*a)], 0, nj + a[0])),
        ],
        out_specs=pl.BlockSpec((BM, GU_TN), lambda j, i, *_: (i, j)),
        scratch_shapes=[pltpu.VMEM((2, 2, D_MODEL, GU_TN), f32),
                        pltpu.VMEM((D_MODEL, GU_TN), bf16), pltpu.VMEM((D_MODEL, GU_TN), bf16),
                        pltpu.SemaphoreType.DMA((2,))],
    )
    return pl.pallas_call(
        _expert_gu_kernel,
        grid_spec=grid_spec,
        out_shape=jax.ShapeDtypeStruct((CAP, D_MODEL), bf16),
        compiler_params=pltpu.CompilerParams(
            dimension_semantics=("arbitrary", "arbitrary"), vmem_limit_bytes=VMEM_LIMIT),
        name="expert_gu",
    )(*layout, xs, w_gu, b_gu, b_gu)


def _expert_down_kernel(be_ref, bs_ref, se_ref, meta_ref, a_ref, w_hbm, b_ref, o_ref,
                        wbuf, w_sc, sem):
    del be_ref
    i = pl.program_id(1)
    last = jnp.maximum(meta_ref[0], 1) - 1

    @pl.when(i <= last)
    def _():
        _stream_expert_weights(bs_ref, se_ref, meta_ref, w_hbm, (0,), DN_TN, wbuf, sem, (w_sc,))
        o_ref[...] = jnp.dot(a_ref[...], w_sc[...], preferred_element_type=f32) + b_ref[...]

    @pl.when(i > last)
    def _():
        o_ref[...] = jnp.zeros_like(o_ref)


def _expert_down(layout, act, w_down, b_down):
    blk = _used_block
    grid_spec = pltpu.PrefetchScalarGridSpec(
        num_scalar_prefetch=4,
        grid=(D_MODEL // DN_TN, NB),
        in_specs=[
            pl.BlockSpec((BM, D_MODEL), lambda *a: (blk(*a), 0)),
            pl.BlockSpec(memory_space=pl.ANY),
            pl.BlockSpec((None, 1, DN_TN), lambda *a: (a[2][blk(*a)], 0, a[0])),
        ],
        out_specs=pl.BlockSpec((BM, DN_TN), lambda j, i, *_: (i, j)),
        scratch_shapes=[pltpu.VMEM((2, 1, D_MODEL, DN_TN), f32),
                        pltpu.VMEM((D_MODEL, DN_TN), bf16),
                        pltpu.SemaphoreType.DMA((2,))],
    )
    return pl.pallas_call(
        _expert_down_kernel,
        grid_spec=grid_spec,
        out_shape=jax.ShapeDtypeStruct((CAP, D_MODEL), f32),
        compiler_params=pltpu.CompilerParams(
            dimension_semantics=("arbitrary", "arbitrary"), vmem_limit_bytes=VMEM_LIMIT),
        name="expert_down",
    )(*layout, act, w_down, b_down)


CB_TM = 128


def _combine_kernel(dcur_ref, dnext_ref, y_hbm, gate_ref, h_ref, g_ref, b_ref, o_ref, buf, sem):
    i = pl.program_id(0)
    slot = i % 2

    def gather(d_ref, s):
        def issue(t, carry):
            for k in range(TOP_K):
                pltpu.make_async_copy(y_hbm.at[pl.ds(d_ref[t * TOP_K + k], 1)],
                                      buf.at[s, k, pl.ds(t, 1)], sem.at[s]).start()
            return carry
        lax.fori_loop(0, CB_TM, issue, 0)

    @pl.when(i == 0)
    def _():
        gather(dcur_ref, 0)

    @pl.when(i + 1 < pl.num_programs(0))
    def _():
        gather(dnext_ref, 1 - slot)

    for k in range(TOP_K):
        pltpu.make_async_copy(y_hbm.at[pl.ds(0, CB_TM)], buf.at[slot, k], sem.at[slot]).wait()
    gate = gate_ref[...]
    ffn = gate[:, 0:1] * buf[slot, 0]
    for k in range(1, TOP_K):
        ffn = ffn + gate[:, k:k + 1] * buf[slot, k]
    o_ref[...] = _layer_norm(ALPHA * h_ref[...] + ffn, g_ref[...], b_ref[...])


def _combine(y, dest_flat, gate, h, ln_g, ln_b):
    n_tiles = T // CB_TM
    return pl.pallas_call(
        _combine_kernel,
        grid=(n_tiles,),
        in_specs=[pl.BlockSpec((CB_TM * TOP_K,), lambda i: (i,), memory_space=pltpu.SMEM),
                  pl.BlockSpec((CB_TM * TOP_K,), lambda i: (jnp.minimum(i + 1, n_tiles - 1),),
                               memory_space=pltpu.SMEM),
                  pl.BlockSpec(memory_space=pl.ANY),
                  pl.BlockSpec((CB_TM, TOP_K), lambda i: (i, 0)),
                  pl.BlockSpec((CB_TM, D_MODEL), lambda i: (i, 0)),
                  pl.BlockSpec((1, D_MODEL), lambda i: (0, 0)),
                  pl.BlockSpec((1, D_MODEL), lambda i: (0, 0))],
        out_specs=pl.BlockSpec((CB_TM, D_MODEL), lambda i: (i, 0)),
        out_shape=jax.ShapeDtypeStruct((T, D_MODEL), f32),
        scratch_shapes=[pltpu.VMEM((2, TOP_K, CB_TM, D_MODEL), f32),
                        pltpu.SemaphoreType.DMA((2,))],
        compiler_params=pltpu.CompilerParams(dimension_semantics=("arbitrary",),
                                             vmem_limit_bytes=VMEM_LIMIT),
        name="combine",
    )(dest_flat, dest_flat, y, gate, h, ln_g, ln_b)


def _expert_layout(counts):
    nblk = (counts + BM - 1) // BM
    bend = jnp.cumsum(nblk)
    pstart = (bend - nblk) * BM
    before = bend[None, :] <= jnp.arange(NB, dtype=i32)[:, None]
    block_e = jnp.minimum(jnp.sum(before.astype(i32), axis=1), N_EXPERTS - 1)
    has = nblk > 0
    block_seg = jnp.sum((before & has[None, :]).astype(i32), axis=1)
    seg_of_e = jnp.cumsum(has.astype(i32)) - 1
    ids = jnp.arange(N_EXPERTS, dtype=i32)
    seg_expert = jnp.sum(jnp.where(has[None, :] & (seg_of_e[None, :] == ids[:, None]),
                                   ids[None, :], 0), axis=1)
    meta = jnp.stack([bend[-1], jnp.sum(has.astype(i32))]).astype(i32)
    return pstart, (block_e, block_seg.astype(i32), seg_expert.astype(i32), meta)


def kernel(x, w_in, rel_bias, sgu_ln_g, sgu_ln_b, sgu_w, sgu_b, w_branch, w_out, ln1_g, ln1_b,
           w_router, b_router, w_gu, b_gu, w_down, b_down, ln2_g, ln2_b):
    x2 = x.reshape(T, D_MODEL)
    proj = _inproj(x2, w_in[0].astype(bf16))
    y_attn = _attention(proj.reshape(BATCH, SEQ, IN_COLS), _attention_bias(rel_bias))
    y_sgu = _sgu(proj, sgu_ln_g, sgu_ln_b, sgu_w[0], sgu_b[0].T)
    y = _branch(y_attn.reshape(T, BRANCH_WIDTH), y_sgu, proj,
                w_branch[0, 0].astype(bf16), w_branch[0, 1].astype(bf16))
    wr_hi = w_router[0].astype(bf16)
    wr_lo = (w_router[0] - wr_hi.astype(f32)).astype(bf16)
    h, idx, gate, counts = _out_ln_router(y, x2, w_out[0].astype(bf16), ln1_g, ln1_b,
                                          jnp.concatenate([wr_hi, wr_lo], axis=1), b_router)
    pstart, layout = _expert_layout(counts[0].astype(i32))
    dest = _rank(idx, pstart.astype(f32)[None, :]).reshape(T * TOP_K)
    xs = _dispatch(h, dest)
    act = _expert_gu(layout, xs, w_gu.reshape(N_EXPERTS, D_MODEL, 2 * D_MODEL),
                     b_gu.reshape(N_EXPERTS, 1, 2 * D_MODEL))
    ye = _expert_down(layout, act, w_down.reshape(N_EXPERTS, D_MODEL, D_MODEL),
                      b_down.reshape(N_EXPERTS, 1, D_MODEL))
    out = _combine(ye, dest, gate, h, ln2_g, ln2_b)
    return out.reshape(BATCH, SEQ, D_MODEL)
```

```python
import functools
import math

import numpy as np
import jax
import jax.numpy as jnp
from jax import lax
from jax.experimental import pallas as pl
from jax.experimental.pallas import tpu as pltpu

D_MODEL = 2048
BATCH = 8
SEQ = 2048
T = BATCH * SEQ
BRANCH_WIDTH = D_MODEL // 2
HEAD_DIM = 128
ATTN_GROUPS = 3
WINDOWS = (128, 512, 2048)
DILATIONS = (1, 4, 16)
ATTN_HEADS = BRANCH_WIDTH // HEAD_DIM
QKV_WIDTH = ATTN_GROUPS * ATTN_HEADS * HEAD_DIM
ATTN_BLOCK = 128
NUM_BUCKETS = 32
MAX_DISTANCE = 2048
SGU_CHUNK = 128
SGU_GROUPS = BRANCH_WIDTH // SGU_CHUNK
IN_COLS = 3 * QKV_WIDTH + 2 * BRANCH_WIDTH + 2 * D_MODEL
N_EXPERTS = 32
TOP_K = 4
SWIGLU_ALPHA = 1.702
SWIGLU_LIMIT = 7.0
LN_EPS = 1e-5
ALPHA = 2.0 ** 0.25
LOG2E = math.log2(math.e)
ATTN_CHUNK = 4

COL_U = 3 * QKV_WIDTH // 1024
COL_V = COL_U + 1
COL_GA = COL_U + 2
COL_GS = COL_GA + 2

BM = 512
NB = (T * TOP_K + N_EXPERTS * (BM - 1) + BM - 1) // BM
CAP = NB * BM

VMEM_LIMIT = 56 * 1024 * 1024

f32 = jnp.float32
bf16 = jnp.bfloat16
i32 = jnp.int32


def _layer_norm(x, g, b):
    mu = jnp.mean(x, axis=-1, keepdims=True)
    xc = x - mu
    var = jnp.mean(xc * xc, axis=-1, keepdims=True)
    return xc * lax.rsqrt(var + LN_EPS) * g + b


IP_TM = 1024
IP_TN = 1024


IP_CHUNK = 256


def _inproj_kernel(x_ref, w_ref, o_ref, xb_sc):
    j = pl.program_id(1)

    @pl.when(j == 0)
    def _():
        xb_sc[...] = x_ref[...].astype(bf16)

    def project(epilogue):
        for c in range(0, IP_TN, IP_CHUNK):
            acc = jnp.dot(xb_sc[...], w_ref[:, c:c + IP_CHUNK], preferred_element_type=f32)
            o_ref[:, c:c + IP_CHUNK] = epilogue(acc)

    @pl.when(j < COL_U)
    def _():
        project(lambda a: a)

    @pl.when((j >= COL_U) & (j < COL_GA))
    def _():
        project(jax.nn.gelu)

    @pl.when(j >= COL_GA)
    def _():
        project(jax.nn.sigmoid)


def _inproj(x2, wb):
    return pl.pallas_call(
        _inproj_kernel,
        grid=(T // IP_TM, IN_COLS // IP_TN),
        in_specs=[pl.BlockSpec((IP_TM, D_MODEL), lambda i, j: (i, 0)),
                  pl.BlockSpec((D_MODEL, IP_TN), lambda i, j: (0, j))],
        out_specs=pl.BlockSpec((IP_TM, IP_TN), lambda i, j: (i, j)),
        out_shape=jax.ShapeDtypeStruct((T, IN_COLS), f32),
        scratch_shapes=[pltpu.VMEM((IP_TM, D_MODEL), bf16)],
        compiler_params=pltpu.CompilerParams(
            dimension_semantics=("arbitrary", "arbitrary"), vmem_limit_bytes=VMEM_LIMIT),
        name="inproj",
    )(x2, wb)


def _attn_kernel(q0, q1, q2, k0, k1, k2, v0, v1, v2, bias_ref, o_ref,
                 m0, m1, m2, l0, l1, l2, a0, a1, a2, out_sc):
    qs, ks, vs = (q0, q1, q2), (k0, k1, k2), (v0, v1, v2)
    ms, ls, accs = (m0, m1, m2), (l0, l1, l2), (a0, a1, a2)
    scale = HEAD_DIM ** -0.5 * LOG2E
    QB = ATTN_BLOCK

    def rows(start, n, r):
        return pl.ds(start, n, stride=r) if r > 1 else pl.ds(start, n)

    def tiles(g, specs):
        r = DILATIONS[g]
        nk = specs[0][2]
        bias = bias_ref[g, :, 2 * QB - nk:]
        s, v = [], []
        for q_start, k_start, _ in specs:
            q = qs[g][rows(q_start, QB, r), :].astype(bf16)
            k = ks[g][rows(k_start, nk, r), :].astype(bf16)
            v.append(vs[g][rows(k_start, nk, r), :].astype(bf16))
            s.append(lax.dot_general(q, k, (((1,), (1,)), ((), ())), preferred_element_type=f32))
        s = jnp.stack(s) * scale + bias[None]
        m = jnp.max(s, axis=-1, keepdims=True)
        p = jnp.exp2(s - m)
        l = jnp.sum(p, axis=-1, keepdims=True)
        p = p.astype(bf16)
        out = []
        for t in range(len(specs)):
            pv = jnp.dot(p[t], v[t], preferred_element_type=f32)
            out.append((jnp.broadcast_to(m[t], pv.shape), jnp.broadcast_to(l[t], pv.shape), pv))
        return out

    for g in range(ATTN_GROUPS):
        r = DILATIONS[g]
        sub_len = SEQ // r
        first, rest = [], []
        for c in range(r):
            for n in range(sub_len // QB):
                q_start = c + n * QB * r
                dst = pl.ds(c * sub_len + n * QB, QB)
                if n == 0:
                    first.append(((q_start, q_start, QB), dst))
                else:
                    rest.append(((q_start, q_start - QB * r, 2 * QB), dst))
        for todo in (first, rest):
            for at in range(0, len(todo), ATTN_CHUNK):
                chunk = todo[at:at + ATTN_CHUNK]
                for (_, dst), (m, l, pv) in zip(chunk, tiles(g, [spec for spec, _ in chunk])):
                    ms[g][dst, :] = m
                    ls[g][dst, :] = l
                    accs[g][dst, :] = pv

    r_max = DILATIONS[-1]
    for c in range(r_max):
        sel = []
        for g in range(ATTN_GROUPS):
            r = DILATIONS[g]
            sel.append(rows((c % r) * (SEQ // r) + c // r, SEQ // r_max, r_max // r))
        m_g = [ms[g][sel[g], :] for g in range(ATTN_GROUPS)]
        m_all = jnp.maximum(jnp.maximum(m_g[0], m_g[1]), m_g[2])
        w_g = [jnp.exp2(m - m_all) for m in m_g]
        den = sum(w * ls[g][sel[g], :] for g, w in enumerate(w_g))
        num = sum(w * accs[g][sel[g], :] for g, w in enumerate(w_g))
        out_sc[rows(c, SEQ // r_max, r_max), :] = num / den
    o_ref[...] = out_sc[...].astype(o_ref.dtype)


def _attention(proj3, bias):
    hb = HEAD_DIM
    def col(kind, g):
        off = (kind * ATTN_GROUPS + g) * ATTN_HEADS
        return pl.BlockSpec((None, SEQ, hb), lambda b, h, off=off: (b, 0, off + h))
    in_specs = [col(kind, g) for kind in range(3) for g in range(ATTN_GROUPS)]
    in_specs.append(pl.BlockSpec((ATTN_GROUPS, None, ATTN_BLOCK, 2 * ATTN_BLOCK),
                                 lambda b, h: (0, h, 0, 0)))
    return pl.pallas_call(
        _attn_kernel,
        grid=(BATCH, ATTN_HEADS),
        in_specs=in_specs,
        out_specs=pl.BlockSpec((None, SEQ, hb), lambda b, h: (b, 0, h)),
        out_shape=jax.ShapeDtypeStruct((BATCH, SEQ, BRANCH_WIDTH), bf16),
        scratch_shapes=[pltpu.VMEM((SEQ, hb), f32)] * (3 * ATTN_GROUPS + 1),
        compiler_params=pltpu.CompilerParams(
            dimension_semantics=("arbitrary", "arbitrary"), vmem_limit_bytes=VMEM_LIMIT),
        name="attention",
    )(*([proj3] * 9), bias)


def _t5_bucket(dist):
    exact = NUM_BUCKETS // 2
    d = jnp.maximum(dist, 1).astype(f32)
    large = exact + (jnp.log(d / exact) / math.log(MAX_DISTANCE / exact)
                     * (NUM_BUCKETS - exact)).astype(i32)
    large = jnp.minimum(large, NUM_BUCKETS - 1)
    return jnp.where(dist < exact, dist, large)


def _attention_bias(rel_bias):
    qi = np.arange(ATTN_BLOCK, dtype=np.int32)[:, None]
    kj = np.arange(2 * ATTN_BLOCK, dtype=np.int32)[None, :]
    sub = qi + ATTN_BLOCK - kj
    table = rel_bias.reshape(NUM_BUCKETS, ATTN_GROUPS, ATTN_HEADS)
    out = []
    for g in range(ATTN_GROUPS):
        span = WINDOWS[g] // DILATIONS[g]
        band = (sub >= 0) & (sub <= span)
        bucket = _t5_bucket(jnp.asarray(DILATIONS[g] * np.clip(sub, 0, span), i32))
        onehot = jax.nn.one_hot(bucket.reshape(-1), NUM_BUCKETS, dtype=f32)
        b = jnp.dot(onehot, table[:, g], precision=lax.Precision.HIGHEST)
        b = jnp.where(band[:, :, None], b.reshape(ATTN_BLOCK, 2 * ATTN_BLOCK, ATTN_HEADS), -1e30)
        out.append(b.transpose(2, 0, 1))
    return jnp.stack(out, axis=0).astype(f32) * LOG2E


SGU_TM = 512


def _sgu_kernel(u_ref, v_ref, g_ref, b_ref, w_ref, bst_ref, o_ref):
    vn = _layer_norm(v_ref[...], g_ref[...], b_ref[...]).astype(bf16)
    ri = lax.broadcasted_iota(i32, (SGU_CHUNK, SGU_CHUNK), 0)
    ci = lax.broadcasted_iota(i32, (SGU_CHUNK, SGU_CHUNK), 1)
    causal = ci <= ri
    for g in range(SGU_GROUPS):
        wg = jnp.where(causal, w_ref[g], 0.0).astype(bf16)
        bias = bst_ref[:, g:g + 1]
        cs = slice(g * SGU_CHUNK, (g + 1) * SGU_CHUNK)
        for c in range(SGU_TM // SGU_CHUNK):
            rs = slice(c * SGU_CHUNK, (c + 1) * SGU_CHUNK)
            mixed = jnp.dot(wg, vn[rs, cs], preferred_element_type=f32) + bias
            o_ref[rs, cs] = (u_ref[rs, cs] * mixed).astype(o_ref.dtype)


def _sgu(proj, ln_g, ln_b, w_s, b_st):
    return pl.pallas_call(
        _sgu_kernel,
        grid=(T // SGU_TM,),
        in_specs=[pl.BlockSpec((SGU_TM, BRANCH_WIDTH), lambda i: (i, COL_U)),
                  pl.BlockSpec((SGU_TM, BRANCH_WIDTH), lambda i: (i, COL_V)),
                  pl.BlockSpec((1, BRANCH_WIDTH), lambda i: (0, 0)),
                  pl.BlockSpec((1, BRANCH_WIDTH), lambda i: (0, 0)),
                  pl.BlockSpec((SGU_GROUPS, SGU_CHUNK, SGU_CHUNK), lambda i: (0, 0, 0)),
                  pl.BlockSpec((SGU_CHUNK, SGU_GROUPS), lambda i: (0, 0))],
        out_specs=pl.BlockSpec((SGU_TM, BRANCH_WIDTH), lambda i: (i, 0)),
        out_shape=jax.ShapeDtypeStruct((T, BRANCH_WIDTH), bf16),
        compiler_params=pltpu.CompilerParams(
            dimension_semantics=("arbitrary",), vmem_limit_bytes=VMEM_LIMIT),
        name="sgu",
    )(proj, proj, ln_g, ln_b, w_s, b_st)


BR_TM = 1024
BR_TN = 1024


def _branch_kernel(ya_ref, ys_ref, ga_ref, gs_ref, wa_ref, ws_ref, o_ref):
    a = jnp.dot(ya_ref[...], wa_ref[...], preferred_element_type=f32)
    s = jnp.dot(ys_ref[...], ws_ref[...], preferred_element_type=f32)
    o_ref[...] = (ga_ref[...] * a + gs_ref[...] * s).astype(o_ref.dtype)


def _branch(ya, ys, proj, wa, ws):
    return pl.pallas_call(
        _branch_kernel,
        grid=(T // BR_TM, D_MODEL // BR_TN),
        in_specs=[pl.BlockSpec((BR_TM, BRANCH_WIDTH), lambda i, j: (i, 0)),
                  pl.BlockSpec((BR_TM, BRANCH_WIDTH), lambda i, j: (i, 0)),
                  pl.BlockSpec((BR_TM, BR_TN), lambda i, j: (i, COL_GA + j)),
                  pl.BlockSpec((BR_TM, BR_TN), lambda i, j: (i, COL_GS + j)),
                  pl.BlockSpec((BRANCH_WIDTH, BR_TN), lambda i, j: (0, j)),
                  pl.BlockSpec((BRANCH_WIDTH, BR_TN), lambda i, j: (0, j))],
        out_specs=pl.BlockSpec((BR_TM, BR_TN), lambda i, j: (i, j)),
        out_shape=jax.ShapeDtypeStruct((T, D_MODEL), bf16),
        compiler_params=pltpu.CompilerParams(
            dimension_semantics=("arbitrary", "arbitrary"), vmem_limit_bytes=VMEM_LIMIT),
        name="branch",
    )(ya, ys, proj, proj, wa, ws)


OL_TM = 512


def _out_ln_router_kernel(y_ref, x_ref, wo_ref, g_ref, b_ref, wr_ref, br_ref,
                          h_ref, idx_ref, gate_ref, cnt_ref):
    i = pl.program_id(0)
    mixed = jnp.dot(y_ref[...], wo_ref[...], preferred_element_type=f32)
    h = _layer_norm(ALPHA * x_ref[...] + mixed, g_ref[...], b_ref[...])
    h_ref[...] = h
    h_hi = h.astype(bf16)
    h_lo = (h - h_hi.astype(f32)).astype(bf16)
    t = jnp.dot(h_hi, wr_ref[...], preferred_element_type=f32)
    u = jnp.dot(h_lo, wr_ref[:, :N_EXPERTS], preferred_element_type=f32)
    logits = t[:, :N_EXPERTS] + (t[:, N_EXPERTS:] + u) + br_ref[...]
    lane = lax.broadcasted_iota(i32, (OL_TM, N_EXPERTS), 1).astype(f32)
    lane_k = lax.broadcasted_iota(i32, (OL_TM, TOP_K), 1)
    idx_out = jnp.zeros((OL_TM, TOP_K), f32)
    val_out = jnp.zeros((OL_TM, TOP_K), f32)
    picked = jnp.zeros((OL_TM, N_EXPERTS), f32)
    work = logits
    for k in range(TOP_K):
        m = jnp.max(work, axis=-1, keepdims=True)
        idx = jnp.min(jnp.where(work == m, lane, float(N_EXPERTS)), axis=-1, keepdims=True)
        sel = lane == idx
        idx_out = jnp.where(lane_k == k, idx, idx_out)
        val_out = jnp.where(lane_k == k, m, val_out)
        picked = picked + sel.astype(f32)
        work = jnp.where(sel, -jnp.inf, work)
    e = jnp.exp(val_out - jnp.max(val_out, axis=-1, keepdims=True))
    gate_ref[...] = e / jnp.sum(e, axis=-1, keepdims=True)
    idx_ref[...] = idx_out.astype(i32)

    @pl.when(i == 0)
    def _():
        cnt_ref[...] = jnp.zeros_like(cnt_ref)

    cnt_ref[...] += jnp.sum(picked, axis=0, keepdims=True)


def _out_ln_router(y, x2, wo, ln_g, ln_b, w_router, b_router):
    return pl.pallas_call(
        _out_ln_router_kernel,
        grid=(T // OL_TM,),
        in_specs=[pl.BlockSpec((OL_TM, D_MODEL), lambda i: (i, 0)),
                  pl.BlockSpec((OL_TM, D_MODEL), lambda i: (i, 0)),
                  pl.BlockSpec((D_MODEL, D_MODEL), lambda i: (0, 0)),
                  pl.BlockSpec((1, D_MODEL), lambda i: (0, 0)),
                  pl.BlockSpec((1, D_MODEL), lambda i: (0, 0)),
                  pl.BlockSpec((D_MODEL, 2 * N_EXPERTS), lambda i: (0, 0)),
                  pl.BlockSpec((1, N_EXPERTS), lambda i: (0, 0))],
        out_specs=[pl.BlockSpec((OL_TM, D_MODEL), lambda i: (i, 0)),
                   pl.BlockSpec((OL_TM, TOP_K), lambda i: (i, 0)),
                   pl.BlockSpec((OL_TM, TOP_K), lambda i: (i, 0)),
                   pl.BlockSpec((1, N_EXPERTS), lambda i: (0, 0))],
        out_shape=[jax.ShapeDtypeStruct((T, D_MODEL), f32),
                   jax.ShapeDtypeStruct((T, TOP_K), i32),
                   jax.ShapeDtypeStruct((T, TOP_K), f32),
                   jax.ShapeDtypeStruct((1, N_EXPERTS), f32)],
        compiler_params=pltpu.CompilerParams(
            dimension_semantics=("arbitrary",), vmem_limit_bytes=VMEM_LIMIT),
        name="out_ln_router",
    )(y, x2, wo, ln_g, ln_b, w_router, b_router)


RK_TM = 512


def _rank_kernel(idx_ref, pstart_ref, dest_ref, carry_sc):
    i = pl.program_id(0)

    @pl.when(i == 0)
    def _():
        carry_sc[...] = jnp.zeros_like(carry_sc)

    idx = idx_ref[...]
    lane = lax.broadcasted_iota(i32, (RK_TM, N_EXPERTS), 1)
    picked = jnp.zeros((RK_TM, N_EXPERTS), f32)
    for k in range(TOP_K):
        picked = picked + (lane == idx[:, k:k + 1]).astype(f32)
    ri = lax.broadcasted_iota(i32, (RK_TM, RK_TM), 0)
    ci = lax.broadcasted_iota(i32, (RK_TM, RK_TM), 1)
    lower = (ci < ri).astype(bf16)
    before = jnp.dot(lower, picked.astype(bf16), preferred_element_type=f32)
    slot = before + carry_sc[...] + pstart_ref[...]
    lane_k = lax.broadcasted_iota(i32, (RK_TM, TOP_K), 1)
    dest = jnp.zeros((RK_TM, TOP_K), f32)
    for k in range(TOP_K):
        d = jnp.sum(jnp.where(lane == idx[:, k:k + 1], slot, 0.0), axis=-1, keepdims=True)
        dest = jnp.where(lane_k == k, d, dest)
    dest_ref[...] = dest.astype(i32)
    carry_sc[...] += jnp.sum(picked, axis=0, keepdims=True)


def _rank(idx, pstart):
    return pl.pallas_call(
        _rank_kernel,
        grid=(T // RK_TM,),
        in_specs=[pl.BlockSpec((RK_TM, TOP_K), lambda i: (i, 0)),
                  pl.BlockSpec((1, N_EXPERTS), lambda i: (0, 0))],
        out_specs=pl.BlockSpec((RK_TM, TOP_K), lambda i: (i, 0)),
        out_shape=jax.ShapeDtypeStruct((T, TOP_K), i32),
        scratch_shapes=[pltpu.VMEM((1, N_EXPERTS), f32)],
        compiler_params=pltpu.CompilerParams(dimension_semantics=("arbitrary",)),
        name="rank",
    )(idx, pstart)


DP_TM = 256
HALF = D_MODEL // 2


def _pack_bf16_pairs(x):
    bits = lax.bitcast_convert_type(x.astype(bf16).astype(f32), i32)
    return bits[:, HALF:] | lax.shift_right_logical(bits[:, :HALF], 16)


def _unpack_bf16_pairs(w):
    lo = lax.bitcast_convert_type(lax.shift_left(w, 16), f32)
    hi = lax.bitcast_convert_type(w & jnp.int32(-65536), f32)
    return jnp.concatenate([lo, hi], axis=1).astype(bf16)


ZROWS = 256
SUBLANES = 8


def _zero_fill(pstart_ref, plen_ref, meta_ref, xs_hbm, zbuf, zsem, wait):
    def go(cp):
        if wait:
            cp.wait()
        else:
            cp.start()

    def expert(e, carry):
        start, n = pstart_ref[e], plen_ref[e]
        head = n & (SUBLANES - 1)
        for t in range(SUBLANES - 1):
            @pl.when(t < head)
            def _():
                go(pltpu.make_async_copy(zbuf.at[pl.ds(0, 1)], xs_hbm.at[pl.ds(start + t, 1)],
                                         zsem.at[0]))
        b = SUBLANES
        while b <= ZROWS:
            @pl.when((n & b) != 0)
            def _(b=b):
                off = pl.multiple_of(start + head + (n & ~(2 * b - 1)), SUBLANES)
                go(pltpu.make_async_copy(zbuf.at[pl.ds(0, b)], xs_hbm.at[pl.ds(off, b)],
                                         zsem.at[0]))
            b *= 2
        return carry

    lax.fori_loop(0, N_EXPERTS, expert, 0)

    def unowned(blk, carry):
        for part in range(BM // ZROWS):
            off = pl.multiple_of(blk * BM + part * ZROWS, ZROWS)
            go(pltpu.make_async_copy(zbuf, xs_hbm.at[pl.ds(off, ZROWS)], zsem.at[0]))
        return carry

    lax.fori_loop(meta_ref[0], NB, unowned, 0)


def _dispatch_kernel(pstart_ref, plen_ref, meta_ref, dest_ref, h_ref, xs_hbm,
                     pack_sc, zbuf, sem, zsem):
    first = pl.program_id(0) == 0

    @pl.when(first)
    def _():
        zbuf[...] = jnp.zeros_like(zbuf)
        _zero_fill(pstart_ref, plen_ref, meta_ref, xs_hbm, zbuf, zsem, wait=False)

    pack_sc[...] = _pack_bf16_pairs(h_ref[...])

    def issue(t, carry):
        for k in range(TOP_K):
            pltpu.make_async_copy(pack_sc.at[pl.ds(t, 1)],
                                  xs_hbm.at[pl.ds(dest_ref[t * TOP_K + k], 1)], sem.at[0]).start()
        return carry

    lax.fori_loop(0, DP_TM, issue, 0)

    @pl.when(first)
    def _():
        _zero_fill(pstart_ref, plen_ref, meta_ref, xs_hbm, zbuf, zsem, wait=True)

    for _ in range(TOP_K):
        pltpu.make_async_copy(pack_sc, xs_hbm.at[pl.ds(0, DP_TM)], sem.at[0]).wait()


def _dispatch(h, dest_flat, pad_start, pad_len, meta):
    grid_spec = pltpu.PrefetchScalarGridSpec(
        num_scalar_prefetch=3,
        grid=(T // DP_TM,),
        in_specs=[pl.BlockSpec((DP_TM * TOP_K,), lambda i, *_: (i,), memory_space=pltpu.SMEM),
                  pl.BlockSpec((DP_TM, D_MODEL), lambda i, *_: (i, 0))],
        out_specs=pl.BlockSpec(memory_space=pl.ANY),
        scratch_shapes=[pltpu.VMEM((DP_TM, HALF), i32), pltpu.VMEM((ZROWS, HALF), i32),
                        pltpu.SemaphoreType.DMA((1,)), pltpu.SemaphoreType.DMA((1,))],
    )
    return pl.pallas_call(
        _dispatch_kernel,
        grid_spec=grid_spec,
        out_shape=jax.ShapeDtypeStruct((CAP, HALF), i32),
        compiler_params=pltpu.CompilerParams(dimension_semantics=("arbitrary",)),
        name="dispatch",
    )(pad_start, pad_len, meta, dest_flat, h)


GU_TN = 1024
DN_TN = 2048


def _stream_expert_weights(bs_ref, se_ref, meta_ref, w_hbm, col_offsets, tn, wbuf, sem, caches):
    j, i, nj = pl.program_id(0), pl.program_id(1), pl.num_programs(0)
    n_seg = jnp.maximum(meta_ref[1], 1)
    seg = bs_ref[i]
    first_block = (i == 0) | (seg != bs_ref[jnp.maximum(i - 1, 0)])

    def copies(s, jj, slot):
        expert = se_ref[s]
        out = []
        for c, off in enumerate(col_offsets):
            start = off + jj * tn
            if not isinstance(start, int):
                start = pl.multiple_of(start, tn)
            out.append(pltpu.make_async_copy(w_hbm.at[expert, :, pl.ds(start, tn)],
                                             wbuf.at[slot, c], sem.at[slot]))
        return out

    @pl.when(first_block)
    def _():
        seq = j * n_seg + seg
        slot = seq % 2

        @pl.when(seq == 0)
        def _():
            for cp in copies(0, 0, 0):
                cp.start()

        more = seg + 1 < n_seg

        @pl.when(more | (j + 1 < nj))
        def _():
            for cp in copies(jnp.where(more, seg + 1, 0), jnp.where(more, j, j + 1), 1 - slot):
                cp.start()

        for cp in copies(seg, j, slot):
            cp.wait()
        for c, cache in enumerate(caches):
            cache[...] = wbuf[slot, c].astype(bf16)


def _expert_gu_kernel(be_ref, bs_ref, se_ref, meta_ref, x_ref, w_hbm, bg_ref, bl_ref, o_ref,
                      wbuf, wg_sc, wl_sc, sem):
    del be_ref
    i = pl.program_id(1)
    last = jnp.maximum(meta_ref[0], 1) - 1

    @pl.when(i <= last)
    def _():
        _stream_expert_weights(bs_ref, se_ref, meta_ref, w_hbm, (0, D_MODEL), GU_TN,
                               wbuf, sem, (wg_sc, wl_sc))
        x = _unpack_bf16_pairs(x_ref[...])
        glu = jnp.dot(x, wg_sc[...], preferred_element_type=f32) + bg_ref[...]
        lin = jnp.dot(x, wl_sc[...], preferred_element_type=f32) + bl_ref[...]
        glu = jnp.minimum(glu, SWIGLU_LIMIT)
        lin = jnp.clip(lin, -SWIGLU_LIMIT, SWIGLU_LIMIT)
        o_ref[...] = (glu * jax.nn.sigmoid(SWIGLU_ALPHA * glu) * (lin + 1.0)).astype(o_ref.dtype)

    @pl.when(i > last)
    def _():
        o_ref[...] = jnp.zeros_like(o_ref)


def _used_block(j, i, be, bs, se, meta):
    return jnp.minimum(i, jnp.maximum(meta[0], 1) - 1)


def _expert_gu(layout, xs, w_gu, b_gu):
    nj = D_MODEL // GU_TN
    blk = _used_block
    grid_spec = pltpu.PrefetchScalarGridSpec(
        num_scalar_prefetch=4,
        grid=(nj, NB),
        in_specs=[
            pl.BlockSpec((BM, HALF), lambda *a: (blk(*a), 0)),
            pl.BlockSpec(memory_space=pl.ANY),
            pl.BlockSpec((None, 1, GU_TN), lambda *a: (a[2][blk(*a)], 0, a[0])),
            pl.BlockSpec((None, 1, GU_TN), lambda *a: (a[2][blk(*a)], 0, nj + a[0])),
        ],
        out_specs=pl.BlockSpec((BM, GU_TN), lambda j, i, *_: (i, j)),
        scratch_shapes=[pltpu.VMEM((2, 2, D_MODEL, GU_TN), f32),
                        pltpu.VMEM((D_MODEL, GU_TN), bf16), pltpu.VMEM((D_MODEL, GU_TN), bf16),
                        pltpu.SemaphoreType.DMA((2,))],
    )
    return pl.pallas_call(
        _expert_gu_kernel,
        grid_spec=grid_spec,
        out_shape=jax.ShapeDtypeStruct((CAP, D_MODEL), bf16),
        compiler_params=pltpu.CompilerParams(
            dimension_semantics=("arbitrary", "arbitrary"), vmem_limit_bytes=VMEM_LIMIT),
        name="expert_gu",
    )(*layout, xs, w_gu, b_gu, b_gu)


def _expert_down_kernel(be_ref, bs_ref, se_ref, meta_ref, a_ref, w_hbm, b_ref, o_ref,
                        wbuf, w_sc, sem):
    del be_ref
    i = pl.program_id(1)
    last = jnp.maximum(meta_ref[0], 1) - 1

    @pl.when(i <= last)
    def _():
        _stream_expert_weights(bs_ref, se_ref, meta_ref, w_hbm, (0,), DN_TN, wbuf, sem, (w_sc,))
        o_ref[...] = jnp.dot(a_ref[...], w_sc[...], preferred_element_type=f32) + b_ref[...]

    @pl.when(i > last)
    def _():
        o_ref[...] = jnp.zeros_like(o_ref)


def _expert_down(layout, act, w_down, b_down):
    blk = _used_block
    grid_spec = pltpu.PrefetchScalarGridSpec(
        num_scalar_prefetch=4,
        grid=(D_MODEL // DN_TN, NB),
        in_specs=[
            pl.BlockSpec((BM, D_MODEL), lambda *a: (blk(*a), 0)),
            pl.BlockSpec(memory_space=pl.ANY),
            pl.BlockSpec((None, 1, DN_TN), lambda *a: (a[2][blk(*a)], 0, a[0])),
        ],
        out_specs=pl.BlockSpec((BM, DN_TN), lambda j, i, *_: (i, j)),
        scratch_shapes=[pltpu.VMEM((2, 1, D_MODEL, DN_TN), f32),
                        pltpu.VMEM((D_MODEL, DN_TN), bf16),
                        pltpu.SemaphoreType.DMA((2,))],
    )
    return pl.pallas_call(
        _expert_down_kernel,
        grid_spec=grid_spec,
        out_shape=jax.ShapeDtypeStruct((CAP, D_MODEL), f32),
        compiler_params=pltpu.CompilerParams(
            dimension_semantics=("arbitrary", "arbitrary"), vmem_limit_bytes=VMEM_LIMIT),
        name="expert_down",
    )(*layout, act, w_down, b_down)


CB_TM = 128


def _combine_kernel(dcur_ref, dnext_ref, y_hbm, gate_ref, h_ref, g_ref, b_ref, o_ref, buf, sem):
    i = pl.program_id(0)
    slot = i % 2

    def gather(d_ref, s):
        def issue(t, carry):
            for k in range(TOP_K):
                pltpu.make_async_copy(y_hbm.at[pl.ds(d_ref[t * TOP_K + k], 1)],
                                      buf.at[s, k, pl.ds(t, 1)], sem.at[s]).start()
            return carry
        lax.fori_loop(0, CB_TM, issue, 0)

    @pl.when(i == 0)
    def _():
        gather(dcur_ref, 0)

    @pl.when(i + 1 < pl.num_programs(0))
    def _():
        gather(dnext_ref, 1 - slot)

    for k in range(TOP_K):
        pltpu.make_async_copy(y_hbm.at[pl.ds(0, CB_TM)], buf.at[slot, k], sem.at[slot]).wait()
    gate = gate_ref[...]
    ffn = gate[:, 0:1] * buf[slot, 0]
    for k in range(1, TOP_K):
        ffn = ffn + gate[:, k:k + 1] * buf[slot, k]
    o_ref[...] = _layer_norm(ALPHA * h_ref[...] + ffn, g_ref[...], b_ref[...])


def _combine(y, dest_flat, gate, h, ln_g, ln_b):
    n_tiles = T // CB_TM
    return pl.pallas_call(
        _combine_kernel,
        grid=(n_tiles,),
        in_specs=[pl.BlockSpec((CB_TM * TOP_K,), lambda i: (i,), memory_space=pltpu.SMEM),
                  pl.BlockSpec((CB_TM * TOP_K,), lambda i: (jnp.minimum(i + 1, n_tiles - 1),),
                               memory_space=pltpu.SMEM),
                  pl.BlockSpec(memory_space=pl.ANY),
                  pl.BlockSpec((CB_TM, TOP_K), lambda i: (i, 0)),
                  pl.BlockSpec((CB_TM, D_MODEL), lambda i: (i, 0)),
                  pl.BlockSpec((1, D_MODEL), lambda i: (0, 0)),
                  pl.BlockSpec((1, D_MODEL), lambda i: (0, 0))],
        out_specs=pl.BlockSpec((CB_TM, D_MODEL), lambda i: (i, 0)),
        out_shape=jax.ShapeDtypeStruct((T, D_MODEL), f32),
        scratch_shapes=[pltpu.VMEM((2, TOP_K, CB_TM, D_MODEL), f32),
                        pltpu.SemaphoreType.DMA((2,))],
        compiler_params=pltpu.CompilerParams(dimension_semantics=("arbitrary",),
                                             vmem_limit_bytes=VMEM_LIMIT),
        name="combine",
    )(dest_flat, dest_flat, y, gate, h, ln_g, ln_b)


def _expert_layout(counts):
    nblk = (counts + BM - 1) // BM
    bend = jnp.cumsum(nblk)
    pstart = (bend - nblk) * BM
    before = bend[None, :] <= jnp.arange(NB, dtype=i32)[:, None]
    block_e = jnp.minimum(jnp.sum(before.astype(i32), axis=1), N_EXPERTS - 1)
    has = nblk > 0
    block_seg = jnp.sum((before & has[None, :]).astype(i32), axis=1)
    seg_of_e = jnp.cumsum(has.astype(i32)) - 1
    ids = jnp.arange(N_EXPERTS, dtype=i32)
    seg_expert = jnp.sum(jnp.where(has[None, :] & (seg_of_e[None, :] == ids[:, None]),
                                   ids[None, :], 0), axis=1)
    meta = jnp.stack([bend[-1], jnp.sum(has.astype(i32))]).astype(i32)
    padding = ((pstart + counts).astype(i32), (nblk * BM - counts).astype(i32))
    return pstart, padding, (block_e, block_seg.astype(i32), seg_expert.astype(i32), meta)


def kernel(x, w_in, rel_bias, sgu_ln_g, sgu_ln_b, sgu_w, sgu_b, w_branch, w_out, ln1_g, ln1_b,
           w_router, b_router, w_gu, b_gu, w_down, b_down, ln2_g, ln2_b):
    x2 = x.reshape(T, D_MODEL)
    proj = _inproj(x2, w_in[0].astype(bf16))
    y_attn = _attention(proj.reshape(BATCH, SEQ, IN_COLS), _attention_bias(rel_bias))
    y_sgu = _sgu(proj, sgu_ln_g, sgu_ln_b, sgu_w[0], sgu_b[0].T)
    y = _branch(y_attn.reshape(T, BRANCH_WIDTH), y_sgu, proj,
                w_branch[0, 0].astype(bf16), w_branch[0, 1].astype(bf16))
    wr_hi = w_router[0].astype(bf16)
    wr_lo = (w_router[0] - wr_hi.astype(f32)).astype(bf16)
    h, idx, gate, counts = _out_ln_router(y, x2, w_out[0].astype(bf16), ln1_g, ln1_b,
                                          jnp.concatenate([wr_hi, wr_lo], axis=1), b_router)
    pstart, padding, layout = _expert_layout(counts[0].astype(i32))
    dest = _rank(idx, pstart.astype(f32)[None, :]).reshape(T * TOP_K)
    xs = _dispatch(h, dest, *padding, layout[-1])
    act = _expert_gu(layout, xs, w_gu.reshape(N_EXPERTS, D_MODEL, 2 * D_MODEL),
                     b_gu.reshape(N_EXPERTS, 1, 2 * D_MODEL))
    ye = _expert_down(layout, act, w_down.reshape(N_EXPERTS, D_MODEL, D_MODEL),
                      b_down.reshape(N_EXPERTS, 1, D_MODEL))
    out = _combine(ye, dest, gate, h, ln2_g, ln2_b)
    return out.reshape(BATCH, SEQ, D_MODEL)
```

```python
import functools
import math

import numpy as np
import jax
import jax.numpy as jnp
from jax import lax
from jax.experimental import pallas as pl
from jax.experimental.pallas import tpu as pltpu

D_MODEL = 2048
BATCH = 8
SEQ = 2048
T = BATCH * SEQ
BRANCH_WIDTH = D_MODEL // 2
HEAD_DIM = 128
ATTN_GROUPS = 3
WINDOWS = (128, 512, 2048)
DILATIONS = (1, 4, 16)
ATTN_HEADS = BRANCH_WIDTH // HEAD_DIM
QKV_WIDTH = ATTN_GROUPS * ATTN_HEADS * HEAD_DIM
ATTN_BLOCK = 128
NUM_BUCKETS = 32
MAX_DISTANCE = 2048
SGU_CHUNK = 128
SGU_GROUPS = BRANCH_WIDTH // SGU_CHUNK
IN_COLS = 3 * QKV_WIDTH + 2 * BRANCH_WIDTH + 2 * D_MODEL
N_EXPERTS = 32
TOP_K = 4
SWIGLU_ALPHA = 1.702
SWIGLU_LIMIT = 7.0
LN_EPS = 1e-5
ALPHA = 2.0 ** 0.25
LOG2E = math.log2(math.e)
ATTN_CHUNK = 16

COL_U = 3 * QKV_WIDTH // 1024
COL_V = COL_U + 1
COL_GA = COL_U + 2
COL_GS = COL_GA + 2

BM = 512
NB = (T * TOP_K + N_EXPERTS * (BM - 1) + BM - 1) // BM
CAP = NB * BM

VMEM_LIMIT = 56 * 1024 * 1024

f32 = jnp.float32
bf16 = jnp.bfloat16
i32 = jnp.int32


def _layer_norm(x, g, b):
    mu = jnp.mean(x, axis=-1, keepdims=True)
    xc = x - mu
    var = jnp.mean(xc * xc, axis=-1, keepdims=True)
    return xc * lax.rsqrt(var + LN_EPS) * g + b


IP_TM = 1024
IP_TN = 1024


IP_CHUNK = 256


def _inproj_kernel(x_ref, w_ref, o_ref, xb_sc):
    j = pl.program_id(1)

    @pl.when(j == 0)
    def _():
        xb_sc[...] = x_ref[...].astype(bf16)

    def project(epilogue):
        for c in range(0, IP_TN, IP_CHUNK):
            acc = jnp.dot(xb_sc[...], w_ref[:, c:c + IP_CHUNK], preferred_element_type=f32)
            o_ref[:, c:c + IP_CHUNK] = epilogue(acc)

    @pl.when(j < COL_U)
    def _():
        project(lambda a: a)

    @pl.when((j >= COL_U) & (j < COL_GA))
    def _():
        project(jax.nn.gelu)

    @pl.when(j >= COL_GA)
    def _():
        project(jax.nn.sigmoid)


def _inproj(x2, wb):
    return pl.pallas_call(
        _inproj_kernel,
        grid=(T // IP_TM, IN_COLS // IP_TN),
        in_specs=[pl.BlockSpec((IP_TM, D_MODEL), lambda i, j: (i, 0)),
                  pl.BlockSpec((D_MODEL, IP_TN), lambda i, j: (0, j))],
        out_specs=pl.BlockSpec((IP_TM, IP_TN), lambda i, j: (i, j)),
        out_shape=jax.ShapeDtypeStruct((T, IN_COLS), f32),
        scratch_shapes=[pltpu.VMEM((IP_TM, D_MODEL), bf16)],
        compiler_params=pltpu.CompilerParams(
            dimension_semantics=("arbitrary", "arbitrary"), vmem_limit_bytes=VMEM_LIMIT),
        name="inproj",
    )(x2, wb)


def _attn_kernel(q0, q1, q2, k0, k1, k2, v0, v1, v2, bias_ref, o_ref,
                 m0, m1, m2, l0, l1, l2, a0, a1, a2, out_sc):
    qs, ks, vs = (q0, q1, q2), (k0, k1, k2), (v0, v1, v2)
    ms, ls, accs = (m0, m1, m2), (l0, l1, l2), (a0, a1, a2)
    scale = HEAD_DIM ** -0.5 * LOG2E
    QB = ATTN_BLOCK

    def rows(start, n, r):
        return pl.ds(start, n, stride=r) if r > 1 else pl.ds(start, n)

    def tiles(g, specs):
        r = DILATIONS[g]
        nk = specs[0][2]
        bias = bias_ref[g, :, 2 * QB - nk:]
        s, v = [], []
        for q_start, k_start, _ in specs:
            q = qs[g][rows(q_start, QB, r), :].astype(bf16)
            k = ks[g][rows(k_start, nk, r), :].astype(bf16)
            v.append(vs[g][rows(k_start, nk, r), :].astype(bf16))
            s.append(lax.dot_general(q, k, (((1,), (1,)), ((), ())), preferred_element_type=f32))
        s = jnp.stack(s) * scale + bias[None]
        m = jnp.max(s, axis=-1, keepdims=True)
        p = jnp.exp2(s - m)
        l = jnp.sum(p, axis=-1, keepdims=True)
        p = p.astype(bf16)
        out = []
        for t in range(len(specs)):
            pv = jnp.dot(p[t], v[t], preferred_element_type=f32)
            out.append((jnp.broadcast_to(m[t], pv.shape), jnp.broadcast_to(l[t], pv.shape), pv))
        return out

    for g in range(ATTN_GROUPS):
        r = DILATIONS[g]
        sub_len = SEQ // r
        first, rest = [], []
        for c in range(r):
            for n in range(sub_len // QB):
                q_start = c + n * QB * r
                dst = pl.ds(c * sub_len + n * QB, QB)
                if n == 0:
                    first.append(((q_start, q_start, QB), dst))
                else:
                    rest.append(((q_start, q_start - QB * r, 2 * QB), dst))
        for todo in (first, rest):
            for at in range(0, len(todo), ATTN_CHUNK):
                chunk = todo[at:at + ATTN_CHUNK]
                for (_, dst), (m, l, pv) in zip(chunk, tiles(g, [spec for spec, _ in chunk])):
                    ms[g][dst, :] = m
                    ls[g][dst, :] = l
                    accs[g][dst, :] = pv

    r_max = DILATIONS[-1]
    for c in range(r_max):
        sel = []
        for g in range(ATTN_GROUPS):
            r = DILATIONS[g]
            sel.append(rows((c % r) * (SEQ // r) + c // r, SEQ // r_max, r_max // r))
        m_g = [ms[g][sel[g], :] for g in range(ATTN_GROUPS)]
        m_all = jnp.maximum(jnp.maximum(m_g[0], m_g[1]), m_g[2])
        w_g = [jnp.exp2(m - m_all) for m in m_g]
        den = sum(w * ls[g][sel[g], :] for g, w in enumerate(w_g))
        num = sum(w * accs[g][sel[g], :] for g, w in enumerate(w_g))
        out_sc[rows(c, SEQ // r_max, r_max), :] = num / den
    o_ref[...] = out_sc[...].astype(o_ref.dtype)


def _attention(proj3, bias):
    hb = HEAD_DIM
    def col(kind, g):
        off = (kind * ATTN_GROUPS + g) * ATTN_HEADS
        return pl.BlockSpec((None, SEQ, hb), lambda b, h, off=off: (b, 0, off + h))
    in_specs = [col(kind, g) for kind in range(3) for g in range(ATTN_GROUPS)]
    in_specs.append(pl.BlockSpec((ATTN_GROUPS, None, ATTN_BLOCK, 2 * ATTN_BLOCK),
                                 lambda b, h: (0, h, 0, 0)))
    return pl.pallas_call(
        _attn_kernel,
        grid=(BATCH, ATTN_HEADS),
        in_specs=in_specs,
        out_specs=pl.BlockSpec((None, SEQ, hb), lambda b, h: (b, 0, h)),
        out_shape=jax.ShapeDtypeStruct((BATCH, SEQ, BRANCH_WIDTH), bf16),
        scratch_shapes=[pltpu.VMEM((SEQ, hb), f32)] * (3 * ATTN_GROUPS + 1),
        compiler_params=pltpu.CompilerParams(
            dimension_semantics=("arbitrary", "arbitrary"), vmem_limit_bytes=VMEM_LIMIT),
        name="attention",
    )(*([proj3] * 9), bias)


def _t5_bucket(dist):
    exact = NUM_BUCKETS // 2
    d = jnp.maximum(dist, 1).astype(f32)
    large = exact + (jnp.log(d / exact) / math.log(MAX_DISTANCE / exact)
                     * (NUM_BUCKETS - exact)).astype(i32)
    large = jnp.minimum(large, NUM_BUCKETS - 1)
    return jnp.where(dist < exact, dist, large)


def _attention_bias(rel_bias):
    qi = np.arange(ATTN_BLOCK, dtype=np.int32)[:, None]
    kj = np.arange(2 * ATTN_BLOCK, dtype=np.int32)[None, :]
    sub = qi + ATTN_BLOCK - kj
    table = rel_bias.reshape(NUM_BUCKETS, ATTN_GROUPS, ATTN_HEADS)
    out = []
    for g in range(ATTN_GROUPS):
        span = WINDOWS[g] // DILATIONS[g]
        band = (sub >= 0) & (sub <= span)
        bucket = _t5_bucket(jnp.asarray(DILATIONS[g] * np.clip(sub, 0, span), i32))
        onehot = jax.nn.one_hot(bucket.reshape(-1), NUM_BUCKETS, dtype=f32)
        b = jnp.dot(onehot, table[:, g], precision=lax.Precision.HIGHEST)
        b = jnp.where(band[:, :, None], b.reshape(ATTN_BLOCK, 2 * ATTN_BLOCK, ATTN_HEADS), -1e30)
        out.append(b.transpose(2, 0, 1))
    return jnp.stack(out, axis=0).astype(f32) * LOG2E


SGU_TM = 512


def _sgu_kernel(u_ref, v_ref, g_ref, b_ref, w_ref, bst_ref, o_ref):
    vn = _layer_norm(v_ref[...], g_ref[...], b_ref[...]).astype(bf16)
    ri = lax.broadcasted_iota(i32, (SGU_CHUNK, SGU_CHUNK), 0)
    ci = lax.broadcasted_iota(i32, (SGU_CHUNK, SGU_CHUNK), 1)
    causal = ci <= ri
    for g in range(SGU_GROUPS):
        wg = jnp.where(causal, w_ref[g], 0.0).astype(bf16)
        bias = bst_ref[:, g:g + 1]
        cs = slice(g * SGU_CHUNK, (g + 1) * SGU_CHUNK)
        for c in range(SGU_TM // SGU_CHUNK):
            rs = slice(c * SGU_CHUNK, (c + 1) * SGU_CHUNK)
            mixed = jnp.dot(wg, vn[rs, cs], preferred_element_type=f32) + bias
            o_ref[rs, cs] = (u_ref[rs, cs] * mixed).astype(o_ref.dtype)


def _sgu(proj, ln_g, ln_b, w_s, b_st):
    return pl.pallas_call(
        _sgu_kernel,
        grid=(T // SGU_TM,),
        in_specs=[pl.BlockSpec((SGU_TM, BRANCH_WIDTH), lambda i: (i, COL_U)),
                  pl.BlockSpec((SGU_TM, BRANCH_WIDTH), lambda i: (i, COL_V)),
                  pl.BlockSpec((1, BRANCH_WIDTH), lambda i: (0, 0)),
                  pl.BlockSpec((1, BRANCH_WIDTH), lambda i: (0, 0)),
                  pl.BlockSpec((SGU_GROUPS, SGU_CHUNK, SGU_CHUNK), lambda i: (0, 0, 0)),
                  pl.BlockSpec((SGU_CHUNK, SGU_GROUPS), lambda i: (0, 0))],
        out_specs=pl.BlockSpec((SGU_TM, BRANCH_WIDTH), lambda i: (i, 0)),
        out_shape=jax.ShapeDtypeStruct((T, BRANCH_WIDTH), bf16),
        compiler_params=pltpu.CompilerParams(
            dimension_semantics=("arbitrary",), vmem_limit_bytes=VMEM_LIMIT),
        name="sgu",
    )(proj, proj, ln_g, ln_b, w_s, b_st)


BR_TM = 1024
BR_TN = 1024


def _branch_kernel(ya_ref, ys_ref, ga_ref, gs_ref, wa_ref, ws_ref, o_ref):
    a = jnp.dot(ya_ref[...], wa_ref[...], preferred_element_type=f32)
    s = jnp.dot(ys_ref[...], ws_ref[...], preferred_element_type=f32)
    o_ref[...] = (ga_ref[...] * a + gs_ref[...] * s).astype(o_ref.dtype)


def _branch(ya, ys, proj, wa, ws):
    return pl.pallas_call(
        _branch_kernel,
        grid=(T // BR_TM, D_MODEL // BR_TN),
        in_specs=[pl.BlockSpec((BR_TM, BRANCH_WIDTH), lambda i, j: (i, 0)),
                  pl.BlockSpec((BR_TM, BRANCH_WIDTH), lambda i, j: (i, 0)),
                  pl.BlockSpec((BR_TM, BR_TN), lambda i, j: (i, COL_GA + j)),
                  pl.BlockSpec((BR_TM, BR_TN), lambda i, j: (i, COL_GS + j)),
                  pl.BlockSpec((BRANCH_WIDTH, BR_TN), lambda i, j: (0, j)),
                  pl.BlockSpec((BRANCH_WIDTH, BR_TN), lambda i, j: (0, j))],
        out_specs=pl.BlockSpec((BR_TM, BR_TN), lambda i, j: (i, j)),
        out_shape=jax.ShapeDtypeStruct((T, D_MODEL), bf16),
        compiler_params=pltpu.CompilerParams(
            dimension_semantics=("arbitrary", "arbitrary"), vmem_limit_bytes=VMEM_LIMIT),
        name="branch",
    )(ya, ys, proj, proj, wa, ws)


OL_TM = 512


def _out_ln_router_kernel(y_ref, x_ref, wo_ref, g_ref, b_ref, wr_ref, br_ref,
                          h_ref, idx_ref, gate_ref, cnt_ref):
    i = pl.program_id(0)
    mixed = jnp.dot(y_ref[...], wo_ref[...], preferred_element_type=f32)
    h = _layer_norm(ALPHA * x_ref[...] + mixed, g_ref[...], b_ref[...])
    h_ref[...] = h
    h_hi = h.astype(bf16)
    h_lo = (h - h_hi.astype(f32)).astype(bf16)
    t = jnp.dot(h_hi, wr_ref[...], preferred_element_type=f32)
    u = jnp.dot(h_lo, wr_ref[:, :N_EXPERTS], preferred_element_type=f32)
    logits = t[:, :N_EXPERTS] + (t[:, N_EXPERTS:] + u) + br_ref[...]
    lane = lax.broadcasted_iota(i32, (OL_TM, N_EXPERTS), 1).astype(f32)
    lane_k = lax.broadcasted_iota(i32, (OL_TM, TOP_K), 1)
    idx_out = jnp.zeros((OL_TM, TOP_K), f32)
    val_out = jnp.zeros((OL_TM, TOP_K), f32)
    picked = jnp.zeros((OL_TM, N_EXPERTS), f32)
    work = logits
    for k in range(TOP_K):
        m = jnp.max(work, axis=-1, keepdims=True)
        idx = jnp.min(jnp.where(work == m, lane, float(N_EXPERTS)), axis=-1, keepdims=True)
        sel = lane == idx
        idx_out = jnp.where(lane_k == k, idx, idx_out)
        val_out = jnp.where(lane_k == k, m, val_out)
        picked = picked + sel.astype(f32)
        work = jnp.where(sel, -jnp.inf, work)
    e = jnp.exp(val_out - jnp.max(val_out, axis=-1, keepdims=True))
    gate_ref[...] = e / jnp.sum(e, axis=-1, keepdims=True)
    idx_ref[...] = idx_out.astype(i32)

    @pl.when(i == 0)
    def _():
        cnt_ref[...] = jnp.zeros_like(cnt_ref)

    cnt_ref[...] += jnp.sum(picked, axis=0, keepdims=True)


def _out_ln_router(y, x2, wo, ln_g, ln_b, w_router, b_router):
    return pl.pallas_call(
        _out_ln_router_kernel,
        grid=(T // OL_TM,),
        in_specs=[pl.BlockSpec((OL_TM, D_MODEL), lambda i: (i, 0)),
                  pl.BlockSpec((OL_TM, D_MODEL), lambda i: (i, 0)),
                  pl.BlockSpec((D_MODEL, D_MODEL), lambda i: (0, 0)),
                  pl.BlockSpec((1, D_MODEL), lambda i: (0, 0)),
                  pl.BlockSpec((1, D_MODEL), lambda i: (0, 0)),
                  pl.BlockSpec((D_MODEL, 2 * N_EXPERTS), lambda i: (0, 0)),
                  pl.BlockSpec((1, N_EXPERTS), lambda i: (0, 0))],
        out_specs=[pl.BlockSpec((OL_TM, D_MODEL), lambda i: (i, 0)),
                   pl.BlockSpec((OL_TM, TOP_K), lambda i: (i, 0)),
                   pl.BlockSpec((OL_TM, TOP_K), lambda i: (i, 0)),
                   pl.BlockSpec((1, N_EXPERTS), lambda i: (0, 0))],
        out_shape=[jax.ShapeDtypeStruct((T, D_MODEL), f32),
                   jax.ShapeDtypeStruct((T, TOP_K), i32),
                   jax.ShapeDtypeStruct((T, TOP_K), f32),
                   jax.ShapeDtypeStruct((1, N_EXPERTS), f32)],
        compiler_params=pltpu.CompilerParams(
            dimension_semantics=("arbitrary",), vmem_limit_bytes=VMEM_LIMIT),
        name="out_ln_router",
    )(y, x2, wo, ln_g, ln_b, w_router, b_router)


RK_TM = 512


def _rank_kernel(idx_ref, pstart_ref, dest_ref, carry_sc):
    i = pl.program_id(0)

    @pl.when(i == 0)
    def _():
        carry_sc[...] = jnp.zeros_like(carry_sc)

    idx = idx_ref[...]
    lane = lax.broadcasted_iota(i32, (RK_TM, N_EXPERTS), 1)
    picked = jnp.zeros((RK_TM, N_EXPERTS), f32)
    for k in range(TOP_K):
        picked = picked + (lane == idx[:, k:k + 1]).astype(f32)
    ri = lax.broadcasted_iota(i32, (RK_TM, RK_TM), 0)
    ci = lax.broadcasted_iota(i32, (RK_TM, RK_TM), 1)
    lower = (ci < ri).astype(bf16)
    before = jnp.dot(lower, picked.astype(bf16), preferred_element_type=f32)
    slot = before + carry_sc[...] + pstart_ref[...]
    lane_k = lax.broadcasted_iota(i32, (RK_TM, TOP_K), 1)
    dest = jnp.zeros((RK_TM, TOP_K), f32)
    for k in range(TOP_K):
        d = jnp.sum(jnp.where(lane == idx[:, k:k + 1], slot, 0.0), axis=-1, keepdims=True)
        dest = jnp.where(lane_k == k, d, dest)
    dest_ref[...] = dest.astype(i32)
    carry_sc[...] += jnp.sum(picked, axis=0, keepdims=True)


def _rank(idx, pstart):
    return pl.pallas_call(
        _rank_kernel,
        grid=(T // RK_TM,),
        in_specs=[pl.BlockSpec((RK_TM, TOP_K), lambda i: (i, 0)),
                  pl.BlockSpec((1, N_EXPERTS), lambda i: (0, 0))],
        out_specs=pl.BlockSpec((RK_TM, TOP_K), lambda i: (i, 0)),
        out_shape=jax.ShapeDtypeStruct((T, TOP_K), i32),
        scratch_shapes=[pltpu.VMEM((1, N_EXPERTS), f32)],
        compiler_params=pltpu.CompilerParams(dimension_semantics=("arbitrary",)),
        name="rank",
    )(idx, pstart)


DP_TM = 256
HALF = D_MODEL // 2


def _pack_bf16_pairs(x):
    bits = lax.bitcast_convert_type(x.astype(bf16).astype(f32), i32)
    return bits[:, HALF:] | lax.shift_right_logical(bits[:, :HALF], 16)


def _unpack_bf16_pairs(w):
    lo = lax.bitcast_convert_type(lax.shift_left(w, 16), f32)
    hi = lax.bitcast_convert_type(w & jnp.int32(-65536), f32)
    return jnp.concatenate([lo, hi], axis=1).astype(bf16)


ZROWS = 256
SUBLANES = 8


def _zero_fill(pstart_ref, plen_ref, meta_ref, xs_hbm, zbuf, zsem, wait):
    def go(cp):
        if wait:
            cp.wait()
        else:
            cp.start()

    def expert(e, carry):
        start, n = pstart_ref[e], plen_ref[e]
        head = n & (SUBLANES - 1)
        for t in range(SUBLANES - 1):
            @pl.when(t < head)
            def _():
                go(pltpu.make_async_copy(zbuf.at[pl.ds(0, 1)], xs_hbm.at[pl.ds(start + t, 1)],
                                         zsem.at[0]))
        b = SUBLANES
        while b <= ZROWS:
            @pl.when((n & b) != 0)
            def _(b=b):
                off = pl.multiple_of(start + head + (n & ~(2 * b - 1)), SUBLANES)
                go(pltpu.make_async_copy(zbuf.at[pl.ds(0, b)], xs_hbm.at[pl.ds(off, b)],
                                         zsem.at[0]))
            b *= 2
        return carry

    lax.fori_loop(0, N_EXPERTS, expert, 0)

    def unowned(blk, carry):
        for part in range(BM // ZROWS):
            off = pl.multiple_of(blk * BM + part * ZROWS, ZROWS)
            go(pltpu.make_async_copy(zbuf, xs_hbm.at[pl.ds(off, ZROWS)], zsem.at[0]))
        return carry

    lax.fori_loop(meta_ref[0], NB, unowned, 0)


def _dispatch_kernel(pstart_ref, plen_ref, meta_ref, dest_ref, h_ref, xs_hbm,
                     pack_sc, zbuf, sem, zsem):
    first = pl.program_id(0) == 0

    @pl.when(first)
    def _():
        zbuf[...] = jnp.zeros_like(zbuf)
        _zero_fill(pstart_ref, plen_ref, meta_ref, xs_hbm, zbuf, zsem, wait=False)

    pack_sc[...] = _pack_bf16_pairs(h_ref[...])

    def issue(t, carry):
        for k in range(TOP_K):
            pltpu.make_async_copy(pack_sc.at[pl.ds(t, 1)],
                                  xs_hbm.at[pl.ds(dest_ref[t * TOP_K + k], 1)],
                                  sem.at[0]).start(priority=k % 2)
        return carry

    lax.fori_loop(0, DP_TM, issue, 0)

    @pl.when(first)
    def _():
        _zero_fill(pstart_ref, plen_ref, meta_ref, xs_hbm, zbuf, zsem, wait=True)

    for _ in range(TOP_K):
        pltpu.make_async_copy(pack_sc, xs_hbm.at[pl.ds(0, DP_TM)], sem.at[0]).wait()


def _dispatch(h, dest_flat, pad_start, pad_len, meta):
    grid_spec = pltpu.PrefetchScalarGridSpec(
        num_scalar_prefetch=3,
        grid=(T // DP_TM,),
        in_specs=[pl.BlockSpec((DP_TM * TOP_K,), lambda i, *_: (i,), memory_space=pltpu.SMEM),
                  pl.BlockSpec((DP_TM, D_MODEL), lambda i, *_: (i, 0))],
        out_specs=pl.BlockSpec(memory_space=pl.ANY),
        scratch_shapes=[pltpu.VMEM((DP_TM, HALF), i32), pltpu.VMEM((ZROWS, HALF), i32),
                        pltpu.SemaphoreType.DMA((1,)), pltpu.SemaphoreType.DMA((1,))],
    )
    return pl.pallas_call(
        _dispatch_kernel,
        grid_spec=grid_spec,
        out_shape=jax.ShapeDtypeStruct((CAP, HALF), i32),
        compiler_params=pltpu.CompilerParams(dimension_semantics=("arbitrary",)),
        name="dispatch",
    )(pad_start, pad_len, meta, dest_flat, h)


GU_TN = 1024
DN_TN = 2048


def _stream_expert_weights(bs_ref, se_ref, meta_ref, w_hbm, col_offsets, tn, wbuf, sem, caches):
    j, i, nj = pl.program_id(0), pl.program_id(1), pl.num_programs(0)
    n_seg = jnp.maximum(meta_ref[1], 1)
    seg = bs_ref[i]
    first_block = (i == 0) | (seg != bs_ref[jnp.maximum(i - 1, 0)])

    def copies(s, jj, slot):
        expert = se_ref[s]
        out = []
        for c, off in enumerate(col_offsets):
            start = off + jj * tn
            if not isinstance(start, int):
                start = pl.multiple_of(start, tn)
            out.append(pltpu.make_async_copy(w_hbm.at[expert, :, pl.ds(start, tn)],
                                             wbuf.at[slot, c], sem.at[slot]))
        return out

    @pl.when(first_block)
    def _():
        seq = j * n_seg + seg
        slot = seq % 2

        @pl.when(seq == 0)
        def _():
            for cp in copies(0, 0, 0):
                cp.start()

        more = seg + 1 < n_seg

        @pl.when(more | (j + 1 < nj))
        def _():
            for cp in copies(jnp.where(more, seg + 1, 0), jnp.where(more, j, j + 1), 1 - slot):
                cp.start()

        for cp in copies(seg, j, slot):
            cp.wait()
        for c, cache in enumerate(caches):
            cache[...] = wbuf[slot, c].astype(bf16)


def _expert_gu_kernel(be_ref, bs_ref, se_ref, meta_ref, x_ref, w_hbm, bg_ref, bl_ref, o_ref,
                      wbuf, wg_sc, wl_sc, sem):
    del be_ref
    i = pl.program_id(1)
    last = jnp.maximum(meta_ref[0], 1) - 1

    @pl.when(i <= last)
    def _():
        _stream_expert_weights(bs_ref, se_ref, meta_ref, w_hbm, (0, D_MODEL), GU_TN,
                               wbuf, sem, (wg_sc, wl_sc))
        x = _unpack_bf16_pairs(x_ref[...])
        glu = jnp.dot(x, wg_sc[...], preferred_element_type=f32) + bg_ref[...]
        lin = jnp.dot(x, wl_sc[...], preferred_element_type=f32) + bl_ref[...]
        glu = jnp.minimum(glu, SWIGLU_LIMIT)
        lin = jnp.clip(lin, -SWIGLU_LIMIT, SWIGLU_LIMIT)
        o_ref[...] = (glu * jax.nn.sigmoid(SWIGLU_ALPHA * glu) * (lin + 1.0)).astype(o_ref.dtype)

    @pl.when(i > last)
    def _():
        o_ref[...] = jnp.zeros_like(o_ref)


def _used_block(j, i, be, bs, se, meta):
    return jnp.minimum(i, jnp.maximum(meta[0], 1) - 1)


def _expert_gu(layout, xs, w_gu, b_gu):
    nj = D_MODEL // GU_TN
    blk = _used_block
    grid_spec = pltpu.PrefetchScalarGridSpec(
        num_scalar_prefetch=4,
        grid=(nj, NB),
        in_specs=[
            pl.BlockSpec((BM, HALF), lambda *a: (blk(*a), 0)),
            pl.BlockSpec(memory_space=pl.ANY),
            pl.BlockSpec((None, 1, GU_TN), lambda *a: (a[2][blk(*a)], 0, a[0])),
            pl.BlockSpec((None, 1, GU_TN), lambda *a: (a[2][blk(*a)], 0, nj + a[0])),
        ],
        out_specs=pl.BlockSpec((BM, GU_TN), lambda j, i, *_: (i, j)),
        scratch_shapes=[pltpu.VMEM((2, 2, D_MODEL, GU_TN), f32),
                        pltpu.VMEM((D_MODEL, GU_TN), bf16), pltpu.VMEM((D_MODEL, GU_TN), bf16),
                        pltpu.SemaphoreType.DMA((2,))],
    )
    return pl.pallas_call(
        _expert_gu_kernel,
        grid_spec=grid_spec,
        out_shape=jax.ShapeDtypeStruct((CAP, D_MODEL), bf16),
        compiler_params=pltpu.CompilerParams(
            dimension_semantics=("arbitrary", "arbitrary"), vmem_limit_bytes=VMEM_LIMIT),
        name="expert_gu",
    )(*layout, xs, w_gu, b_gu, b_gu)


def _expert_down_kernel(be_ref, bs_ref, se_ref, meta_ref, a_ref, w_hbm, b_ref, o_ref,
                        wbuf, w_sc, sem):
    del be_ref
    i = pl.program_id(1)
    last = jnp.maximum(meta_ref[0], 1) - 1

    @pl.when(i <= last)
    def _():
        _stream_expert_weights(bs_ref, se_ref, meta_ref, w_hbm, (0,), DN_TN, wbuf, sem, (w_sc,))
        o_ref[...] = jnp.dot(a_ref[...], w_sc[...], preferred_element_type=f32) + b_ref[...]

    @pl.when(i > last)
    def _():
        o_ref[...] = jnp.zeros_like(o_ref)


def _expert_down(layout, act, w_down, b_down):
    blk = _used_block
    grid_spec = pltpu.PrefetchScalarGridSpec(
        num_scalar_prefetch=4,
        grid=(D_MODEL // DN_TN, NB),
        in_specs=[
            pl.BlockSpec((BM, D_MODEL), lambda *a: (blk(*a), 0)),
            pl.BlockSpec(memory_space=pl.ANY),
            pl.BlockSpec((None, 1, DN_TN), lambda *a: (a[2][blk(*a)], 0, a[0])),
        ],
        out_specs=pl.BlockSpec((BM, DN_TN), lambda j, i, *_: (i, j)),
        scratch_shapes=[pltpu.VMEM((2, 1, D_MODEL, DN_TN), f32),
                        pltpu.VMEM((D_MODEL, DN_TN), bf16),
                        pltpu.SemaphoreType.DMA((2,))],
    )
    return pl.pallas_call(
        _expert_down_kernel,
        grid_spec=grid_spec,
        out_shape=jax.ShapeDtypeStruct((CAP, D_MODEL), f32),
        compiler_params=pltpu.CompilerParams(
            dimension_semantics=("arbitrary", "arbitrary"), vmem_limit_bytes=VMEM_LIMIT),
        name="expert_down",
    )(*layout, act, w_down, b_down)


CB_TM = 128
CB_SLABS = 4


def _combine_kernel(dcur_ref, dnext_ref, y_hbm, gate_ref, h_ref, g_ref, b_ref, o_ref,
                    buf_a, buf_b, sem):
    i, n = pl.program_id(0), pl.num_programs(0)

    def start_rows(d_ref, buf, s, t):
        for k in range(TOP_K):
            pltpu.make_async_copy(y_hbm.at[pl.ds(d_ref[t * TOP_K + k], 1)],
                                  buf.at[k, pl.ds(t, 1)], sem.at[s]).start(priority=k % 2)

    def wait_tile(buf, s):
        for k in range(TOP_K):
            pltpu.make_async_copy(y_hbm.at[pl.ds(0, CB_TM)], buf.at[k], sem.at[s]).wait()

    @pl.when(i == 0)
    def _():
        def prime(t, carry):
            start_rows(dcur_ref, buf_a, 0, t)
            return carry
        lax.fori_loop(0, CB_TM, prime, 0)

    def step(cur, s_cur, nxt, s_nxt):
        wait_tile(cur, s_cur)

        def slabs(gi, carry):
            for part in range(CB_SLABS):
                t0 = pl.multiple_of((gi * CB_SLABS + part) * SUBLANES, SUBLANES)
                for tt in range(SUBLANES):
                    start_rows(dnext_ref, nxt, s_nxt, t0 + tt)
                rows = pl.ds(t0, SUBLANES)
                gate = gate_ref[rows, :]
                ffn = gate[:, 0:1] * cur[0, rows, :]
                for k in range(1, TOP_K):
                    ffn = ffn + gate[:, k:k + 1] * cur[k, rows, :]
                o_ref[rows, :] = _layer_norm(ALPHA * h_ref[rows, :] + ffn, g_ref[...], b_ref[...])
            return carry

        lax.fori_loop(0, CB_TM // (SUBLANES * CB_SLABS), slabs, 0)

        @pl.when(i == n - 1)
        def _():
            wait_tile(nxt, s_nxt)

    @pl.when(i % 2 == 0)
    def _():
        step(buf_a, 0, buf_b, 1)

    @pl.when(i % 2 == 1)
    def _():
        step(buf_b, 1, buf_a, 0)


def _combine(y, dest_flat, gate, h, ln_g, ln_b):
    n_tiles = T // CB_TM
    return pl.pallas_call(
        _combine_kernel,
        grid=(n_tiles,),
        in_specs=[pl.BlockSpec((CB_TM * TOP_K,), lambda i: (i,), memory_space=pltpu.SMEM),
                  pl.BlockSpec((CB_TM * TOP_K,), lambda i: (jnp.minimum(i + 1, n_tiles - 1),),
                               memory_space=pltpu.SMEM),
                  pl.BlockSpec(memory_space=pl.ANY),
                  pl.BlockSpec((CB_TM, TOP_K), lambda i: (i, 0)),
                  pl.BlockSpec((CB_TM, D_MODEL), lambda i: (i, 0)),
                  pl.BlockSpec((1, D_MODEL), lambda i: (0, 0)),
                  pl.BlockSpec((1, D_MODEL), lambda i: (0, 0))],
        out_specs=pl.BlockSpec((CB_TM, D_MODEL), lambda i: (i, 0)),
        out_shape=jax.ShapeDtypeStruct((T, D_MODEL), f32),
        scratch_shapes=[pltpu.VMEM((TOP_K, CB_TM, D_MODEL), f32),
                        pltpu.VMEM((TOP_K, CB_TM, D_MODEL), f32),
                        pltpu.SemaphoreType.DMA((2,))],
        compiler_params=pltpu.CompilerParams(dimension_semantics=("arbitrary",),
                                             vmem_limit_bytes=VMEM_LIMIT),
        name="combine",
    )(dest_flat, dest_flat, y, gate, h, ln_g, ln_b)


def _expert_layout(counts):
    nblk = (counts + BM - 1) // BM
    bend = jnp.cumsum(nblk)
    pstart = (bend - nblk) * BM
    before = bend[None, :] <= jnp.arange(NB, dtype=i32)[:, None]
    block_e = jnp.minimum(jnp.sum(before.astype(i32), axis=1), N_EXPERTS - 1)
    has = nblk > 0
    block_seg = jnp.sum((before & has[None, :]).astype(i32), axis=1)
    seg_of_e = jnp.cumsum(has.astype(i32)) - 1
    ids = jnp.arange(N_EXPERTS, dtype=i32)
    seg_expert = jnp.sum(jnp.where(has[None, :] & (seg_of_e[None, :] == ids[:, None]),
                                   ids[None, :], 0), axis=1)
    meta = jnp.stack([bend[-1], jnp.sum(has.astype(i32))]).astype(i32)
    padding = ((pstart + counts).astype(i32), (nblk * BM - counts).astype(i32))
    return pstart, padding, (block_e, block_seg.astype(i32), seg_expert.astype(i32), meta)


def kernel(x, w_in, rel_bias, sgu_ln_g, sgu_ln_b, sgu_w, sgu_b, w_branch, w_out, ln1_g, ln1_b,
           w_router, b_router, w_gu, b_gu, w_down, b_down, ln2_g, ln2_b):
    x2 = x.reshape(T, D_MODEL)
    proj = _inproj(x2, w_in[0].astype(bf16))
    y_attn = _attention(proj.reshape(BATCH, SEQ, IN_COLS), _attention_bias(rel_bias))
    y_sgu = _sgu(proj, sgu_ln_g, sgu_ln_b, sgu_w[0], sgu_b[0].T)
    y = _branch(y_attn.reshape(T, BRANCH_WIDTH), y_sgu, proj,
                w_branch[0, 0].astype(bf16), w_branch[0, 1].astype(bf16))
    wr_hi = w_router[0].astype(bf16)
    wr_lo = (w_router[0] - wr_hi.astype(f32)).astype(bf16)
    h, idx, gate, counts = _out_ln_router(y, x2, w_out[0].astype(bf16), ln1_g, ln1_b,
                                          jnp.concatenate([wr_hi, wr_lo], axis=1), b_router)
    pstart, padding, layout = _expert_layout(counts[0].astype(i32))
    dest = _rank(idx, pstart.astype(f32)[None, :]).reshape(T * TOP_K)
    xs = _dispatch(h, dest, *padding, layout[-1])
    act = _expert_gu(layout, xs, w_gu.reshape(N_EXPERTS, D_MODEL, 2 * D_MODEL),
                     b_gu.reshape(N_EXPERTS, 1, 2 * D_MODEL))
    ye = _expert_down(layout, act, w_down.reshape(N_EXPERTS, D_MODEL, D_MODEL),
                      b_down.reshape(N_EXPERTS, 1, D_MODEL))
    out = _combine(ye, dest, gate, h, ln2_g, ln2_b)
    return out.reshape(BATCH, SEQ, D_MODEL)
```

```python
import functools
import math

import numpy as np
import jax
import jax.numpy as jnp
from jax import lax
from jax.experimental import pallas as pl
from jax.experimental.pallas import tpu as pltpu

D_MODEL = 2048
BATCH = 8
SEQ = 2048
T = BATCH * SEQ
BRANCH_WIDTH = D_MODEL // 2
HEAD_DIM = 128
ATTN_GROUPS = 3
WINDOWS = (128, 512, 2048)
DILATIONS = (1, 4, 16)
ATTN_HEADS = BRANCH_WIDTH // HEAD_DIM
QKV_WIDTH = ATTN_GROUPS * ATTN_HEADS * HEAD_DIM
ATTN_BLOCK = 128
NUM_BUCKETS = 32
MAX_DISTANCE = 2048
SGU_CHUNK = 128
SGU_GROUPS = BRANCH_WIDTH // SGU_CHUNK
IN_COLS = 3 * QKV_WIDTH + 2 * BRANCH_WIDTH + 2 * D_MODEL
N_EXPERTS = 32
TOP_K = 4
SWIGLU_ALPHA = 1.702
SWIGLU_LIMIT = 7.0
LN_EPS = 1e-5
ALPHA = 2.0 ** 0.25
LOG2E = math.log2(math.e)
ATTN_CHUNK = 16

COL_U = 3 * QKV_WIDTH // 1024
COL_V = COL_U + 1
COL_GA = COL_U + 2
COL_GS = COL_GA + 2

BM = 512
NB = (T * TOP_K + N_EXPERTS * (BM - 1) + BM - 1) // BM
CAP = NB * BM

VMEM_LIMIT = 56 * 1024 * 1024
VMEM_LIMIT_MIX = 61 * 1024 * 1024

f32 = jnp.float32
bf16 = jnp.bfloat16
i32 = jnp.int32


def _layer_norm(x, g, b):
    mu = jnp.mean(x, axis=-1, keepdims=True)
    xc = x - mu
    var = jnp.mean(xc * xc, axis=-1, keepdims=True)
    return xc * lax.rsqrt(var + LN_EPS) * g + b


IP_TM = 1024
IP_TN = 1024


IP_CHUNK = 256


def _inproj_kernel(x_ref, w_ref, o_ref, xb_sc):
    j = pl.program_id(1)

    @pl.when(j == 0)
    def _():
        xb_sc[...] = x_ref[...].astype(bf16)

    def project(epilogue):
        for c in range(0, IP_TN, IP_CHUNK):
            acc = jnp.dot(xb_sc[...], w_ref[:, c:c + IP_CHUNK], preferred_element_type=f32)
            o_ref[:, c:c + IP_CHUNK] = epilogue(acc)

    @pl.when(j < COL_U)
    def _():
        project(lambda a: a)

    @pl.when((j >= COL_U) & (j < COL_GA))
    def _():
        project(jax.nn.gelu)

    @pl.when(j >= COL_GA)
    def _():
        project(jax.nn.sigmoid)


def _inproj(x2, wb):
    return pl.pallas_call(
        _inproj_kernel,
        grid=(T // IP_TM, IN_COLS // IP_TN),
        in_specs=[pl.BlockSpec((IP_TM, D_MODEL), lambda i, j: (i, 0)),
                  pl.BlockSpec((D_MODEL, IP_TN), lambda i, j: (0, j))],
        out_specs=pl.BlockSpec((IP_TM, IP_TN), lambda i, j: (i, j)),
        out_shape=jax.ShapeDtypeStruct((T, IN_COLS), f32),
        scratch_shapes=[pltpu.VMEM((IP_TM, D_MODEL), bf16)],
        compiler_params=pltpu.CompilerParams(
            dimension_semantics=("arbitrary", "arbitrary"), vmem_limit_bytes=VMEM_LIMIT),
        name="inproj",
    )(x2, wb)


def _attn_kernel(q0, q1, q2, k0, k1, k2, v0, v1, v2, bias_ref, o_ref,
                 m0, m1, m2, l0, l1, l2, a0, a1, a2, out_sc):
    qs, ks, vs = (q0, q1, q2), (k0, k1, k2), (v0, v1, v2)
    ms, ls, accs = (m0, m1, m2), (l0, l1, l2), (a0, a1, a2)
    scale = HEAD_DIM ** -0.5 * LOG2E
    QB = ATTN_BLOCK

    def rows(start, n, r):
        return pl.ds(start, n, stride=r) if r > 1 else pl.ds(start, n)

    def tiles(g, specs):
        r = DILATIONS[g]
        nk = specs[0][2]
        bias = bias_ref[g, :, 2 * QB - nk:]
        s, v = [], []
        for q_start, k_start, _ in specs:
            q = qs[g][rows(q_start, QB, r), :].astype(bf16)
            k = ks[g][rows(k_start, nk, r), :].astype(bf16)
            v.append(vs[g][rows(k_start, nk, r), :].astype(bf16))
            s.append(lax.dot_general(q, k, (((1,), (1,)), ((), ())), preferred_element_type=f32))
        s = jnp.stack(s) * scale + bias[None]
        m = jnp.max(s, axis=-1, keepdims=True)
        p = jnp.exp2(s - m)
        l = jnp.sum(p, axis=-1, keepdims=True)
        p = p.astype(bf16)
        out = []
        for t in range(len(specs)):
            pv = jnp.dot(p[t], v[t], preferred_element_type=f32)
            out.append((jnp.broadcast_to(m[t], pv.shape), jnp.broadcast_to(l[t], pv.shape), pv))
        return out

    for g in range(ATTN_GROUPS):
        r = DILATIONS[g]
        sub_len = SEQ // r
        first, rest = [], []
        for c in range(r):
            for n in range(sub_len // QB):
                q_start = c + n * QB * r
                dst = pl.ds(c * sub_len + n * QB, QB)
                if n == 0:
                    first.append(((q_start, q_start, QB), dst))
                else:
                    rest.append(((q_start, q_start - QB * r, 2 * QB), dst))
        for todo in (first, rest):
            for at in range(0, len(todo), ATTN_CHUNK):
                chunk = todo[at:at + ATTN_CHUNK]
                for (_, dst), (m, l, pv) in zip(chunk, tiles(g, [spec for spec, _ in chunk])):
                    ms[g][dst, :] = m
                    ls[g][dst, :] = l
                    accs[g][dst, :] = pv

    r_max = DILATIONS[-1]
    for c in range(r_max):
        sel = []
        for g in range(ATTN_GROUPS):
            r = DILATIONS[g]
            sel.append(rows((c % r) * (SEQ // r) + c // r, SEQ // r_max, r_max // r))
        m_g = [ms[g][sel[g], :] for g in range(ATTN_GROUPS)]
        m_all = jnp.maximum(jnp.maximum(m_g[0], m_g[1]), m_g[2])
        w_g = [jnp.exp2(m - m_all) for m in m_g]
        den = sum(w * ls[g][sel[g], :] for g, w in enumerate(w_g))
        num = sum(w * accs[g][sel[g], :] for g, w in enumerate(w_g))
        out_sc[rows(c, SEQ // r_max, r_max), :] = num / den
    o_ref[...] = out_sc[...].astype(o_ref.dtype)


def _attention(proj3, bias):
    hb = HEAD_DIM
    def col(kind, g):
        off = (kind * ATTN_GROUPS + g) * ATTN_HEADS
        return pl.BlockSpec((None, SEQ, hb), lambda b, h, off=off: (b, 0, off + h))
    in_specs = [col(kind, g) for kind in range(3) for g in range(ATTN_GROUPS)]
    in_specs.append(pl.BlockSpec((ATTN_GROUPS, None, ATTN_BLOCK, 2 * ATTN_BLOCK),
                                 lambda b, h: (0, h, 0, 0)))
    return pl.pallas_call(
        _attn_kernel,
        grid=(BATCH, ATTN_HEADS),
        in_specs=in_specs,
        out_specs=pl.BlockSpec((None, SEQ, hb), lambda b, h: (b, 0, h)),
        out_shape=jax.ShapeDtypeStruct((BATCH, SEQ, BRANCH_WIDTH), bf16),
        scratch_shapes=[pltpu.VMEM((SEQ, hb), f32)] * (3 * ATTN_GROUPS + 1),
        compiler_params=pltpu.CompilerParams(
            dimension_semantics=("arbitrary", "arbitrary"), vmem_limit_bytes=VMEM_LIMIT),
        name="attention",
    )(*([proj3] * 9), bias)


def _t5_bucket(dist):
    exact = NUM_BUCKETS // 2
    d = jnp.maximum(dist, 1).astype(f32)
    large = exact + (jnp.log(d / exact) / math.log(MAX_DISTANCE / exact)
                     * (NUM_BUCKETS - exact)).astype(i32)
    large = jnp.minimum(large, NUM_BUCKETS - 1)
    return jnp.where(dist < exact, dist, large)


def _attention_bias(rel_bias):
    qi = np.arange(ATTN_BLOCK, dtype=np.int32)[:, None]
    kj = np.arange(2 * ATTN_BLOCK, dtype=np.int32)[None, :]
    sub = qi + ATTN_BLOCK - kj
    table = rel_bias.reshape(NUM_BUCKETS, ATTN_GROUPS, ATTN_HEADS)
    out = []
    for g in range(ATTN_GROUPS):
        span = WINDOWS[g] // DILATIONS[g]
        band = (sub >= 0) & (sub <= span)
        bucket = _t5_bucket(jnp.asarray(DILATIONS[g] * np.clip(sub, 0, span), i32))
        onehot = jax.nn.one_hot(bucket.reshape(-1), NUM_BUCKETS, dtype=f32)
        b = jnp.dot(onehot, table[:, g], precision=lax.Precision.HIGHEST)
        b = jnp.where(band[:, :, None], b.reshape(ATTN_BLOCK, 2 * ATTN_BLOCK, ATTN_HEADS), -1e30)
        out.append(b.transpose(2, 0, 1))
    return jnp.stack(out, axis=0).astype(f32) * LOG2E


SGU_TM = 512


def _sgu_kernel(u_ref, v_ref, g_ref, b_ref, w_ref, bst_ref, o_ref):
    vn = _layer_norm(v_ref[...], g_ref[...], b_ref[...]).astype(bf16)
    ri = lax.broadcasted_iota(i32, (SGU_CHUNK, SGU_CHUNK), 0)
    ci = lax.broadcasted_iota(i32, (SGU_CHUNK, SGU_CHUNK), 1)
    causal = ci <= ri
    for g in range(SGU_GROUPS):
        wg = jnp.where(causal, w_ref[g], 0.0).astype(bf16)
        bias = bst_ref[:, g:g + 1]
        cs = slice(g * SGU_CHUNK, (g + 1) * SGU_CHUNK)
        for c in range(SGU_TM // SGU_CHUNK):
            rs = slice(c * SGU_CHUNK, (c + 1) * SGU_CHUNK)
            mixed = jnp.dot(wg, vn[rs, cs], preferred_element_type=f32) + bias
            o_ref[rs, cs] = (u_ref[rs, cs] * mixed).astype(o_ref.dtype)


def _sgu(proj, ln_g, ln_b, w_s, b_st):
    return pl.pallas_call(
        _sgu_kernel,
        grid=(T // SGU_TM,),
        in_specs=[pl.BlockSpec((SGU_TM, BRANCH_WIDTH), lambda i: (i, COL_U)),
                  pl.BlockSpec((SGU_TM, BRANCH_WIDTH), lambda i: (i, COL_V)),
                  pl.BlockSpec((1, BRANCH_WIDTH), lambda i: (0, 0)),
                  pl.BlockSpec((1, BRANCH_WIDTH), lambda i: (0, 0)),
                  pl.BlockSpec((SGU_GROUPS, SGU_CHUNK, SGU_CHUNK), lambda i: (0, 0, 0)),
                  pl.BlockSpec((SGU_CHUNK, SGU_GROUPS), lambda i: (0, 0))],
        out_specs=pl.BlockSpec((SGU_TM, BRANCH_WIDTH), lambda i: (i, 0)),
        out_shape=jax.ShapeDtypeStruct((T, BRANCH_WIDTH), bf16),
        compiler_params=pltpu.CompilerParams(
            dimension_semantics=("arbitrary",), vmem_limit_bytes=VMEM_LIMIT),
        name="sgu",
    )(proj, proj, ln_g, ln_b, w_s, b_st)


OL_TM = 512
GATE_TN = 1024


def _out_ln_router_kernel(ya_ref, ys_ref, ga0_ref, ga1_ref, gs0_ref, gs1_ref, x_ref,
                          wa_ref, ws_ref, wo_ref, g_ref, b_ref, wr_ref, br_ref,
                          h_ref, idx_ref, gate_ref, cnt_ref):
    i = pl.program_id(0)
    a = jnp.dot(ya_ref[...], wa_ref[...], preferred_element_type=f32)
    s = jnp.dot(ys_ref[...], ws_ref[...], preferred_element_type=f32)
    y = jnp.concatenate(
        [ga0_ref[...] * a[:, :GATE_TN] + gs0_ref[...] * s[:, :GATE_TN],
         ga1_ref[...] * a[:, GATE_TN:] + gs1_ref[...] * s[:, GATE_TN:]], axis=1).astype(bf16)
    mixed = jnp.dot(y, wo_ref[...], preferred_element_type=f32)
    h = _layer_norm(ALPHA * x_ref[...] + mixed, g_ref[...], b_ref[...])
    h_ref[...] = h
    h_hi = h.astype(bf16)
    h_lo = (h - h_hi.astype(f32)).astype(bf16)
    t = jnp.dot(h_hi, wr_ref[...], preferred_element_type=f32)
    u = jnp.dot(h_lo, wr_ref[:, :N_EXPERTS], preferred_element_type=f32)
    logits = t[:, :N_EXPERTS] + (t[:, N_EXPERTS:] + u) + br_ref[...]
    lane = lax.broadcasted_iota(i32, (OL_TM, N_EXPERTS), 1).astype(f32)
    lane_k = lax.broadcasted_iota(i32, (OL_TM, TOP_K), 1)
    idx_out = jnp.zeros((OL_TM, TOP_K), f32)
    val_out = jnp.zeros((OL_TM, TOP_K), f32)
    picked = jnp.zeros((OL_TM, N_EXPERTS), f32)
    work = logits
    for k in range(TOP_K):
        m = jnp.max(work, axis=-1, keepdims=True)
        idx = jnp.min(jnp.where(work == m, lane, float(N_EXPERTS)), axis=-1, keepdims=True)
        sel = lane == idx
        idx_out = jnp.where(lane_k == k, idx, idx_out)
        val_out = jnp.where(lane_k == k, m, val_out)
        picked = picked + sel.astype(f32)
        work = jnp.where(sel, -jnp.inf, work)
    e = jnp.exp(val_out - jnp.max(val_out, axis=-1, keepdims=True))
    gate_ref[...] = e / jnp.sum(e, axis=-1, keepdims=True)
    idx_ref[...] = idx_out.astype(i32)

    @pl.when(i == 0)
    def _():
        cnt_ref[...] = jnp.zeros_like(cnt_ref)

    cnt_ref[...] += jnp.sum(picked, axis=0, keepdims=True)


def _out_ln_router(ya, ys, proj, x2, wa, ws, wo, ln_g, ln_b, w_router, b_router):
    def resident(shape):
        return pl.BlockSpec(shape, lambda i: (0,) * len(shape), pipeline_mode=pl.Buffered(1))

    def gate_cols(block):
        return pl.BlockSpec((OL_TM, GATE_TN), lambda i, block=block: (i, block))

    return pl.pallas_call(
        _out_ln_router_kernel,
        grid=(T // OL_TM,),
        in_specs=[pl.BlockSpec((OL_TM, BRANCH_WIDTH), lambda i: (i, 0)),
                  pl.BlockSpec((OL_TM, BRANCH_WIDTH), lambda i: (i, 0)),
                  gate_cols(COL_GA), gate_cols(COL_GA + 1),
                  gate_cols(COL_GS), gate_cols(COL_GS + 1),
                  pl.BlockSpec((OL_TM, D_MODEL), lambda i: (i, 0)),
                  resident((BRANCH_WIDTH, D_MODEL)),
                  resident((BRANCH_WIDTH, D_MODEL)),
                  resident((D_MODEL, D_MODEL)),
                  resident((1, D_MODEL)),
                  resident((1, D_MODEL)),
                  resident((D_MODEL, 2 * N_EXPERTS)),
                  resident((1, N_EXPERTS))],
        out_specs=[pl.BlockSpec((OL_TM, D_MODEL), lambda i: (i, 0)),
                   pl.BlockSpec((OL_TM, TOP_K), lambda i: (i, 0)),
                   pl.BlockSpec((OL_TM, TOP_K), lambda i: (i, 0)),
                   pl.BlockSpec((1, N_EXPERTS), lambda i: (0, 0))],
        out_shape=[jax.ShapeDtypeStruct((T, D_MODEL), f32),
                   jax.ShapeDtypeStruct((T, TOP_K), i32),
                   jax.ShapeDtypeStruct((T, TOP_K), f32),
                   jax.ShapeDtypeStruct((1, N_EXPERTS), f32)],
        compiler_params=pltpu.CompilerParams(
            dimension_semantics=("arbitrary",), vmem_limit_bytes=VMEM_LIMIT_MIX),
        name="out_ln_router",
    )(ya, ys, proj, proj, proj, proj, x2, wa, ws, wo, ln_g, ln_b, w_router, b_router)


RK_TM = 512


def _rank_kernel(idx_ref, pstart_ref, dest_ref, carry_sc):
    i = pl.program_id(0)

    @pl.when(i == 0)
    def _():
        carry_sc[...] = jnp.zeros_like(carry_sc)

    idx = idx_ref[...]
    lane = lax.broadcasted_iota(i32, (RK_TM, N_EXPERTS), 1)
    picked = jnp.zeros((RK_TM, N_EXPERTS), f32)
    for k in range(TOP_K):
        picked = picked + (lane == idx[:, k:k + 1]).astype(f32)
    ri = lax.broadcasted_iota(i32, (RK_TM, RK_TM), 0)
    ci = lax.broadcasted_iota(i32, (RK_TM, RK_TM), 1)
    lower = (ci < ri).astype(bf16)
    before = jnp.dot(lower, picked.astype(bf16), preferred_element_type=f32)
    slot = before + carry_sc[...] + pstart_ref[...]
    lane_k = lax.broadcasted_iota(i32, (RK_TM, TOP_K), 1)
    dest = jnp.zeros((RK_TM, TOP_K), f32)
    for k in range(TOP_K):
        d = jnp.sum(jnp.where(lane == idx[:, k:k + 1], slot, 0.0), axis=-1, keepdims=True)
        dest = jnp.where(lane_k == k, d, dest)
    dest_ref[...] = dest.astype(i32)
    carry_sc[...] += jnp.sum(picked, axis=0, keepdims=True)


def _rank(idx, pstart):
    return pl.pallas_call(
        _rank_kernel,
        grid=(T // RK_TM,),
        in_specs=[pl.BlockSpec((RK_TM, TOP_K), lambda i: (i, 0)),
                  pl.BlockSpec((1, N_EXPERTS), lambda i: (0, 0))],
        out_specs=pl.BlockSpec((RK_TM, TOP_K), lambda i: (i, 0)),
        out_shape=jax.ShapeDtypeStruct((T, TOP_K), i32),
        scratch_shapes=[pltpu.VMEM((1, N_EXPERTS), f32)],
        compiler_params=pltpu.CompilerParams(dimension_semantics=("arbitrary",)),
        name="rank",
    )(idx, pstart)


DP_TM = 256
HALF = D_MODEL // 2


def _pack_bf16_pairs(x):
    bits = lax.bitcast_convert_type(x.astype(bf16).astype(f32), i32)
    return bits[:, HALF:] | lax.shift_right_logical(bits[:, :HALF], 16)


def _unpack_bf16_pairs(w):
    lo = lax.bitcast_convert_type(lax.shift_left(w, 16), f32)
    hi = lax.bitcast_convert_type(w & jnp.int32(-65536), f32)
    return jnp.concatenate([lo, hi], axis=1).astype(bf16)


ZROWS = 256
SUBLANES = 8


def _zero_fill(pstart_ref, plen_ref, meta_ref, xs_hbm, zbuf, zsem, wait):
    def go(cp):
        if wait:
            cp.wait()
        else:
            cp.start()

    def expert(e, carry):
        start, n = pstart_ref[e], plen_ref[e]
        head = n & (SUBLANES - 1)
        for t in range(SUBLANES - 1):
            @pl.when(t < head)
            def _():
                go(pltpu.make_async_copy(zbuf.at[pl.ds(0, 1)], xs_hbm.at[pl.ds(start + t, 1)],
                                         zsem.at[0]))
        b = SUBLANES
        while b <= ZROWS:
            @pl.when((n & b) != 0)
            def _(b=b):
                off = pl.multiple_of(start + head + (n & ~(2 * b - 1)), SUBLANES)
                go(pltpu.make_async_copy(zbuf.at[pl.ds(0, b)], xs_hbm.at[pl.ds(off, b)],
                                         zsem.at[0]))
            b *= 2
        return carry

    lax.fori_loop(0, N_EXPERTS, expert, 0)

    def unowned(blk, carry):
        for part in range(BM // ZROWS):
            off = pl.multiple_of(blk * BM + part * ZROWS, ZROWS)
            go(pltpu.make_async_copy(zbuf, xs_hbm.at[pl.ds(off, ZROWS)], zsem.at[0]))
        return carry

    lax.fori_loop(meta_ref[0], NB, unowned, 0)


def _dispatch_kernel(pstart_ref, plen_ref, meta_ref, dest_ref, h_ref, xs_hbm,
                     pack_sc, zbuf, sem, zsem):
    first = pl.program_id(0) == 0

    @pl.when(first)
    def _():
        zbuf[...] = jnp.zeros_like(zbuf)
        _zero_fill(pstart_ref, plen_ref, meta_ref, xs_hbm, zbuf, zsem, wait=False)

    pack_sc[...] = _pack_bf16_pairs(h_ref[...])

    def issue(t, carry):
        for k in range(TOP_K):
            pltpu.make_async_copy(pack_sc.at[pl.ds(t, 1)],
                                  xs_hbm.at[pl.ds(dest_ref[t * TOP_K + k], 1)],
                                  sem.at[0]).start(priority=k % 2)
        return carry

    lax.fori_loop(0, DP_TM, issue, 0)

    @pl.when(first)
    def _():
        _zero_fill(pstart_ref, plen_ref, meta_ref, xs_hbm, zbuf, zsem, wait=True)

    for _ in range(TOP_K):
        pltpu.make_async_copy(pack_sc, xs_hbm.at[pl.ds(0, DP_TM)], sem.at[0]).wait()


def _dispatch(h, dest_flat, pad_start, pad_len, meta):
    grid_spec = pltpu.PrefetchScalarGridSpec(
        num_scalar_prefetch=3,
        grid=(T // DP_TM,),
        in_specs=[pl.BlockSpec((DP_TM * TOP_K,), lambda i, *_: (i,), memory_space=pltpu.SMEM),
                  pl.BlockSpec((DP_TM, D_MODEL), lambda i, *_: (i, 0))],
        out_specs=pl.BlockSpec(memory_space=pl.ANY),
        scratch_shapes=[pltpu.VMEM((DP_TM, HALF), i32), pltpu.VMEM((ZROWS, HALF), i32),
                        pltpu.SemaphoreType.DMA((1,)), pltpu.SemaphoreType.DMA((1,))],
    )
    return pl.pallas_call(
        _dispatch_kernel,
        grid_spec=grid_spec,
        out_shape=jax.ShapeDtypeStruct((CAP, HALF), i32),
        compiler_params=pltpu.CompilerParams(dimension_semantics=("arbitrary",)),
        name="dispatch",
    )(pad_start, pad_len, meta, dest_flat, h)


GU_TN = 1024
DN_TN = 2048


def _stream_expert_weights(bs_ref, se_ref, meta_ref, w_hbm, col_offsets, tn, wbuf, sem, caches):
    j, i, nj = pl.program_id(0), pl.program_id(1), pl.num_programs(0)
    n_seg = jnp.maximum(meta_ref[1], 1)
    seg = bs_ref[i]
    first_block = (i == 0) | (seg != bs_ref[jnp.maximum(i - 1, 0)])

    def copies(s, jj, slot):
        expert = se_ref[s]
        out = []
        for c, off in enumerate(col_offsets):
            start = off + jj * tn
            if not isinstance(start, int):
                start = pl.multiple_of(start, tn)
            out.append(pltpu.make_async_copy(w_hbm.at[expert, :, pl.ds(start, tn)],
                                             wbuf.at[slot, c], sem.at[slot]))
        return out

    @pl.when(first_block)
    def _():
        seq = j * n_seg + seg
        slot = seq % 2

        @pl.when(seq == 0)
        def _():
            for cp in copies(0, 0, 0):
                cp.start()

        more = seg + 1 < n_seg

        @pl.when(more | (j + 1 < nj))
        def _():
            for cp in copies(jnp.where(more, seg + 1, 0), jnp.where(more, j, j + 1), 1 - slot):
                cp.start()

        for cp in copies(seg, j, slot):
            cp.wait()
        for c, cache in enumerate(caches):
            cache[...] = wbuf[slot, c].astype(bf16)


def _expert_gu_kernel(be_ref, bs_ref, se_ref, meta_ref, x_ref, w_hbm, bg_ref, bl_ref, o_ref,
                      wbuf, wg_sc, wl_sc, sem):
    del be_ref
    i = pl.program_id(1)
    last = jnp.maximum(meta_ref[0], 1) - 1

    @pl.when(i <= last)
    def _():
        _stream_expert_weights(bs_ref, se_ref, meta_ref, w_hbm, (0, D_MODEL), GU_TN,
                               wbuf, sem, (wg_sc, wl_sc))
        x = _unpack_bf16_pairs(x_ref[...])
        glu = jnp.dot(x, wg_sc[...], preferred_element_type=f32) + bg_ref[...]
        lin = jnp.dot(x, wl_sc[...], preferred_element_type=f32) + bl_ref[...]
        glu = jnp.minimum(glu, SWIGLU_LIMIT)
        lin = jnp.clip(lin, -SWIGLU_LIMIT, SWIGLU_LIMIT)
        o_ref[...] = (glu * jax.nn.sigmoid(SWIGLU_ALPHA * glu) * (lin + 1.0)).astype(o_ref.dtype)

    @pl.when(i > last)
    def _():
        o_ref[...] = jnp.zeros_like(o_ref)


def _used_block(j, i, be, bs, se, meta):
    return jnp.minimum(i, jnp.maximum(meta[0], 1) - 1)


def _expert_gu(layout, xs, w_gu, b_gu):
    nj = D_MODEL // GU_TN
    blk = _used_block
    grid_spec = pltpu.PrefetchScalarGridSpec(
        num_scalar_prefetch=4,
        grid=(nj, NB),
        in_specs=[
            pl.BlockSpec((BM, HALF), lambda *a: (blk(*a), 0)),
            pl.BlockSpec(memory_space=pl.ANY),
            pl.BlockSpec((None, 1, GU_TN), lambda *a: (a[2][blk(*a)], 0, a[0])),
            pl.BlockSpec((None, 1, GU_TN), lambda *a: (a[2][blk(*a)], 0, nj + a[0])),
        ],
        out_specs=pl.BlockSpec((BM, GU_TN), lambda j, i, *_: (i, j)),
        scratch_shapes=[pltpu.VMEM((2, 2, D_MODEL, GU_TN), f32),
                        pltpu.VMEM((D_MODEL, GU_TN), bf16), pltpu.VMEM((D_MODEL, GU_TN), bf16),
                        pltpu.SemaphoreType.DMA((2,))],
    )
    return pl.pallas_call(
        _expert_gu_kernel,
        grid_spec=grid_spec,
        out_shape=jax.ShapeDtypeStruct((CAP, D_MODEL), bf16),
        compiler_params=pltpu.CompilerParams(
            dimension_semantics=("arbitrary", "arbitrary"), vmem_limit_bytes=VMEM_LIMIT),
        name="expert_gu",
    )(*layout, xs, w_gu, b_gu, b_gu)


def _expert_down_kernel(be_ref, bs_ref, se_ref, meta_ref, a_ref, w_hbm, b_ref, o_ref,
                        wbuf, w_sc, sem):
    del be_ref
    i = pl.program_id(1)
    last = jnp.maximum(meta_ref[0], 1) - 1

    @pl.when(i <= last)
    def _():
        _stream_expert_weights(bs_ref, se_ref, meta_ref, w_hbm, (0,), DN_TN, wbuf, sem, (w_sc,))
        o_ref[...] = jnp.dot(a_ref[...], w_sc[...], preferred_element_type=f32) + b_ref[...]

    @pl.when(i > last)
    def _():
        o_ref[...] = jnp.zeros_like(o_ref)


def _expert_down(layout, act, w_down, b_down):
    blk = _used_block
    grid_spec = pltpu.PrefetchScalarGridSpec(
        num_scalar_prefetch=4,
        grid=(D_MODEL // DN_TN, NB),
        in_specs=[
            pl.BlockSpec((BM, D_MODEL), lambda *a: (blk(*a), 0)),
            pl.BlockSpec(memory_space=pl.ANY),
            pl.BlockSpec((None, 1, DN_TN), lambda *a: (a[2][blk(*a)], 0, a[0])),
        ],
        out_specs=pl.BlockSpec((BM, DN_TN), lambda j, i, *_: (i, j)),
        scratch_shapes=[pltpu.VMEM((2, 1, D_MODEL, DN_TN), f32),
                        pltpu.VMEM((D_MODEL, DN_TN), bf16),
                        pltpu.SemaphoreType.DMA((2,))],
    )
    return pl.pallas_call(
        _expert_down_kernel,
        grid_spec=grid_spec,
        out_shape=jax.ShapeDtypeStruct((CAP, D_MODEL), f32),
        compiler_params=pltpu.CompilerParams(
            dimension_semantics=("arbitrary", "arbitrary"), vmem_limit_bytes=VMEM_LIMIT),
        name="expert_down",
    )(*layout, act, w_down, b_down)


CB_TM = 128
CB_SLABS = 4


def _combine_kernel(dcur_ref, dnext_ref, y_hbm, gate_ref, h_ref, g_ref, b_ref, o_ref,
                    buf_a, buf_b, sem):
    i, n = pl.program_id(0), pl.num_programs(0)

    def start_rows(d_ref, buf, s, t):
        for k in range(TOP_K):
            pltpu.make_async_copy(y_hbm.at[pl.ds(d_ref[t * TOP_K + k], 1)],
                                  buf.at[k, pl.ds(t, 1)], sem.at[s]).start(priority=k % 2)

    def wait_tile(buf, s):
        for k in range(TOP_K):
            pltpu.make_async_copy(y_hbm.at[pl.ds(0, CB_TM)], buf.at[k], sem.at[s]).wait()

    @pl.when(i == 0)
    def _():
        def prime(t, carry):
            start_rows(dcur_ref, buf_a, 0, t)
            return carry
        lax.fori_loop(0, CB_TM, prime, 0)

    def step(cur, s_cur, nxt, s_nxt):
        wait_tile(cur, s_cur)

        def slabs(gi, carry):
            for part in range(CB_SLABS):
                t0 = pl.multiple_of((gi * CB_SLABS + part) * SUBLANES, SUBLANES)
                for tt in range(SUBLANES):
                    start_rows(dnext_ref, nxt, s_nxt, t0 + tt)
                rows = pl.ds(t0, SUBLANES)
                gate = gate_ref[rows, :]
                ffn = gate[:, 0:1] * cur[0, rows, :]
                for k in range(1, TOP_K):
                    ffn = ffn + gate[:, k:k + 1] * cur[k, rows, :]
                o_ref[rows, :] = _layer_norm(ALPHA * h_ref[rows, :] + ffn, g_ref[...], b_ref[...])
            return carry

        lax.fori_loop(0, CB_TM // (SUBLANES * CB_SLABS), slabs, 0)

        @pl.when(i == n - 1)
        def _():
            wait_tile(nxt, s_nxt)

    @pl.when(i % 2 == 0)
    def _():
        step(buf_a, 0, buf_b, 1)

    @pl.when(i % 2 == 1)
    def _():
        step(buf_b, 1, buf_a, 0)


def _combine(y, dest_flat, gate, h, ln_g, ln_b):
    n_tiles = T // CB_TM
    return pl.pallas_call(
        _combine_kernel,
        grid=(n_tiles,),
        in_specs=[pl.BlockSpec((CB_TM * TOP_K,), lambda i: (i,), memory_space=pltpu.SMEM),
                  pl.BlockSpec((CB_TM * TOP_K,), lambda i: (jnp.minimum(i + 1, n_tiles - 1),),
                               memory_space=pltpu.SMEM),
                  pl.BlockSpec(memory_space=pl.ANY),
                  pl.BlockSpec((CB_TM, TOP_K), lambda i: (i, 0)),
                  pl.BlockSpec((CB_TM, D_MODEL), lambda i: (i, 0)),
                  pl.BlockSpec((1, D_MODEL), lambda i: (0, 0)),
                  pl.BlockSpec((1, D_MODEL), lambda i: (0, 0))],
        out_specs=pl.BlockSpec((CB_TM, D_MODEL), lambda i: (i, 0)),
        out_shape=jax.ShapeDtypeStruct((T, D_MODEL), f32),
        scratch_shapes=[pltpu.VMEM((TOP_K, CB_TM, D_MODEL), f32),
                        pltpu.VMEM((TOP_K, CB_TM, D_MODEL), f32),
                        pltpu.SemaphoreType.DMA((2,))],
        compiler_params=pltpu.CompilerParams(dimension_semantics=("arbitrary",),
                                             vmem_limit_bytes=VMEM_LIMIT),
        name="combine",
    )(dest_flat, dest_flat, y, gate, h, ln_g, ln_b)


def _expert_layout(counts):
    nblk = (counts + BM - 1) // BM
    bend = jnp.cumsum(nblk)
    pstart = (bend - nblk) * BM
    before = bend[None, :] <= jnp.arange(NB, dtype=i32)[:, None]
    block_e = jnp.minimum(jnp.sum(before.astype(i32), axis=1), N_EXPERTS - 1)
    has = nblk > 0
    block_seg = jnp.sum((before & has[None, :]).astype(i32), axis=1)
    seg_of_e = jnp.cumsum(has.astype(i32)) - 1
    ids = jnp.arange(N_EXPERTS, dtype=i32)
    seg_expert = jnp.sum(jnp.where(has[None, :] & (seg_of_e[None, :] == ids[:, None]),
                                   ids[None, :], 0), axis=1)
    meta = jnp.stack([bend[-1], jnp.sum(has.astype(i32))]).astype(i32)
    padding = ((pstart + counts).astype(i32), (nblk * BM - counts).astype(i32))
    return pstart, padding, (block_e, block_seg.astype(i32), seg_expert.astype(i32), meta)


def kernel(x, w_in, rel_bias, sgu_ln_g, sgu_ln_b, sgu_w, sgu_b, w_branch, w_out, ln1_g, ln1_b,
           w_router, b_router, w_gu, b_gu, w_down, b_down, ln2_g, ln2_b):
    x2 = x.reshape(T, D_MODEL)
    proj = _inproj(x2, w_in[0].astype(bf16))
    y_attn = _attention(proj.reshape(BATCH, SEQ, IN_COLS), _attention_bias(rel_bias))
    y_sgu = _sgu(proj, sgu_ln_g, sgu_ln_b, sgu_w[0], sgu_b[0].T)
    wr_hi = w_router[0].astype(bf16)
    wr_lo = (w_router[0] - wr_hi.astype(f32)).astype(bf16)
    h, idx, gate, counts = _out_ln_router(
        y_attn.reshape(T, BRANCH_WIDTH), y_sgu, proj, x2,
        w_branch[0, 0].astype(bf16), w_branch[0, 1].astype(bf16), w_out[0].astype(bf16),
        ln1_g, ln1_b, jnp.concatenate([wr_hi, wr_lo], axis=1), b_router)
    pstart, padding, layout = _expert_layout(counts[0].astype(i32))
    dest = _rank(idx, pstart.astype(f32)[None, :]).reshape(T * TOP_K)
    xs = _dispatch(h, dest, *padding, layout[-1])
    act = _expert_gu(layout, xs, w_gu.reshape(N_EXPERTS, D_MODEL, 2 * D_MODEL),
                     b_gu.reshape(N_EXPERTS, 1, 2 * D_MODEL))
    ye = _expert_down(layout, act, w_down.reshape(N_EXPERTS, D_MODEL, D_MODEL),
                      b_down.reshape(N_EXPERTS, 1, D_MODEL))
    out = _combine(ye, dest, gate, h, ln2_g, ln2_b)
    return out.reshape(BATCH, SEQ, D_MODEL)
```

```python
import functools
import math

import numpy as np
import jax
import jax.numpy as jnp
from jax import lax
from jax.experimental import pallas as pl
from jax.experimental.pallas import tpu as pltpu

D_MODEL = 2048
BATCH = 8
SEQ = 2048
T = BATCH * SEQ
BRANCH_WIDTH = D_MODEL // 2
HEAD_DIM = 128
ATTN_GROUPS = 3
WINDOWS = (128, 512, 2048)
DILATIONS = (1, 4, 16)
ATTN_HEADS = BRANCH_WIDTH // HEAD_DIM
QKV_WIDTH = ATTN_GROUPS * ATTN_HEADS * HEAD_DIM
ATTN_BLOCK = 128
NUM_BUCKETS = 32
MAX_DISTANCE = 2048
SGU_CHUNK = 128
SGU_GROUPS = BRANCH_WIDTH // SGU_CHUNK
IN_COLS = 3 * QKV_WIDTH + 2 * BRANCH_WIDTH + 2 * D_MODEL
N_EXPERTS = 32
TOP_K = 4
SWIGLU_ALPHA = 1.702
SWIGLU_LIMIT = 7.0
LN_EPS = 1e-5
ALPHA = 2.0 ** 0.25
LOG2E = math.log2(math.e)
ATTN_CHUNK = 16

COL_U = 3 * QKV_WIDTH // 1024
COL_V = COL_U + 1
COL_GA = COL_U + 2
COL_GS = COL_GA + 2

BM = 512
NB = (T * TOP_K + N_EXPERTS * (BM - 1) + BM - 1) // BM
CAP = NB * BM

VMEM_LIMIT = 56 * 1024 * 1024
VMEM_LIMIT_MIX = 61 * 1024 * 1024

f32 = jnp.float32
bf16 = jnp.bfloat16
i32 = jnp.int32


def _layer_norm(x, g, b):
    mu = jnp.mean(x, axis=-1, keepdims=True)
    xc = x - mu
    var = jnp.mean(xc * xc, axis=-1, keepdims=True)
    return xc * lax.rsqrt(var + LN_EPS) * g + b


IP_TM = 1024
IP_TN = 1024


IP_CHUNK = 256


def _inproj_kernel(x_ref, w_ref, o_ref, xb_sc):
    j = pl.program_id(1)

    @pl.when(j == 0)
    def _():
        xb_sc[...] = x_ref[...].astype(bf16)

    def project(epilogue):
        for c in range(0, IP_TN, IP_CHUNK):
            acc = jnp.dot(xb_sc[...], w_ref[:, c:c + IP_CHUNK], preferred_element_type=f32)
            o_ref[:, c:c + IP_CHUNK] = epilogue(acc)

    @pl.when(j < COL_U)
    def _():
        project(lambda a: a)

    @pl.when((j >= COL_U) & (j < COL_GA))
    def _():
        project(jax.nn.gelu)

    @pl.when(j >= COL_GA)
    def _():
        project(jax.nn.sigmoid)


def _inproj(x2, wb):
    return pl.pallas_call(
        _inproj_kernel,
        grid=(T // IP_TM, IN_COLS // IP_TN),
        in_specs=[pl.BlockSpec((IP_TM, D_MODEL), lambda i, j: (i, 0)),
                  pl.BlockSpec((D_MODEL, IP_TN), lambda i, j: (0, j))],
        out_specs=pl.BlockSpec((IP_TM, IP_TN), lambda i, j: (i, j)),
        out_shape=jax.ShapeDtypeStruct((T, IN_COLS), f32),
        scratch_shapes=[pltpu.VMEM((IP_TM, D_MODEL), bf16)],
        compiler_params=pltpu.CompilerParams(
            dimension_semantics=("arbitrary", "arbitrary"), vmem_limit_bytes=VMEM_LIMIT),
        name="inproj",
    )(x2, wb)


def _attn_kernel(q0, q1, q2, k0, k1, k2, v0, v1, v2, bias_ref, o_ref,
                 m0, m1, m2, l0, l1, l2, a0, a1, a2, out_sc):
    qs, ks, vs = (q0, q1, q2), (k0, k1, k2), (v0, v1, v2)
    ms, ls, accs = (m0, m1, m2), (l0, l1, l2), (a0, a1, a2)
    scale = HEAD_DIM ** -0.5 * LOG2E
    QB = ATTN_BLOCK

    def rows(start, n, r):
        return pl.ds(start, n, stride=r) if r > 1 else pl.ds(start, n)

    def tiles(g, specs):
        r = DILATIONS[g]
        nk = specs[0][2]
        bias = bias_ref[g, :, 2 * QB - nk:]
        s, v = [], []
        for q_start, k_start, _ in specs:
            q = qs[g][rows(q_start, QB, r), :].astype(bf16)
            k = ks[g][rows(k_start, nk, r), :].astype(bf16)
            v.append(vs[g][rows(k_start, nk, r), :].astype(bf16))
            s.append(lax.dot_general(q, k, (((1,), (1,)), ((), ())), preferred_element_type=f32))
        s = jnp.stack(s) * scale + bias[None]
        m = jnp.max(s, axis=-1, keepdims=True)
        p = jnp.exp2(s - m)
        l = jnp.sum(p, axis=-1, keepdims=True)
        p = p.astype(bf16)
        out = []
        for t in range(len(specs)):
            pv = jnp.dot(p[t], v[t], preferred_element_type=f32)
            out.append((jnp.broadcast_to(m[t], pv.shape), jnp.broadcast_to(l[t], pv.shape), pv))
        return out

    for g in range(ATTN_GROUPS):
        r = DILATIONS[g]
        sub_len = SEQ // r
        first, rest = [], []
        for c in range(r):
            for n in range(sub_len // QB):
                q_start = c + n * QB * r
                dst = pl.ds(c * sub_len + n * QB, QB)
                if n == 0:
                    first.append(((q_start, q_start, QB), dst))
                else:
                    rest.append(((q_start, q_start - QB * r, 2 * QB), dst))
        for todo in (first, rest):
            for at in range(0, len(todo), ATTN_CHUNK):
                chunk = todo[at:at + ATTN_CHUNK]
                for (_, dst), (m, l, pv) in zip(chunk, tiles(g, [spec for spec, _ in chunk])):
                    ms[g][dst, :] = m
                    ls[g][dst, :] = l
                    accs[g][dst, :] = pv

    r_max = DILATIONS[-1]
    for c in range(r_max):
        sel = []
        for g in range(ATTN_GROUPS):
            r = DILATIONS[g]
            sel.append(rows((c % r) * (SEQ // r) + c // r, SEQ // r_max, r_max // r))
        m_g = [ms[g][sel[g], :] for g in range(ATTN_GROUPS)]
        m_all = jnp.maximum(jnp.maximum(m_g[0], m_g[1]), m_g[2])
        w_g = [jnp.exp2(m - m_all) for m in m_g]
        den = sum(w * ls[g][sel[g], :] for g, w in enumerate(w_g))
        num = sum(w * accs[g][sel[g], :] for g, w in enumerate(w_g))
        out_sc[rows(c, SEQ // r_max, r_max), :] = num / den
    o_ref[...] = out_sc[...].astype(o_ref.dtype)


def _attention(proj3, bias):
    hb = HEAD_DIM
    def col(kind, g):
        off = (kind * ATTN_GROUPS + g) * ATTN_HEADS
        return pl.BlockSpec((None, SEQ, hb), lambda b, h, off=off: (b, 0, off + h))
    in_specs = [col(kind, g) for kind in range(3) for g in range(ATTN_GROUPS)]
    in_specs.append(pl.BlockSpec((ATTN_GROUPS, None, ATTN_BLOCK, 2 * ATTN_BLOCK),
                                 lambda b, h: (0, h, 0, 0)))
    return pl.pallas_call(
        _attn_kernel,
        grid=(BATCH, ATTN_HEADS),
        in_specs=in_specs,
        out_specs=pl.BlockSpec((None, SEQ, hb), lambda b, h: (b, 0, h)),
        out_shape=jax.ShapeDtypeStruct((BATCH, SEQ, BRANCH_WIDTH), bf16),
        scratch_shapes=[pltpu.VMEM((SEQ, hb), f32)] * (3 * ATTN_GROUPS + 1),
        compiler_params=pltpu.CompilerParams(
            dimension_semantics=("arbitrary", "arbitrary"), vmem_limit_bytes=VMEM_LIMIT),
        name="attention",
    )(*([proj3] * 9), bias)


def _t5_bucket(dist):
    exact = NUM_BUCKETS // 2
    d = jnp.maximum(dist, 1).astype(f32)
    large = exact + (jnp.log(d / exact) / math.log(MAX_DISTANCE / exact)
                     * (NUM_BUCKETS - exact)).astype(i32)
    large = jnp.minimum(large, NUM_BUCKETS - 1)
    return jnp.where(dist < exact, dist, large)


def _attention_bias(rel_bias):
    qi = np.arange(ATTN_BLOCK, dtype=np.int32)[:, None]
    kj = np.arange(2 * ATTN_BLOCK, dtype=np.int32)[None, :]
    sub = qi + ATTN_BLOCK - kj
    table = rel_bias.reshape(NUM_BUCKETS, ATTN_GROUPS, ATTN_HEADS)
    out = []
    for g in range(ATTN_GROUPS):
        span = WINDOWS[g] // DILATIONS[g]
        band = (sub >= 0) & (sub <= span)
        bucket = _t5_bucket(jnp.asarray(DILATIONS[g] * np.clip(sub, 0, span), i32))
        onehot = jax.nn.one_hot(bucket.reshape(-1), NUM_BUCKETS, dtype=f32)
        b = jnp.dot(onehot, table[:, g], precision=lax.Precision.HIGHEST)
        b = jnp.where(band[:, :, None], b.reshape(ATTN_BLOCK, 2 * ATTN_BLOCK, ATTN_HEADS), -1e30)
        out.append(b.transpose(2, 0, 1))
    return jnp.stack(out, axis=0).astype(f32) * LOG2E


SGU_TM = 512


def _sgu_kernel(u_ref, v_ref, g_ref, b_ref, w_ref, bst_ref, o_ref):
    vn = _layer_norm(v_ref[...], g_ref[...], b_ref[...]).astype(bf16)
    ri = lax.broadcasted_iota(i32, (SGU_CHUNK, SGU_CHUNK), 0)
    ci = lax.broadcasted_iota(i32, (SGU_CHUNK, SGU_CHUNK), 1)
    causal = ci <= ri
    for g in range(SGU_GROUPS):
        wg = jnp.where(causal, w_ref[g], 0.0).astype(bf16)
        bias = bst_ref[:, g:g + 1]
        cs = slice(g * SGU_CHUNK, (g + 1) * SGU_CHUNK)
        for c in range(SGU_TM // SGU_CHUNK):
            rs = slice(c * SGU_CHUNK, (c + 1) * SGU_CHUNK)
            mixed = jnp.dot(wg, vn[rs, cs], preferred_element_type=f32) + bias
            o_ref[rs, cs] = (u_ref[rs, cs] * mixed).astype(o_ref.dtype)


def _sgu(proj, ln_g, ln_b, w_s, b_st):
    return pl.pallas_call(
        _sgu_kernel,
        grid=(T // SGU_TM,),
        in_specs=[pl.BlockSpec((SGU_TM, BRANCH_WIDTH), lambda i: (i, COL_U)),
                  pl.BlockSpec((SGU_TM, BRANCH_WIDTH), lambda i: (i, COL_V)),
                  pl.BlockSpec((1, BRANCH_WIDTH), lambda i: (0, 0)),
                  pl.BlockSpec((1, BRANCH_WIDTH), lambda i: (0, 0)),
                  pl.BlockSpec((SGU_GROUPS, SGU_CHUNK, SGU_CHUNK), lambda i: (0, 0, 0)),
                  pl.BlockSpec((SGU_CHUNK, SGU_GROUPS), lambda i: (0, 0))],
        out_specs=pl.BlockSpec((SGU_TM, BRANCH_WIDTH), lambda i: (i, 0)),
        out_shape=jax.ShapeDtypeStruct((T, BRANCH_WIDTH), bf16),
        compiler_params=pltpu.CompilerParams(
            dimension_semantics=("arbitrary",), vmem_limit_bytes=VMEM_LIMIT),
        name="sgu",
    )(proj, proj, ln_g, ln_b, w_s, b_st)


OL_TM = 512
GATE_TN = 1024


def _out_ln_router_kernel(ya_ref, ys_ref, ga0_ref, ga1_ref, gs0_ref, gs1_ref, x_ref,
                          wa_ref, ws_ref, wo_ref, g_ref, b_ref, wr_ref, br_ref,
                          h_ref, idx_ref, gate_ref, cnt_ref):
    i = pl.program_id(0)
    a = jnp.dot(ya_ref[...], wa_ref[...], preferred_element_type=f32)
    s = jnp.dot(ys_ref[...], ws_ref[...], preferred_element_type=f32)
    y = jnp.concatenate(
        [ga0_ref[...] * a[:, :GATE_TN] + gs0_ref[...] * s[:, :GATE_TN],
         ga1_ref[...] * a[:, GATE_TN:] + gs1_ref[...] * s[:, GATE_TN:]], axis=1).astype(bf16)
    mixed = jnp.dot(y, wo_ref[...], preferred_element_type=f32)
    h = _layer_norm(ALPHA * x_ref[...] + mixed, g_ref[...], b_ref[...])
    h_ref[...] = h
    h_hi = h.astype(bf16)
    h_lo = (h - h_hi.astype(f32)).astype(bf16)
    t = jnp.dot(h_hi, wr_ref[...], preferred_element_type=f32)
    u = jnp.dot(h_lo, wr_ref[:, :N_EXPERTS], preferred_element_type=f32)
    logits = t[:, :N_EXPERTS] + (t[:, N_EXPERTS:] + u) + br_ref[...]
    lane = lax.broadcasted_iota(i32, (OL_TM, N_EXPERTS), 1).astype(f32)
    lane_k = lax.broadcasted_iota(i32, (OL_TM, TOP_K), 1)
    idx_out = jnp.zeros((OL_TM, TOP_K), f32)
    val_out = jnp.zeros((OL_TM, TOP_K), f32)
    picked = jnp.zeros((OL_TM, N_EXPERTS), f32)
    work = logits
    for k in range(TOP_K):
        m = jnp.max(work, axis=-1, keepdims=True)
        idx = jnp.min(jnp.where(work == m, lane, float(N_EXPERTS)), axis=-1, keepdims=True)
        sel = lane == idx
        idx_out = jnp.where(lane_k == k, idx, idx_out)
        val_out = jnp.where(lane_k == k, m, val_out)
        picked = picked + sel.astype(f32)
        work = jnp.where(sel, -jnp.inf, work)
    e = jnp.exp(val_out - jnp.max(val_out, axis=-1, keepdims=True))
    gate_ref[...] = e / jnp.sum(e, axis=-1, keepdims=True)
    idx_ref[...] = idx_out.astype(i32)

    @pl.when(i == 0)
    def _():
        cnt_ref[...] = jnp.zeros_like(cnt_ref)

    cnt_ref[...] += jnp.sum(picked, axis=0, keepdims=True)


def _out_ln_router(ya, ys, proj, x2, wa, ws, wo, ln_g, ln_b, w_router, b_router):
    def resident(shape):
        return pl.BlockSpec(shape, lambda i: (0,) * len(shape), pipeline_mode=pl.Buffered(1))

    def gate_cols(block):
        return pl.BlockSpec((OL_TM, GATE_TN), lambda i, block=block: (i, block))

    return pl.pallas_call(
        _out_ln_router_kernel,
        grid=(T // OL_TM,),
        in_specs=[pl.BlockSpec((OL_TM, BRANCH_WIDTH), lambda i: (i, 0)),
                  pl.BlockSpec((OL_TM, BRANCH_WIDTH), lambda i: (i, 0)),
                  gate_cols(COL_GA), gate_cols(COL_GA + 1),
                  gate_cols(COL_GS), gate_cols(COL_GS + 1),
                  pl.BlockSpec((OL_TM, D_MODEL), lambda i: (i, 0)),
                  resident((BRANCH_WIDTH, D_MODEL)),
                  resident((BRANCH_WIDTH, D_MODEL)),
                  resident((D_MODEL, D_MODEL)),
                  resident((1, D_MODEL)),
                  resident((1, D_MODEL)),
                  resident((D_MODEL, 2 * N_EXPERTS)),
                  resident((1, N_EXPERTS))],
        out_specs=[pl.BlockSpec((OL_TM, D_MODEL), lambda i: (i, 0)),
                   pl.BlockSpec((OL_TM, TOP_K), lambda i: (i, 0)),
                   pl.BlockSpec((OL_TM, TOP_K), lambda i: (i, 0)),
                   pl.BlockSpec((1, N_EXPERTS), lambda i: (0, 0))],
        out_shape=[jax.ShapeDtypeStruct((T, D_MODEL), f32),
                   jax.ShapeDtypeStruct((T, TOP_K), i32),
                   jax.ShapeDtypeStruct((T, TOP_K), f32),
                   jax.ShapeDtypeStruct((1, N_EXPERTS), f32)],
        compiler_params=pltpu.CompilerParams(
            dimension_semantics=("arbitrary",), vmem_limit_bytes=VMEM_LIMIT_MIX),
        name="out_ln_router",
    )(ya, ys, proj, proj, proj, proj, x2, wa, ws, wo, ln_g, ln_b, w_router, b_router)


RK_TM = 512


def _rank_kernel(idx_ref, pstart_ref, dest_ref, carry_sc):
    i = pl.program_id(0)

    @pl.when(i == 0)
    def _():
        carry_sc[...] = jnp.zeros_like(carry_sc)

    idx = idx_ref[...]
    lane = lax.broadcasted_iota(i32, (RK_TM, N_EXPERTS), 1)
    picked = jnp.zeros((RK_TM, N_EXPERTS), f32)
    for k in range(TOP_K):
        picked = picked + (lane == idx[:, k:k + 1]).astype(f32)
    ri = lax.broadcasted_iota(i32, (RK_TM, RK_TM), 0)
    ci = lax.broadcasted_iota(i32, (RK_TM, RK_TM), 1)
    lower = (ci < ri).astype(bf16)
    before = jnp.dot(lower, picked.astype(bf16), preferred_element_type=f32)
    slot = before + carry_sc[...] + pstart_ref[...]
    lane_k = lax.broadcasted_iota(i32, (RK_TM, TOP_K), 1)
    dest = jnp.zeros((RK_TM, TOP_K), f32)
    for k in range(TOP_K):
        d = jnp.sum(jnp.where(lane == idx[:, k:k + 1], slot, 0.0), axis=-1, keepdims=True)
        dest = jnp.where(lane_k == k, d, dest)
    dest_ref[...] = dest.astype(i32)
    carry_sc[...] += jnp.sum(picked, axis=0, keepdims=True)


def _rank(idx, pstart):
    return pl.pallas_call(
        _rank_kernel,
        grid=(T // RK_TM,),
        in_specs=[pl.BlockSpec((RK_TM, TOP_K), lambda i: (i, 0)),
                  pl.BlockSpec((1, N_EXPERTS), lambda i: (0, 0))],
        out_specs=pl.BlockSpec((RK_TM, TOP_K), lambda i: (i, 0)),
        out_shape=jax.ShapeDtypeStruct((T, TOP_K), i32),
        scratch_shapes=[pltpu.VMEM((1, N_EXPERTS), f32)],
        compiler_params=pltpu.CompilerParams(dimension_semantics=("arbitrary",)),
        name="rank",
    )(idx, pstart)


DP_TM = 256
HALF = D_MODEL // 2


def _pack_bf16_pairs(x):
    bits = lax.bitcast_convert_type(x.astype(bf16).astype(f32), i32)
    return bits[:, HALF:] | lax.shift_right_logical(bits[:, :HALF], 16)


def _unpack_bf16_halves(w):
    lo = lax.bitcast_convert_type(lax.shift_left(w, 16), f32)
    hi = lax.bitcast_convert_type(w & jnp.int32(-65536), f32)
    return lo, hi


def _unpack_bf16_pairs(w):
    return jnp.concatenate(_unpack_bf16_halves(w), axis=1).astype(bf16)


ZROWS = 256
SUBLANES = 8


def _zero_fill(pstart_ref, plen_ref, meta_ref, xs_hbm, zbuf, zsem, wait):
    def go(cp):
        if wait:
            cp.wait()
        else:
            cp.start()

    def expert(e, carry):
        start, n = pstart_ref[e], plen_ref[e]
        head = n & (SUBLANES - 1)
        for t in range(SUBLANES - 1):
            @pl.when(t < head)
            def _():
                go(pltpu.make_async_copy(zbuf.at[pl.ds(0, 1)], xs_hbm.at[pl.ds(start + t, 1)],
                                         zsem.at[0]))
        b = SUBLANES
        while b <= ZROWS:
            @pl.when((n & b) != 0)
            def _(b=b):
                off = pl.multiple_of(start + head + (n & ~(2 * b - 1)), SUBLANES)
                go(pltpu.make_async_copy(zbuf.at[pl.ds(0, b)], xs_hbm.at[pl.ds(off, b)],
                                         zsem.at[0]))
            b *= 2
        return carry

    lax.fori_loop(0, N_EXPERTS, expert, 0)

    def unowned(blk, carry):
        for part in range(BM // ZROWS):
            off = pl.multiple_of(blk * BM + part * ZROWS, ZROWS)
            go(pltpu.make_async_copy(zbuf, xs_hbm.at[pl.ds(off, ZROWS)], zsem.at[0]))
        return carry

    lax.fori_loop(meta_ref[0], NB, unowned, 0)


def _dispatch_kernel(pstart_ref, plen_ref, meta_ref, dest_ref, h_ref, xs_hbm,
                     pack_sc, zbuf, sem, zsem):
    first = pl.program_id(0) == 0

    @pl.when(first)
    def _():
        zbuf[...] = jnp.zeros_like(zbuf)
        _zero_fill(pstart_ref, plen_ref, meta_ref, xs_hbm, zbuf, zsem, wait=False)

    pack_sc[...] = _pack_bf16_pairs(h_ref[...])

    def issue(t, carry):
        for k in range(TOP_K):
            pltpu.make_async_copy(pack_sc.at[pl.ds(t, 1)],
                                  xs_hbm.at[pl.ds(dest_ref[t * TOP_K + k], 1)],
                                  sem.at[0]).start(priority=k % 2)
        return carry

    lax.fori_loop(0, DP_TM, issue, 0)

    @pl.when(first)
    def _():
        _zero_fill(pstart_ref, plen_ref, meta_ref, xs_hbm, zbuf, zsem, wait=True)

    for _ in range(TOP_K):
        pltpu.make_async_copy(pack_sc, xs_hbm.at[pl.ds(0, DP_TM)], sem.at[0]).wait()


def _dispatch(h, dest_flat, pad_start, pad_len, meta):
    grid_spec = pltpu.PrefetchScalarGridSpec(
        num_scalar_prefetch=3,
        grid=(T // DP_TM,),
        in_specs=[pl.BlockSpec((DP_TM * TOP_K,), lambda i, *_: (i,), memory_space=pltpu.SMEM),
                  pl.BlockSpec((DP_TM, D_MODEL), lambda i, *_: (i, 0))],
        out_specs=pl.BlockSpec(memory_space=pl.ANY),
        scratch_shapes=[pltpu.VMEM((DP_TM, HALF), i32), pltpu.VMEM((ZROWS, HALF), i32),
                        pltpu.SemaphoreType.DMA((1,)), pltpu.SemaphoreType.DMA((1,))],
    )
    return pl.pallas_call(
        _dispatch_kernel,
        grid_spec=grid_spec,
        out_shape=jax.ShapeDtypeStruct((CAP, HALF), i32),
        compiler_params=pltpu.CompilerParams(dimension_semantics=("arbitrary",)),
        name="dispatch",
    )(pad_start, pad_len, meta, dest_flat, h)


GU_TN = 1024
DN_TN = 2048


def _stream_expert_weights(bs_ref, se_ref, meta_ref, w_hbm, col_offsets, tn, wbuf, sem, caches):
    j, i, nj = pl.program_id(0), pl.program_id(1), pl.num_programs(0)
    n_seg = jnp.maximum(meta_ref[1], 1)
    seg = bs_ref[i]
    first_block = (i == 0) | (seg != bs_ref[jnp.maximum(i - 1, 0)])

    def copies(s, jj, slot):
        expert = se_ref[s]
        out = []
        for c, off in enumerate(col_offsets):
            start = off + jj * tn
            if not isinstance(start, int):
                start = pl.multiple_of(start, tn)
            out.append(pltpu.make_async_copy(w_hbm.at[expert, :, pl.ds(start, tn)],
                                             wbuf.at[slot, c], sem.at[slot]))
        return out

    @pl.when(first_block)
    def _():
        seq = j * n_seg + seg
        slot = seq % 2

        @pl.when(seq == 0)
        def _():
            for cp in copies(0, 0, 0):
                cp.start()

        more = seg + 1 < n_seg

        @pl.when(more | (j + 1 < nj))
        def _():
            for cp in copies(jnp.where(more, seg + 1, 0), jnp.where(more, j, j + 1), 1 - slot):
                cp.start()

        for cp in copies(seg, j, slot):
            cp.wait()
        for c, cache in enumerate(caches):
            cache[...] = wbuf[slot, c].astype(bf16)


def _expert_gu_kernel(be_ref, bs_ref, se_ref, meta_ref, x_ref, w_hbm, bg_ref, bl_ref, o_ref,
                      wbuf, wg_sc, wl_sc, sem):
    del be_ref
    i = pl.program_id(1)
    last = jnp.maximum(meta_ref[0], 1) - 1

    @pl.when(i <= last)
    def _():
        _stream_expert_weights(bs_ref, se_ref, meta_ref, w_hbm, (0, D_MODEL), GU_TN,
                               wbuf, sem, (wg_sc, wl_sc))
        x = _unpack_bf16_pairs(x_ref[...])
        glu = jnp.dot(x, wg_sc[...], preferred_element_type=f32) + bg_ref[...]
        lin = jnp.dot(x, wl_sc[...], preferred_element_type=f32) + bl_ref[...]
        glu = jnp.minimum(glu, SWIGLU_LIMIT)
        lin = jnp.clip(lin, -SWIGLU_LIMIT, SWIGLU_LIMIT)
        o_ref[...] = (glu * jax.nn.sigmoid(SWIGLU_ALPHA * glu) * (lin + 1.0)).astype(o_ref.dtype)

    @pl.when(i > last)
    def _():
        o_ref[...] = jnp.zeros_like(o_ref)


def _used_block(j, i, be, bs, se, meta):
    return jnp.minimum(i, jnp.maximum(meta[0], 1) - 1)


def _expert_gu(layout, xs, w_gu, b_gu):
    nj = D_MODEL // GU_TN
    blk = _used_block
    grid_spec = pltpu.PrefetchScalarGridSpec(
        num_scalar_prefetch=4,
        grid=(nj, NB),
        in_specs=[
            pl.BlockSpec((BM, HALF), lambda *a: (blk(*a), 0)),
            pl.BlockSpec(memory_space=pl.ANY),
            pl.BlockSpec((None, 1, GU_TN), lambda *a: (a[2][blk(*a)], 0, a[0])),
            pl.BlockSpec((None, 1, GU_TN), lambda *a: (a[2][blk(*a)], 0, nj + a[0])),
        ],
        out_specs=pl.BlockSpec((BM, GU_TN), lambda j, i, *_: (i, j)),
        scratch_shapes=[pltpu.VMEM((2, 2, D_MODEL, GU_TN), f32),
                        pltpu.VMEM((D_MODEL, GU_TN), bf16), pltpu.VMEM((D_MODEL, GU_TN), bf16),
                        pltpu.SemaphoreType.DMA((2,))],
    )
    return pl.pallas_call(
        _expert_gu_kernel,
        grid_spec=grid_spec,
        out_shape=jax.ShapeDtypeStruct((CAP, D_MODEL), bf16),
        compiler_params=pltpu.CompilerParams(
            dimension_semantics=("arbitrary", "arbitrary"), vmem_limit_bytes=VMEM_LIMIT),
        name="expert_gu",
    )(*layout, xs, w_gu, b_gu, b_gu)


def _expert_down_kernel(be_ref, bs_ref, se_ref, meta_ref, a_ref, w_hbm, b_ref, o_ref,
                        wbuf, w_sc, sem):
    del be_ref
    i = pl.program_id(1)
    last = jnp.maximum(meta_ref[0], 1) - 1

    @pl.when(i <= last)
    def _():
        _stream_expert_weights(bs_ref, se_ref, meta_ref, w_hbm, (0,), DN_TN, wbuf, sem, (w_sc,))
        y = jnp.dot(a_ref[...], w_sc[...], preferred_element_type=f32) + b_ref[...]
        o_ref[...] = _pack_bf16_pairs(y)

    @pl.when(i > last)
    def _():
        o_ref[...] = jnp.zeros_like(o_ref)


def _expert_down(layout, act, w_down, b_down):
    blk = _used_block
    grid_spec = pltpu.PrefetchScalarGridSpec(
        num_scalar_prefetch=4,
        grid=(D_MODEL // DN_TN, NB),
        in_specs=[
            pl.BlockSpec((BM, D_MODEL), lambda *a: (blk(*a), 0)),
            pl.BlockSpec(memory_space=pl.ANY),
            pl.BlockSpec((None, 1, DN_TN), lambda *a: (a[2][blk(*a)], 0, a[0])),
        ],
        out_specs=pl.BlockSpec((BM, DN_TN // 2), lambda j, i, *_: (i, j)),
        scratch_shapes=[pltpu.VMEM((2, 1, D_MODEL, DN_TN), f32),
                        pltpu.VMEM((D_MODEL, DN_TN), bf16),
                        pltpu.SemaphoreType.DMA((2,))],
    )
    assert DN_TN == D_MODEL
    return pl.pallas_call(
        _expert_down_kernel,
        grid_spec=grid_spec,
        out_shape=jax.ShapeDtypeStruct((CAP, HALF), i32),
        compiler_params=pltpu.CompilerParams(
            dimension_semantics=("arbitrary", "arbitrary"), vmem_limit_bytes=VMEM_LIMIT),
        name="expert_down",
    )(*layout, act, w_down, b_down)


CB_TM = 128
CB_SLABS = 4


def _combine_kernel(dcur_ref, dnext_ref, y_hbm, gate_ref, h_ref, g_ref, b_ref, o_ref,
                    buf_a, buf_b, sem):
    i, n = pl.program_id(0), pl.num_programs(0)

    def start_rows(d_ref, buf, s, t):
        for k in range(TOP_K):
            pltpu.make_async_copy(y_hbm.at[pl.ds(d_ref[t * TOP_K + k], 1)],
                                  buf.at[k, pl.ds(t, 1)], sem.at[s]).start(priority=k % 2)

    def wait_tile(buf, s):
        for k in range(TOP_K):
            pltpu.make_async_copy(y_hbm.at[pl.ds(0, CB_TM)], buf.at[k], sem.at[s]).wait()

    @pl.when(i == 0)
    def _():
        def prime(t, carry):
            start_rows(dcur_ref, buf_a, 0, t)
            return carry
        lax.fori_loop(0, CB_TM, prime, 0)

    def step(cur, s_cur, nxt, s_nxt):
        wait_tile(cur, s_cur)

        def slabs(gi, carry):
            for part in range(CB_SLABS):
                t0 = pl.multiple_of((gi * CB_SLABS + part) * SUBLANES, SUBLANES)
                for tt in range(SUBLANES):
                    start_rows(dnext_ref, nxt, s_nxt, t0 + tt)
                rows = pl.ds(t0, SUBLANES)
                gate = gate_ref[rows, :]
                lo, hi = _unpack_bf16_halves(cur[0, rows, :])
                ffn_lo, ffn_hi = gate[:, 0:1] * lo, gate[:, 0:1] * hi
                for k in range(1, TOP_K):
                    lo, hi = _unpack_bf16_halves(cur[k, rows, :])
                    ffn_lo = ffn_lo + gate[:, k:k + 1] * lo
                    ffn_hi = ffn_hi + gate[:, k:k + 1] * hi
                ffn = jnp.concatenate([ffn_lo, ffn_hi], axis=1)
                o_ref[rows, :] = _layer_norm(ALPHA * h_ref[rows, :] + ffn, g_ref[...], b_ref[...])
            return carry

        lax.fori_loop(0, CB_TM // (SUBLANES * CB_SLABS), slabs, 0)

        @pl.when(i == n - 1)
        def _():
            wait_tile(nxt, s_nxt)

    @pl.when(i % 2 == 0)
    def _():
        step(buf_a, 0, buf_b, 1)

    @pl.when(i % 2 == 1)
    def _():
        step(buf_b, 1, buf_a, 0)


def _combine(y, dest_flat, gate, h, ln_g, ln_b):
    n_tiles = T // CB_TM
    return pl.pallas_call(
        _combine_kernel,
        grid=(n_tiles,),
        in_specs=[pl.BlockSpec((CB_TM * TOP_K,), lambda i: (i,), memory_space=pltpu.SMEM),
                  pl.BlockSpec((CB_TM * TOP_K,), lambda i: (jnp.minimum(i + 1, n_tiles - 1),),
                               memory_space=pltpu.SMEM),
                  pl.BlockSpec(memory_space=pl.ANY),
                  pl.BlockSpec((CB_TM, TOP_K), lambda i: (i, 0)),
                  pl.BlockSpec((CB_TM, D_MODEL), lambda i: (i, 0)),
                  pl.BlockSpec((1, D_MODEL), lambda i: (0, 0)),
                  pl.BlockSpec((1, D_MODEL), lambda i: (0, 0))],
        out_specs=pl.BlockSpec((CB_TM, D_MODEL), lambda i: (i, 0)),
        out_shape=jax.ShapeDtypeStruct((T, D_MODEL), f32),
        scratch_shapes=[pltpu.VMEM((TOP_K, CB_TM, HALF), i32),
                        pltpu.VMEM((TOP_K, CB_TM, HALF), i32),
                        pltpu.SemaphoreType.DMA((2,))],
        compiler_params=pltpu.CompilerParams(dimension_semantics=("arbitrary",),
                                             vmem_limit_bytes=VMEM_LIMIT),
        name="combine",
    )(dest_flat, dest_flat, y, gate, h, ln_g, ln_b)


def _expert_layout(counts):
    nblk = (counts + BM - 1) // BM
    bend = jnp.cumsum(nblk)
    pstart = (bend - nblk) * BM
    before = bend[None, :] <= jnp.arange(NB, dtype=i32)[:, None]
    block_e = jnp.minimum(jnp.sum(before.astype(i32), axis=1), N_EXPERTS - 1)
    has = nblk > 0
    block_seg = jnp.sum((before & has[None, :]).astype(i32), axis=1)
    seg_of_e = jnp.cumsum(has.astype(i32)) - 1
    ids = jnp.arange(N_EXPERTS, dtype=i32)
    seg_expert = jnp.sum(jnp.where(has[None, :] & (seg_of_e[None, :] == ids[:, None]),
                                   ids[None, :], 0), axis=1)
    meta = jnp.stack([bend[-1], jnp.sum(has.astype(i32))]).astype(i32)
    padding = ((pstart + counts).astype(i32), (nblk * BM - counts).astype(i32))
    return pstart, padding, (block_e, block_seg.astype(i32), seg_expert.astype(i32), meta)


def kernel(x, w_in, rel_bias, sgu_ln_g, sgu_ln_b, sgu_w, sgu_b, w_branch, w_out, ln1_g, ln1_b,
           w_router, b_router, w_gu, b_gu, w_down, b_down, ln2_g, ln2_b):
    x2 = x.reshape(T, D_MODEL)
    proj = _inproj(x2, w_in[0].astype(bf16))
    y_attn = _attention(proj.reshape(BATCH, SEQ, IN_COLS), _attention_bias(rel_bias))
    y_sgu = _sgu(proj, sgu_ln_g, sgu_ln_b, sgu_w[0], sgu_b[0].T)
    wr_hi = w_router[0].astype(bf16)
    wr_lo = (w_router[0] - wr_hi.astype(f32)).astype(bf16)
    h, idx, gate, counts = _out_ln_router(
        y_attn.reshape(T, BRANCH_WIDTH), y_sgu, proj, x2,
        w_branch[0, 0].astype(bf16), w_branch[0, 1].astype(bf16), w_out[0].astype(bf16),
        ln1_g, ln1_b, jnp.concatenate([wr_hi, wr_lo], axis=1), b_router)
    pstart, padding, layout = _expert_layout(counts[0].astype(i32))
    dest = _rank(idx, pstart.astype(f32)[None, :]).reshape(T * TOP_K)
    xs = _dispatch(h, dest, *padding, layout[-1])
    act = _expert_gu(layout, xs, w_gu.reshape(N_EXPERTS, D_MODEL, 2 * D_MODEL),
                     b_gu.reshape(N_EXPERTS, 1, 2 * D_MODEL))
    ye = _expert_down(layout, act, w_down.reshape(N_EXPERTS, D_MODEL, D_MODEL),
                      b_down.reshape(N_EXPERTS, 1, D_MODEL))
    out = _combine(ye, dest, gate, h, ln2_g, ln2_b)
    return out.reshape(BATCH, SEQ, D_MODEL)
```

```python
import functools
import math

import numpy as np
import jax
import jax.numpy as jnp
from jax import lax
from jax.experimental import pallas as pl
from jax.experimental.pallas import tpu as pltpu

D_MODEL = 2048
BATCH = 8
SEQ = 2048
T = BATCH * SEQ
BRANCH_WIDTH = D_MODEL // 2
HEAD_DIM = 128
ATTN_GROUPS = 3
WINDOWS = (128, 512, 2048)
DILATIONS = (1, 4, 16)
ATTN_HEADS = BRANCH_WIDTH // HEAD_DIM
QKV_WIDTH = ATTN_GROUPS * ATTN_HEADS * HEAD_DIM
ATTN_BLOCK = 128
NUM_BUCKETS = 32
MAX_DISTANCE = 2048
SGU_CHUNK = 128
SGU_GROUPS = BRANCH_WIDTH // SGU_CHUNK
IN_COLS = 3 * QKV_WIDTH + 2 * BRANCH_WIDTH + 2 * D_MODEL
N_EXPERTS = 32
TOP_K = 4
SWIGLU_ALPHA = 1.702
SWIGLU_LIMIT = 7.0
LN_EPS = 1e-5
ALPHA = 2.0 ** 0.25
LOG2E = math.log2(math.e)
ATTN_CHUNK = 16

COL_U = 3 * QKV_WIDTH // 1024
COL_V = COL_U + 1
COL_GA = COL_U + 2
COL_GS = COL_GA + 2

BM = 512
NB = (T * TOP_K + N_EXPERTS * (BM - 1) + BM - 1) // BM
CAP = NB * BM

VMEM_LIMIT = 56 * 1024 * 1024
VMEM_LIMIT_MIX = 61 * 1024 * 1024

f32 = jnp.float32
bf16 = jnp.bfloat16
i32 = jnp.int32


def _sigmoid(x):
    return 0.5 * jnp.tanh(0.5 * x) + 0.5


def _layer_norm(x, g, b):
    mu = jnp.mean(x, axis=-1, keepdims=True)
    xc = x - mu
    var = jnp.mean(xc * xc, axis=-1, keepdims=True)
    return xc * lax.rsqrt(var + LN_EPS) * g + b


IP_TM = 1024
IP_TN = 1024


IP_CHUNK = 256


def _inproj_kernel(x_ref, w_ref, o_ref, xb_sc):
    j = pl.program_id(1)

    @pl.when(j == 0)
    def _():
        xb_sc[...] = x_ref[...].astype(bf16)

    def project(epilogue):
        for c in range(0, IP_TN, IP_CHUNK):
            acc = jnp.dot(xb_sc[...], w_ref[:, c:c + IP_CHUNK], preferred_element_type=f32)
            o_ref[:, c:c + IP_CHUNK] = epilogue(acc)

    @pl.when(j < COL_U)
    def _():
        project(lambda a: a)

    @pl.when((j >= COL_U) & (j < COL_GA))
    def _():
        project(jax.nn.gelu)

    @pl.when(j >= COL_GA)
    def _():
        project(_sigmoid)


def _inproj(x2, wb):
    return pl.pallas_call(
        _inproj_kernel,
        grid=(T // IP_TM, IN_COLS // IP_TN),
        in_specs=[pl.BlockSpec((IP_TM, D_MODEL), lambda i, j: (i, 0)),
                  pl.BlockSpec((D_MODEL, IP_TN), lambda i, j: (0, j))],
        out_specs=pl.BlockSpec((IP_TM, IP_TN), lambda i, j: (i, j)),
        out_shape=jax.ShapeDtypeStruct((T, IN_COLS), f32),
        scratch_shapes=[pltpu.VMEM((IP_TM, D_MODEL), bf16)],
        compiler_params=pltpu.CompilerParams(
            dimension_semantics=("arbitrary", "arbitrary"), vmem_limit_bytes=VMEM_LIMIT),
        name="inproj",
    )(x2, wb)


def _attn_kernel(q0, q1, q2, k0, k1, k2, v0, v1, v2, bias_ref, o_ref,
                 m0, m1, m2, l0, l1, l2, a0, a1, a2, out_sc):
    qs, ks, vs = (q0, q1, q2), (k0, k1, k2), (v0, v1, v2)
    ms, ls, accs = (m0, m1, m2), (l0, l1, l2), (a0, a1, a2)
    scale = HEAD_DIM ** -0.5 * LOG2E
    QB = ATTN_BLOCK

    def rows(start, n, r):
        return pl.ds(start, n, stride=r) if r > 1 else pl.ds(start, n)

    def tiles(g, specs):
        r = DILATIONS[g]
        nk = specs[0][2]
        bias = bias_ref[g, :, 2 * QB - nk:]
        s, v = [], []
        for q_start, k_start, _ in specs:
            q = qs[g][rows(q_start, QB, r), :].astype(bf16)
            k = ks[g][rows(k_start, nk, r), :].astype(bf16)
            v.append(vs[g][rows(k_start, nk, r), :].astype(bf16))
            s.append(lax.dot_general(q, k, (((1,), (1,)), ((), ())), preferred_element_type=f32))
        s = jnp.stack(s) * scale + bias[None]
        m = jnp.max(s, axis=-1, keepdims=True)
        p = jnp.exp2(s - m)
        l = jnp.sum(p, axis=-1, keepdims=True)
        p = p.astype(bf16)
        out = []
        for t in range(len(specs)):
            pv = jnp.dot(p[t], v[t], preferred_element_type=f32)
            out.append((jnp.broadcast_to(m[t], pv.shape), jnp.broadcast_to(l[t], pv.shape), pv))
        return out

    for g in range(ATTN_GROUPS):
        r = DILATIONS[g]
        sub_len = SEQ // r
        first, rest = [], []
        for c in range(r):
            for n in range(sub_len // QB):
                q_start = c + n * QB * r
                dst = pl.ds(c * sub_len + n * QB, QB)
                if n == 0:
                    first.append(((q_start, q_start, QB), dst))
                else:
                    rest.append(((q_start, q_start - QB * r, 2 * QB), dst))
        for todo in (first, rest):
            for at in range(0, len(todo), ATTN_CHUNK):
                chunk = todo[at:at + ATTN_CHUNK]
                for (_, dst), (m, l, pv) in zip(chunk, tiles(g, [spec for spec, _ in chunk])):
                    ms[g][dst, :] = m
                    ls[g][dst, :] = l
                    accs[g][dst, :] = pv

    r_max = DILATIONS[-1]
    for c in range(r_max):
        sel = []
        for g in range(ATTN_GROUPS):
            r = DILATIONS[g]
            sel.append(rows((c % r) * (SEQ // r) + c // r, SEQ // r_max, r_max // r))
        m_g = [ms[g][sel[g], :] for g in range(ATTN_GROUPS)]
        m_all = jnp.maximum(jnp.maximum(m_g[0], m_g[1]), m_g[2])
        w_g = [jnp.exp2(m - m_all) for m in m_g]
        den = sum(w * ls[g][sel[g], :] for g, w in enumerate(w_g))
        num = sum(w * accs[g][sel[g], :] for g, w in enumerate(w_g))
        out_sc[rows(c, SEQ // r_max, r_max), :] = num / den
    o_ref[...] = out_sc[...].astype(o_ref.dtype)


def _attention(proj3, bias):
    hb = HEAD_DIM
    def col(kind, g):
        off = (kind * ATTN_GROUPS + g) * ATTN_HEADS
        return pl.BlockSpec((None, SEQ, hb), lambda b, h, off=off: (b, 0, off + h))
    in_specs = [col(kind, g) for kind in range(3) for g in range(ATTN_GROUPS)]
    in_specs.append(pl.BlockSpec((ATTN_GROUPS, None, ATTN_BLOCK, 2 * ATTN_BLOCK),
                                 lambda b, h: (0, h, 0, 0)))
    return pl.pallas_call(
        _attn_kernel,
        grid=(BATCH, ATTN_HEADS),
        in_specs=in_specs,
        out_specs=pl.BlockSpec((None, SEQ, hb), lambda b, h: (b, 0, h)),
        out_shape=jax.ShapeDtypeStruct((BATCH, SEQ, BRANCH_WIDTH), bf16),
        scratch_shapes=[pltpu.VMEM((SEQ, hb), f32)] * (3 * ATTN_GROUPS + 1),
        compiler_params=pltpu.CompilerParams(
            dimension_semantics=("arbitrary", "arbitrary"), vmem_limit_bytes=VMEM_LIMIT),
        name="attention",
    )(*([proj3] * 9), bias)


def _t5_bucket(dist):
    exact = NUM_BUCKETS // 2
    d = jnp.maximum(dist, 1).astype(f32)
    large = exact + (jnp.log(d / exact) / math.log(MAX_DISTANCE / exact)
                     * (NUM_BUCKETS - exact)).astype(i32)
    large = jnp.minimum(large, NUM_BUCKETS - 1)
    return jnp.where(dist < exact, dist, large)


def _attention_bias(rel_bias):
    qi = np.arange(ATTN_BLOCK, dtype=np.int32)[:, None]
    kj = np.arange(2 * ATTN_BLOCK, dtype=np.int32)[None, :]
    sub = qi + ATTN_BLOCK - kj
    table = rel_bias.reshape(NUM_BUCKETS, ATTN_GROUPS, ATTN_HEADS)
    out = []
    for g in range(ATTN_GROUPS):
        span = WINDOWS[g] // DILATIONS[g]
        band = (sub >= 0) & (sub <= span)
        bucket = _t5_bucket(jnp.asarray(DILATIONS[g] * np.clip(sub, 0, span), i32))
        onehot = jax.nn.one_hot(bucket.reshape(-1), NUM_BUCKETS, dtype=f32)
        b = jnp.dot(onehot, table[:, g], precision=lax.Precision.HIGHEST)
        b = jnp.where(band[:, :, None], b.reshape(ATTN_BLOCK, 2 * ATTN_BLOCK, ATTN_HEADS), -1e30)
        out.append(b.transpose(2, 0, 1))
    return jnp.stack(out, axis=0).astype(f32) * LOG2E


SGU_TM = 512


def _sgu_kernel(u_ref, v_ref, g_ref, b_ref, w_ref, bst_ref, o_ref):
    vn = _layer_norm(v_ref[...], g_ref[...], b_ref[...]).astype(bf16)
    ri = lax.broadcasted_iota(i32, (SGU_CHUNK, SGU_CHUNK), 0)
    ci = lax.broadcasted_iota(i32, (SGU_CHUNK, SGU_CHUNK), 1)
    causal = ci <= ri
    for g in range(SGU_GROUPS):
        wg = jnp.where(causal, w_ref[g], 0.0).astype(bf16)
        bias = bst_ref[:, g:g + 1]
        cs = slice(g * SGU_CHUNK, (g + 1) * SGU_CHUNK)
        for c in range(SGU_TM // SGU_CHUNK):
            rs = slice(c * SGU_CHUNK, (c + 1) * SGU_CHUNK)
            mixed = jnp.dot(wg, vn[rs, cs], preferred_element_type=f32) + bias
            o_ref[rs, cs] = (u_ref[rs, cs] * mixed).astype(o_ref.dtype)


def _sgu(proj, ln_g, ln_b, w_s, b_st):
    return pl.pallas_call(
        _sgu_kernel,
        grid=(T // SGU_TM,),
        in_specs=[pl.BlockSpec((SGU_TM, BRANCH_WIDTH), lambda i: (i, COL_U)),
                  pl.BlockSpec((SGU_TM, BRANCH_WIDTH), lambda i: (i, COL_V)),
                  pl.BlockSpec((1, BRANCH_WIDTH), lambda i: (0, 0)),
                  pl.BlockSpec((1, BRANCH_WIDTH), lambda i: (0, 0)),
                  pl.BlockSpec((SGU_GROUPS, SGU_CHUNK, SGU_CHUNK), lambda i: (0, 0, 0)),
                  pl.BlockSpec((SGU_CHUNK, SGU_GROUPS), lambda i: (0, 0))],
        out_specs=pl.BlockSpec((SGU_TM, BRANCH_WIDTH), lambda i: (i, 0)),
        out_shape=jax.ShapeDtypeStruct((T, BRANCH_WIDTH), bf16),
        compiler_params=pltpu.CompilerParams(
            dimension_semantics=("arbitrary",), vmem_limit_bytes=VMEM_LIMIT),
        name="sgu",
    )(proj, proj, ln_g, ln_b, w_s, b_st)


OL_TM = 512
GATE_TN = 1024


def _out_ln_router_kernel(ya_ref, ys_ref, ga0_ref, ga1_ref, gs0_ref, gs1_ref, x_ref,
                          wa_ref, ws_ref, wo_ref, g_ref, b_ref, wr_ref, br_ref,
                          h_ref, idx_ref, gate_ref, cnt_ref):
    i = pl.program_id(0)
    a = jnp.dot(ya_ref[...], wa_ref[...], preferred_element_type=f32)
    s = jnp.dot(ys_ref[...], ws_ref[...], preferred_element_type=f32)
    y = jnp.concatenate(
        [ga0_ref[...] * a[:, :GATE_TN] + gs0_ref[...] * s[:, :GATE_TN],
         ga1_ref[...] * a[:, GATE_TN:] + gs1_ref[...] * s[:, GATE_TN:]], axis=1).astype(bf16)
    mixed = jnp.dot(y, wo_ref[...], preferred_element_type=f32)
    h = _layer_norm(ALPHA * x_ref[...] + mixed, g_ref[...], b_ref[...])
    h_ref[...] = h
    h_hi = h.astype(bf16)
    h_lo = (h - h_hi.astype(f32)).astype(bf16)
    t = jnp.dot(h_hi, wr_ref[...], preferred_element_type=f32)
    u = jnp.dot(h_lo, wr_ref[:, :N_EXPERTS], preferred_element_type=f32)
    logits = t[:, :N_EXPERTS] + (t[:, N_EXPERTS:] + u) + br_ref[...]
    lane = lax.broadcasted_iota(i32, (OL_TM, N_EXPERTS), 1).astype(f32)
    lane_k = lax.broadcasted_iota(i32, (OL_TM, TOP_K), 1)
    idx_out = jnp.zeros((OL_TM, TOP_K), f32)
    val_out = jnp.zeros((OL_TM, TOP_K), f32)
    picked = jnp.zeros((OL_TM, N_EXPERTS), f32)
    work = logits
    for k in range(TOP_K):
        m = jnp.max(work, axis=-1, keepdims=True)
        idx = jnp.min(jnp.where(work == m, lane, float(N_EXPERTS)), axis=-1, keepdims=True)
        sel = lane == idx
        idx_out = jnp.where(lane_k == k, idx, idx_out)
        val_out = jnp.where(lane_k == k, m, val_out)
        picked = picked + sel.astype(f32)
        work = jnp.where(sel, -jnp.inf, work)
    e = jnp.exp(val_out - jnp.max(val_out, axis=-1, keepdims=True))
    gate_ref[...] = e / jnp.sum(e, axis=-1, keepdims=True)
    idx_ref[...] = idx_out.astype(i32)

    @pl.when(i == 0)
    def _():
        cnt_ref[...] = jnp.zeros_like(cnt_ref)

    cnt_ref[...] += jnp.sum(picked, axis=0, keepdims=True)


def _out_ln_router(ya, ys, proj, x2, wa, ws, wo, ln_g, ln_b, w_router, b_router):
    def resident(shape):
        return pl.BlockSpec(shape, lambda i: (0,) * len(shape), pipeline_mode=pl.Buffered(1))

    def gate_cols(block):
        return pl.BlockSpec((OL_TM, GATE_TN), lambda i, block=block: (i, block))

    return pl.pallas_call(
        _out_ln_router_kernel,
        grid=(T // OL_TM,),
        in_specs=[pl.BlockSpec((OL_TM, BRANCH_WIDTH), lambda i: (i, 0)),
                  pl.BlockSpec((OL_TM, BRANCH_WIDTH), lambda i: (i, 0)),
                  gate_cols(COL_GA), gate_cols(COL_GA + 1),
                  gate_cols(COL_GS), gate_cols(COL_GS + 1),
                  pl.BlockSpec((OL_TM, D_MODEL), lambda i: (i, 0)),
                  resident((BRANCH_WIDTH, D_MODEL)),
                  resident((BRANCH_WIDTH, D_MODEL)),
                  resident((D_MODEL, D_MODEL)),
                  resident((1, D_MODEL)),
                  resident((1, D_MODEL)),
                  resident((D_MODEL, 2 * N_EXPERTS)),
                  resident((1, N_EXPERTS))],
        out_specs=[pl.BlockSpec((OL_TM, D_MODEL), lambda i: (i, 0)),
                   pl.BlockSpec((OL_TM, TOP_K), lambda i: (i, 0)),
                   pl.BlockSpec((OL_TM, TOP_K), lambda i: (i, 0)),
                   pl.BlockSpec((1, N_EXPERTS), lambda i: (0, 0))],
        out_shape=[jax.ShapeDtypeStruct((T, D_MODEL), f32),
                   jax.ShapeDtypeStruct((T, TOP_K), i32),
                   jax.ShapeDtypeStruct((T, TOP_K), f32),
                   jax.ShapeDtypeStruct((1, N_EXPERTS), f32)],
        compiler_params=pltpu.CompilerParams(
            dimension_semantics=("arbitrary",), vmem_limit_bytes=VMEM_LIMIT_MIX),
        name="out_ln_router",
    )(ya, ys, proj, proj, proj, proj, x2, wa, ws, wo, ln_g, ln_b, w_router, b_router)


RK_TM = 512


def _rank_kernel(idx_ref, pstart_ref, dest_ref, carry_sc):
    i = pl.program_id(0)

    @pl.when(i == 0)
    def _():
        carry_sc[...] = jnp.zeros_like(carry_sc)

    idx = idx_ref[...]
    lane = lax.broadcasted_iota(i32, (RK_TM, N_EXPERTS), 1)
    picked = jnp.zeros((RK_TM, N_EXPERTS), f32)
    for k in range(TOP_K):
        picked = picked + (lane == idx[:, k:k + 1]).astype(f32)
    ri = lax.broadcasted_iota(i32, (RK_TM, RK_TM), 0)
    ci = lax.broadcasted_iota(i32, (RK_TM, RK_TM), 1)
    lower = (ci < ri).astype(bf16)
    before = jnp.dot(lower, picked.astype(bf16), preferred_element_type=f32)
    slot = before + carry_sc[...] + pstart_ref[...]
    lane_k = lax.broadcasted_iota(i32, (RK_TM, TOP_K), 1)
    dest = jnp.zeros((RK_TM, TOP_K), f32)
    for k in range(TOP_K):
        d = jnp.sum(jnp.where(lane == idx[:, k:k + 1], slot, 0.0), axis=-1, keepdims=True)
        dest = jnp.where(lane_k == k, d, dest)
    dest_ref[...] = dest.astype(i32)
    carry_sc[...] += jnp.sum(picked, axis=0, keepdims=True)


def _rank(idx, pstart):
    return pl.pallas_call(
        _rank_kernel,
        grid=(T // RK_TM,),
        in_specs=[pl.BlockSpec((RK_TM, TOP_K), lambda i: (i, 0)),
                  pl.BlockSpec((1, N_EXPERTS), lambda i: (0, 0))],
        out_specs=pl.BlockSpec((RK_TM, TOP_K), lambda i: (i, 0)),
        out_shape=jax.ShapeDtypeStruct((T, TOP_K), i32),
        scratch_shapes=[pltpu.VMEM((1, N_EXPERTS), f32)],
        compiler_params=pltpu.CompilerParams(dimension_semantics=("arbitrary",)),
        name="rank",
    )(idx, pstart)


DP_TM = 256
HALF = D_MODEL // 2


def _pack_bf16_pairs(x):
    bits = lax.bitcast_convert_type(x.astype(bf16).astype(f32), i32)
    return bits[:, HALF:] | lax.shift_right_logical(bits[:, :HALF], 16)


def _unpack_bf16_pairs(w):
    lo = lax.bitcast_convert_type(lax.shift_left(w, 16), f32)
    hi = lax.bitcast_convert_type(w & jnp.int32(-65536), f32)
    return jnp.concatenate([lo, hi], axis=1).astype(bf16)


ZROWS = 256
SUBLANES = 8


def _zero_fill(pstart_ref, plen_ref, meta_ref, xs_hbm, zbuf, zsem, wait):
    def go(cp):
        if wait:
            cp.wait()
        else:
            cp.start()

    def expert(e, carry):
        start, n = pstart_ref[e], plen_ref[e]
        head = n & (SUBLANES - 1)
        for t in range(SUBLANES - 1):
            @pl.when(t < head)
            def _():
                go(pltpu.make_async_copy(zbuf.at[pl.ds(0, 1)], xs_hbm.at[pl.ds(start + t, 1)],
                                         zsem.at[0]))
        b = SUBLANES
        while b <= ZROWS:
            @pl.when((n & b) != 0)
            def _(b=b):
                off = pl.multiple_of(start + head + (n & ~(2 * b - 1)), SUBLANES)
                go(pltpu.make_async_copy(zbuf.at[pl.ds(0, b)], xs_hbm.at[pl.ds(off, b)],
                                         zsem.at[0]))
            b *= 2
        return carry

    lax.fori_loop(0, N_EXPERTS, expert, 0)

    def unowned(blk, carry):
        for part in range(BM // ZROWS):
            off = pl.multiple_of(blk * BM + part * ZROWS, ZROWS)
            go(pltpu.make_async_copy(zbuf, xs_hbm.at[pl.ds(off, ZROWS)], zsem.at[0]))
        return carry

    lax.fori_loop(meta_ref[0], NB, unowned, 0)


def _dispatch_kernel(pstart_ref, plen_ref, meta_ref, dest_ref, h_ref, xs_hbm,
                     pack_sc, zbuf, sem, zsem):
    first = pl.program_id(0) == 0

    @pl.when(first)
    def _():
        zbuf[...] = jnp.zeros_like(zbuf)
        _zero_fill(pstart_ref, plen_ref, meta_ref, xs_hbm, zbuf, zsem, wait=False)

    pack_sc[...] = _pack_bf16_pairs(h_ref[...])

    def issue(t, carry):
        for k in range(TOP_K):
            pltpu.make_async_copy(pack_sc.at[pl.ds(t, 1)],
                                  xs_hbm.at[pl.ds(dest_ref[t * TOP_K + k], 1)],
                                  sem.at[0]).start(priority=k % 2)
        return carry

    lax.fori_loop(0, DP_TM, issue, 0)

    @pl.when(first)
    def _():
        _zero_fill(pstart_ref, plen_ref, meta_ref, xs_hbm, zbuf, zsem, wait=True)

    for _ in range(TOP_K):
        pltpu.make_async_copy(pack_sc, xs_hbm.at[pl.ds(0, DP_TM)], sem.at[0]).wait()


def _dispatch(h, dest_flat, pad_start, pad_len, meta):
    grid_spec = pltpu.PrefetchScalarGridSpec(
        num_scalar_prefetch=3,
        grid=(T // DP_TM,),
        in_specs=[pl.BlockSpec((DP_TM * TOP_K,), lambda i, *_: (i,), memory_space=pltpu.SMEM),
                  pl.BlockSpec((DP_TM, D_MODEL), lambda i, *_: (i, 0))],
        out_specs=pl.BlockSpec(memory_space=pl.ANY),
        scratch_shapes=[pltpu.VMEM((DP_TM, HALF), i32), pltpu.VMEM((ZROWS, HALF), i32),
                        pltpu.SemaphoreType.DMA((1,)), pltpu.SemaphoreType.DMA((1,))],
    )
    return pl.pallas_call(
        _dispatch_kernel,
        grid_spec=grid_spec,
        out_shape=jax.ShapeDtypeStruct((CAP, HALF), i32),
        compiler_params=pltpu.CompilerParams(dimension_semantics=("arbitrary",)),
        name="dispatch",
    )(pad_start, pad_len, meta, dest_flat, h)


GU_TN = 1024
DN_TN = 2048


def _stream_expert_weights(bs_ref, se_ref, meta_ref, w_hbm, col_offsets, tn, wbuf, sem, caches):
    j, i, nj = pl.program_id(0), pl.program_id(1), pl.num_programs(0)
    n_seg = jnp.maximum(meta_ref[1], 1)
    seg = bs_ref[i]
    first_block = (i == 0) | (seg != bs_ref[jnp.maximum(i - 1, 0)])

    def copies(s, jj, slot):
        expert = se_ref[s]
        out = []
        for c, off in enumerate(col_offsets):
            start = off + jj * tn
            if not isinstance(start, int):
                start = pl.multiple_of(start, tn)
            out.append(pltpu.make_async_copy(w_hbm.at[expert, :, pl.ds(start, tn)],
                                             wbuf.at[slot, c], sem.at[slot]))
        return out

    @pl.when(first_block)
    def _():
        seq = j * n_seg + seg
        slot = seq % 2

        @pl.when(seq == 0)
        def _():
            for cp in copies(0, 0, 0):
                cp.start()

        more = seg + 1 < n_seg

        @pl.when(more | (j + 1 < nj))
        def _():
            for cp in copies(jnp.where(more, seg + 1, 0), jnp.where(more, j, j + 1), 1 - slot):
                cp.start()

        for cp in copies(seg, j, slot):
            cp.wait()
        for c, cache in enumerate(caches):
            cache[...] = wbuf[slot, c].astype(bf16)


def _expert_gu_kernel(be_ref, bs_ref, se_ref, meta_ref, x_ref, w_hbm, bg_ref, bl_ref, o_ref,
                      wbuf, wg_sc, wl_sc, sem):
    del be_ref
    i = pl.program_id(1)
    last = jnp.maximum(meta_ref[0], 1) - 1

    @pl.when(i <= last)
    def _():
        _stream_expert_weights(bs_ref, se_ref, meta_ref, w_hbm, (0, D_MODEL), GU_TN,
                               wbuf, sem, (wg_sc, wl_sc))
        x = _unpack_bf16_pairs(x_ref[...])
        glu = jnp.dot(x, wg_sc[...], preferred_element_type=f32) + bg_ref[...]
        lin = jnp.dot(x, wl_sc[...], preferred_element_type=f32) + bl_ref[...]
        glu = jnp.minimum(glu, SWIGLU_LIMIT)
        lin = jnp.clip(lin, -SWIGLU_LIMIT, SWIGLU_LIMIT)
        o_ref[...] = (glu * _sigmoid(SWIGLU_ALPHA * glu) * (lin + 1.0)).astype(o_ref.dtype)

    @pl.when(i > last)
    def _():
        o_ref[...] = jnp.zeros_like(o_ref)


def _used_block(j, i, be, bs, se, meta):
    return jnp.minimum(i, jnp.maximum(meta[0], 1) - 1)


def _expert_gu(layout, xs, w_gu, b_gu):
    nj = D_MODEL // GU_TN
    blk = _used_block
    grid_spec = pltpu.PrefetchScalarGridSpec(
        num_scalar_prefetch=4,
        grid=(nj, NB),
        in_specs=[
            pl.BlockSpec((BM, HALF), lambda *a: (blk(*a), 0)),
            pl.BlockSpec(memory_space=pl.ANY),
            pl.BlockSpec((None, 1, GU_TN), lambda *a: (a[2][blk(*a)], 0, a[0])),
            pl.BlockSpec((None, 1, GU_TN), lambda *a: (a[2][blk(*a)], 0, nj + a[0])),
        ],
        out_specs=pl.BlockSpec((BM, GU_TN), lambda j, i, *_: (i, j)),
        scratch_shapes=[pltpu.VMEM((2, 2, D_MODEL, GU_TN), f32),
                        pltpu.VMEM((D_MODEL, GU_TN), bf16), pltpu.VMEM((D_MODEL, GU_TN), bf16),
                        pltpu.SemaphoreType.DMA((2,))],
    )
    return pl.pallas_call(
        _expert_gu_kernel,
        grid_spec=grid_spec,
        out_shape=jax.ShapeDtypeStruct((CAP, D_MODEL), bf16),
        compiler_params=pltpu.CompilerParams(
            dimension_semantics=("arbitrary", "arbitrary"), vmem_limit_bytes=VMEM_LIMIT),
        name="expert_gu",
    )(*layout, xs, w_gu, b_gu, b_gu)


def _expert_down_kernel(be_ref, bs_ref, se_ref, meta_ref, a_ref, w_hbm, b_ref, o_ref,
                        wbuf, w_sc, sem):
    del be_ref
    i = pl.program_id(1)
    last = jnp.maximum(meta_ref[0], 1) - 1

    @pl.when(i <= last)
    def _():
        _stream_expert_weights(bs_ref, se_ref, meta_ref, w_hbm, (0,), DN_TN, wbuf, sem, (w_sc,))
        o_ref[...] = jnp.dot(a_ref[...], w_sc[...], preferred_element_type=f32) + b_ref[...]

    @pl.when(i > last)
    def _():
        o_ref[...] = jnp.zeros_like(o_ref)


def _expert_down(layout, act, w_down, b_down):
    blk = _used_block
    grid_spec = pltpu.PrefetchScalarGridSpec(
        num_scalar_prefetch=4,
        grid=(D_MODEL // DN_TN, NB),
        in_specs=[
            pl.BlockSpec((BM, D_MODEL), lambda *a: (blk(*a), 0)),
            pl.BlockSpec(memory_space=pl.ANY),
            pl.BlockSpec((None, 1, DN_TN), lambda *a: (a[2][blk(*a)], 0, a[0])),
        ],
        out_specs=pl.BlockSpec((BM, DN_TN), lambda j, i, *_: (i, j)),
        scratch_shapes=[pltpu.VMEM((2, 1, D_MODEL, DN_TN), f32),
                        pltpu.VMEM((D_MODEL, DN_TN), bf16),
                        pltpu.SemaphoreType.DMA((2,))],
    )
    return pl.pallas_call(
        _expert_down_kernel,
        grid_spec=grid_spec,
        out_shape=jax.ShapeDtypeStruct((CAP, D_MODEL), f32),
        compiler_params=pltpu.CompilerParams(
            dimension_semantics=("arbitrary", "arbitrary"), vmem_limit_bytes=VMEM_LIMIT),
        name="expert_down",
    )(*layout, act, w_down, b_down)


CB_TM = 128
CB_SLABS = 4


def _combine_kernel(dcur_ref, dnext_ref, y_hbm, gate_ref, h_ref, g_ref, b_ref, o_ref,
                    buf_a, buf_b, sem):
    i, n = pl.program_id(0), pl.num_programs(0)

    def start_rows(d_ref, buf, s, t):
        for k in range(TOP_K):
            pltpu.make_async_copy(y_hbm.at[pl.ds(d_ref[t * TOP_K + k], 1)],
                                  buf.at[k, pl.ds(t, 1)], sem.at[s]).start(priority=k % 2)

    def wait_tile(buf, s):
        for k in range(TOP_K):
            pltpu.make_async_copy(y_hbm.at[pl.ds(0, CB_TM)], buf.at[k], sem.at[s]).wait()

    @pl.when(i == 0)
    def _():
        def prime(t, carry):
            start_rows(dcur_ref, buf_a, 0, t)
            return carry
        lax.fori_loop(0, CB_TM, prime, 0)

    def step(cur, s_cur, nxt, s_nxt):
        wait_tile(cur, s_cur)

        def slabs(gi, carry):
            for part in range(CB_SLABS):
                t0 = pl.multiple_of((gi * CB_SLABS + part) * SUBLANES, SUBLANES)
                for tt in range(SUBLANES):
                    start_rows(dnext_ref, nxt, s_nxt, t0 + tt)
                rows = pl.ds(t0, SUBLANES)
                gate = gate_ref[rows, :]
                ffn = gate[:, 0:1] * cur[0, rows, :]
                for k in range(1, TOP_K):
                    ffn = ffn + gate[:, k:k + 1] * cur[k, rows, :]
                o_ref[rows, :] = _layer_norm(ALPHA * h_ref[rows, :] + ffn, g_ref[...], b_ref[...])
            return carry

        lax.fori_loop(0, CB_TM // (SUBLANES * CB_SLABS), slabs, 0)

        @pl.when(i == n - 1)
        def _():
            wait_tile(nxt, s_nxt)

    @pl.when(i % 2 == 0)
    def _():
        step(buf_a, 0, buf_b, 1)

    @pl.when(i % 2 == 1)
    def _():
        step(buf_b, 1, buf_a, 0)


def _combine(y, dest_flat, gate, h, ln_g, ln_b):
    n_tiles = T // CB_TM
    return pl.pallas_call(
        _combine_kernel,
        grid=(n_tiles,),
        in_specs=[pl.BlockSpec((CB_TM * TOP_K,), lambda i: (i,), memory_space=pltpu.SMEM),
                  pl.BlockSpec((CB_TM * TOP_K,), lambda i: (jnp.minimum(i + 1, n_tiles - 1),),
                               memory_space=pltpu.SMEM),
                  pl.BlockSpec(memory_space=pl.ANY),
                  pl.BlockSpec((CB_TM, TOP_K), lambda i: (i, 0)),
                  pl.BlockSpec((CB_TM, D_MODEL), lambda i: (i, 0)),
                  pl.BlockSpec((1, D_MODEL), lambda i: (0, 0)),
                  pl.BlockSpec((1, D_MODEL), lambda i: (0, 0))],
        out_specs=pl.BlockSpec((CB_TM, D_MODEL), lambda i: (i, 0)),
        out_shape=jax.ShapeDtypeStruct((T, D_MODEL), f32),
        scratch_shapes=[pltpu.VMEM((TOP_K, CB_TM, D_MODEL), f32),
                        pltpu.VMEM((TOP_K, CB_TM, D_MODEL), f32),
                        pltpu.SemaphoreType.DMA((2,))],
        compiler_params=pltpu.CompilerParams(dimension_semantics=("arbitrary",),
                                             vmem_limit_bytes=VMEM_LIMIT),
        name="combine",
    )(dest_flat, dest_flat, y, gate, h, ln_g, ln_b)


def _expert_layout(counts):
    nblk = (counts + BM - 1) // BM
    bend = jnp.cumsum(nblk)
    pstart = (bend - nblk) * BM
    before = bend[None, :] <= jnp.arange(NB, dtype=i32)[:, None]
    block_e = jnp.minimum(jnp.sum(before.astype(i32), axis=1), N_EXPERTS - 1)
    has = nblk > 0
    block_seg = jnp.sum((before & has[None, :]).astype(i32), axis=1)
    seg_of_e = jnp.cumsum(has.astype(i32)) - 1
    ids = jnp.arange(N_EXPERTS, dtype=i32)
    seg_expert = jnp.sum(jnp.where(has[None, :] & (seg_of_e[None, :] == ids[:, None]),
                                   ids[None, :], 0), axis=1)
    meta = jnp.stack([bend[-1], jnp.sum(has.astype(i32))]).astype(i32)
    padding = ((pstart + counts).astype(i32), (nblk * BM - counts).astype(i32))
    return pstart, padding, (block_e, block_seg.astype(i32), seg_expert.astype(i32), meta)


def kernel(x, w_in, rel_bias, sgu_ln_g, sgu_ln_b, sgu_w, sgu_b, w_branch, w_out, ln1_g, ln1_b,
           w_router, b_router, w_gu, b_gu, w_down, b_down, ln2_g, ln2_b):
    x2 = x.reshape(T, D_MODEL)
    proj = _inproj(x2, w_in[0].astype(bf16))
    y_attn = _attention(proj.reshape(BATCH, SEQ, IN_COLS), _attention_bias(rel_bias))
    y_sgu = _sgu(proj, sgu_ln_g, sgu_ln_b, sgu_w[0], sgu_b[0].T)
    wr_hi = w_router[0].astype(bf16)
    wr_lo = (w_router[0] - wr_hi.astype(f32)).astype(bf16)
    h, idx, gate, counts = _out_ln_router(
        y_attn.reshape(T, BRANCH_WIDTH), y_sgu, proj, x2,
        w_branch[0, 0].astype(bf16), w_branch[0, 1].astype(bf16), w_out[0].astype(bf16),
        ln1_g, ln1_b, jnp.concatenate([wr_hi, wr_lo], axis=1), b_router)
    pstart, padding, layout = _expert_layout(counts[0].astype(i32))
    dest = _rank(idx, pstart.astype(f32)[None, :]).reshape(T * TOP_K)
    xs = _dispatch(h, dest, *padding, layout[-1])
    act = _expert_gu(layout, xs, w_gu.reshape(N_EXPERTS, D_MODEL, 2 * D_MODEL),
                     b_gu.reshape(N_EXPERTS, 1, 2 * D_MODEL))
    ye = _expert_down(layout, act, w_down.reshape(N_EXPERTS, D_MODEL, D_MODEL),
                      b_down.reshape(N_EXPERTS, 1, D_MODEL))
    out = _combine(ye, dest, gate, h, ln2_g, ln2_b)
    return out.reshape(BATCH, SEQ, D_MODEL)
```

```python
import functools
import math

import numpy as np
import jax
import jax.numpy as jnp
from jax import lax
from jax.experimental import pallas as pl
from jax.experimental.pallas import tpu as pltpu

D_MODEL = 2048
BATCH = 8
SEQ = 2048
T = BATCH * SEQ
BRANCH_WIDTH = D_MODEL // 2
HEAD_DIM = 128
ATTN_GROUPS = 3
WINDOWS = (128, 512, 2048)
DILATIONS = (1, 4, 16)
ATTN_HEADS = BRANCH_WIDTH // HEAD_DIM
QKV_WIDTH = ATTN_GROUPS * ATTN_HEADS * HEAD_DIM
ATTN_BLOCK = 128
NUM_BUCKETS = 32
MAX_DISTANCE = 2048
SGU_CHUNK = 128
SGU_GROUPS = BRANCH_WIDTH // SGU_CHUNK
IN_COLS = 3 * QKV_WIDTH + 2 * BRANCH_WIDTH + 2 * D_MODEL
N_EXPERTS = 32
TOP_K = 4
SWIGLU_ALPHA = 1.702
SWIGLU_LIMIT = 7.0
LN_EPS = 1e-5
ALPHA = 2.0 ** 0.25
LOG2E = math.log2(math.e)
ATTN_CHUNK = 16

COL_U = 3 * QKV_WIDTH // 1024
COL_V = COL_U + 1
COL_GA = COL_U + 2
COL_GS = COL_GA + 2

BM = 512
NB = (T * TOP_K + N_EXPERTS * (BM - 1) + BM - 1) // BM
CAP = NB * BM

VMEM_LIMIT = 56 * 1024 * 1024
VMEM_LIMIT_MIX = 61 * 1024 * 1024

f32 = jnp.float32
bf16 = jnp.bfloat16
i32 = jnp.int32


def _sigmoid(x):
    return 0.5 * jnp.tanh(0.5 * x) + 0.5


def _layer_norm(x, g, b):
    mu = jnp.mean(x, axis=-1, keepdims=True)
    xc = x - mu
    var = jnp.mean(xc * xc, axis=-1, keepdims=True)
    return xc * lax.rsqrt(var + LN_EPS) * g + b


IP_TM = 1024
IP_TN = 1536
IP_CHUNK = 256


def _inproj_epilogue(col):
    if col < COL_U * 1024:
        return "plain", lambda a: a
    if col < COL_GA * 1024:
        return "gelu", jax.nn.gelu
    return "sigmoid", _sigmoid


def _inproj_kernel(x_ref, w_ref, o_ref, xb_sc):
    j = pl.program_id(1)

    @pl.when(j == 0)
    def _():
        xb_sc[...] = x_ref[...].astype(bf16)

    patterns = {}
    for jj in range(IN_COLS // IP_TN):
        names = tuple(_inproj_epilogue(jj * IP_TN + c)[0] for c in range(0, IP_TN, IP_CHUNK))
        patterns.setdefault(names, []).append(jj)

    for tiles in patterns.values():
        cond = functools.reduce(lambda a, b: a | b, [j == jj for jj in tiles])

        @pl.when(cond)
        def _(first=tiles[0]):
            for c in range(0, IP_TN, IP_CHUNK):
                acc = jnp.dot(xb_sc[...], w_ref[:, c:c + IP_CHUNK], preferred_element_type=f32)
                o_ref[:, c:c + IP_CHUNK] = _inproj_epilogue(first * IP_TN + c)[1](acc)


def _inproj(x2, wb):
    return pl.pallas_call(
        _inproj_kernel,
        grid=(T // IP_TM, IN_COLS // IP_TN),
        in_specs=[pl.BlockSpec((IP_TM, D_MODEL), lambda i, j: (i, 0)),
                  pl.BlockSpec((D_MODEL, IP_TN), lambda i, j: (0, j))],
        out_specs=pl.BlockSpec((IP_TM, IP_TN), lambda i, j: (i, j)),
        out_shape=jax.ShapeDtypeStruct((T, IN_COLS), f32),
        scratch_shapes=[pltpu.VMEM((IP_TM, D_MODEL), bf16)],
        compiler_params=pltpu.CompilerParams(
            dimension_semantics=("arbitrary", "arbitrary"), vmem_limit_bytes=VMEM_LIMIT),
        name="inproj",
    )(x2, wb)


def _attn_kernel(q0, q1, q2, k0, k1, k2, v0, v1, v2, bias_ref, o_ref,
                 m0, m1, m2, l0, l1, l2, a0, a1, a2, out_sc):
    qs, ks, vs = (q0, q1, q2), (k0, k1, k2), (v0, v1, v2)
    ms, ls, accs = (m0, m1, m2), (l0, l1, l2), (a0, a1, a2)
    scale = HEAD_DIM ** -0.5 * LOG2E
    QB = ATTN_BLOCK

    def rows(start, n, r):
        return pl.ds(start, n, stride=r) if r > 1 else pl.ds(start, n)

    def tiles(g, specs):
        r = DILATIONS[g]
        nk = specs[0][2]
        bias = bias_ref[g, :, 2 * QB - nk:]
        s, v = [], []
        for q_start, k_start, _ in specs:
            q = qs[g][rows(q_start, QB, r), :].astype(bf16)
            k = ks[g][rows(k_start, nk, r), :].astype(bf16)
            v.append(vs[g][rows(k_start, nk, r), :].astype(bf16))
            s.append(lax.dot_general(q, k, (((1,), (1,)), ((), ())), preferred_element_type=f32))
        s = jnp.stack(s) * scale + bias[None]
        m = jnp.max(s, axis=-1, keepdims=True)
        p = jnp.exp2(s - m)
        l = jnp.sum(p, axis=-1, keepdims=True)
        p = p.astype(bf16)
        out = []
        for t in range(len(specs)):
            pv = jnp.dot(p[t], v[t], preferred_element_type=f32)
            out.append((jnp.broadcast_to(m[t], pv.shape), jnp.broadcast_to(l[t], pv.shape), pv))
        return out

    for g in range(ATTN_GROUPS):
        r = DILATIONS[g]
        sub_len = SEQ // r
        first, rest = [], []
        for c in range(r):
            for n in range(sub_len // QB):
                q_start = c + n * QB * r
                dst = pl.ds(c * sub_len + n * QB, QB)
                if n == 0:
                    first.append(((q_start, q_start, QB), dst))
                else:
                    rest.append(((q_start, q_start - QB * r, 2 * QB), dst))
        for todo in (first, rest):
            for at in range(0, len(todo), ATTN_CHUNK):
                chunk = todo[at:at + ATTN_CHUNK]
                for (_, dst), (m, l, pv) in zip(chunk, tiles(g, [spec for spec, _ in chunk])):
                    ms[g][dst, :] = m
                    ls[g][dst, :] = l
                    accs[g][dst, :] = pv

    r_max = DILATIONS[-1]
    for c in range(r_max):
        sel = []
        for g in range(ATTN_GROUPS):
            r = DILATIONS[g]
            sel.append(rows((c % r) * (SEQ // r) + c // r, SEQ // r_max, r_max // r))
        m_g = [ms[g][sel[g], :] for g in range(ATTN_GROUPS)]
        m_all = jnp.maximum(jnp.maximum(m_g[0], m_g[1]), m_g[2])
        w_g = [jnp.exp2(m - m_all) for m in m_g]
        den = sum(w * ls[g][sel[g], :] for g, w in enumerate(w_g))
        num = sum(w * accs[g][sel[g], :] for g, w in enumerate(w_g))
        out_sc[rows(c, SEQ // r_max, r_max), :] = num / den
    o_ref[...] = out_sc[...].astype(o_ref.dtype)


def _attention(proj3, bias):
    hb = HEAD_DIM
    def col(kind, g):
        off = (kind * ATTN_GROUPS + g) * ATTN_HEADS
        return pl.BlockSpec((None, SEQ, hb), lambda b, h, off=off: (b, 0, off + h))
    in_specs = [col(kind, g) for kind in range(3) for g in range(ATTN_GROUPS)]
    in_specs.append(pl.BlockSpec((ATTN_GROUPS, None, ATTN_BLOCK, 2 * ATTN_BLOCK),
                                 lambda b, h: (0, h, 0, 0)))
    return pl.pallas_call(
        _attn_kernel,
        grid=(BATCH, ATTN_HEADS),
        in_specs=in_specs,
        out_specs=pl.BlockSpec((None, SEQ, hb), lambda b, h: (b, 0, h)),
        out_shape=jax.ShapeDtypeStruct((BATCH, SEQ, BRANCH_WIDTH), bf16),
        scratch_shapes=[pltpu.VMEM((SEQ, hb), f32)] * (3 * ATTN_GROUPS + 1),
        compiler_params=pltpu.CompilerParams(
            dimension_semantics=("arbitrary", "arbitrary"), vmem_limit_bytes=VMEM_LIMIT),
        name="attention",
    )(*([proj3] * 9), bias)


def _t5_bucket(dist):
    exact = NUM_BUCKETS // 2
    d = jnp.maximum(dist, 1).astype(f32)
    large = exact + (jnp.log(d / exact) / math.log(MAX_DISTANCE / exact)
                     * (NUM_BUCKETS - exact)).astype(i32)
    large = jnp.minimum(large, NUM_BUCKETS - 1)
    return jnp.where(dist < exact, dist, large)


def _attention_bias(rel_bias):
    qi = np.arange(ATTN_BLOCK, dtype=np.int32)[:, None]
    kj = np.arange(2 * ATTN_BLOCK, dtype=np.int32)[None, :]
    sub = qi + ATTN_BLOCK - kj
    table = rel_bias.reshape(NUM_BUCKETS, ATTN_GROUPS, ATTN_HEADS)
    out = []
    for g in range(ATTN_GROUPS):
        span = WINDOWS[g] // DILATIONS[g]
        band = (sub >= 0) & (sub <= span)
        bucket = _t5_bucket(jnp.asarray(DILATIONS[g] * np.clip(sub, 0, span), i32))
        onehot = jax.nn.one_hot(bucket.reshape(-1), NUM_BUCKETS, dtype=f32)
        b = jnp.dot(onehot, table[:, g], precision=lax.Precision.HIGHEST)
        b = jnp.where(band[:, :, None], b.reshape(ATTN_BLOCK, 2 * ATTN_BLOCK, ATTN_HEADS), -1e30)
        out.append(b.transpose(2, 0, 1))
    return jnp.stack(out, axis=0).astype(f32) * LOG2E


SGU_TM = 512


def _sgu_kernel(u_ref, v_ref, g_ref, b_ref, w_ref, bst_ref, o_ref):
    vn = _layer_norm(v_ref[...], g_ref[...], b_ref[...]).astype(bf16)
    ri = lax.broadcasted_iota(i32, (SGU_CHUNK, SGU_CHUNK), 0)
    ci = lax.broadcasted_iota(i32, (SGU_CHUNK, SGU_CHUNK), 1)
    causal = ci <= ri
    for g in range(SGU_GROUPS):
        wg = jnp.where(causal, w_ref[g], 0.0).astype(bf16)
        bias = bst_ref[:, g:g + 1]
        cs = slice(g * SGU_CHUNK, (g + 1) * SGU_CHUNK)
        for c in range(SGU_TM // SGU_CHUNK):
            rs = slice(c * SGU_CHUNK, (c + 1) * SGU_CHUNK)
            mixed = jnp.dot(wg, vn[rs, cs], preferred_element_type=f32) + bias
            o_ref[rs, cs] = (u_ref[rs, cs] * mixed).astype(o_ref.dtype)


def _sgu(proj, ln_g, ln_b, w_s, b_st):
    return pl.pallas_call(
        _sgu_kernel,
        grid=(T // SGU_TM,),
        in_specs=[pl.BlockSpec((SGU_TM, BRANCH_WIDTH), lambda i: (i, COL_U)),
                  pl.BlockSpec((SGU_TM, BRANCH_WIDTH), lambda i: (i, COL_V)),
                  pl.BlockSpec((1, BRANCH_WIDTH), lambda i: (0, 0)),
                  pl.BlockSpec((1, BRANCH_WIDTH), lambda i: (0, 0)),
                  pl.BlockSpec((SGU_GROUPS, SGU_CHUNK, SGU_CHUNK), lambda i: (0, 0, 0)),
                  pl.BlockSpec((SGU_CHUNK, SGU_GROUPS), lambda i: (0, 0))],
        out_specs=pl.BlockSpec((SGU_TM, BRANCH_WIDTH), lambda i: (i, 0)),
        out_shape=jax.ShapeDtypeStruct((T, BRANCH_WIDTH), bf16),
        compiler_params=pltpu.CompilerParams(
            dimension_semantics=("arbitrary",), vmem_limit_bytes=VMEM_LIMIT),
        name="sgu",
    )(proj, proj, ln_g, ln_b, w_s, b_st)


OL_TM = 512
GATE_TN = 1024


def _out_ln_router_kernel(ya_ref, ys_ref, ga0_ref, ga1_ref, gs0_ref, gs1_ref, x_ref,
                          wa_ref, ws_ref, wo_ref, g_ref, b_ref, wr_ref, br_ref,
                          h_ref, idx_ref, gate_ref, cnt_ref):
    i = pl.program_id(0)
    a = jnp.dot(ya_ref[...], wa_ref[...], preferred_element_type=f32)
    s = jnp.dot(ys_ref[...], ws_ref[...], preferred_element_type=f32)
    y = jnp.concatenate(
        [ga0_ref[...] * a[:, :GATE_TN] + gs0_ref[...] * s[:, :GATE_TN],
         ga1_ref[...] * a[:, GATE_TN:] + gs1_ref[...] * s[:, GATE_TN:]], axis=1).astype(bf16)
    mixed = jnp.dot(y, wo_ref[...], preferred_element_type=f32)
    h = _layer_norm(ALPHA * x_ref[...] + mixed, g_ref[...], b_ref[...])
    h_ref[...] = h
    h_hi = h.astype(bf16)
    h_lo = (h - h_hi.astype(f32)).astype(bf16)
    t = jnp.dot(h_hi, wr_ref[...], preferred_element_type=f32)
    u = jnp.dot(h_lo, wr_ref[:, :N_EXPERTS], preferred_element_type=f32)
    logits = t[:, :N_EXPERTS] + (t[:, N_EXPERTS:] + u) + br_ref[...]
    lane = lax.broadcasted_iota(i32, (OL_TM, N_EXPERTS), 1).astype(f32)
    lane_k = lax.broadcasted_iota(i32, (OL_TM, TOP_K), 1)
    idx_out = jnp.zeros((OL_TM, TOP_K), f32)
    val_out = jnp.zeros((OL_TM, TOP_K), f32)
    picked = jnp.zeros((OL_TM, N_EXPERTS), f32)
    work = logits
    for k in range(TOP_K):
        m = jnp.max(work, axis=-1, keepdims=True)
        idx = jnp.min(jnp.where(work == m, lane, float(N_EXPERTS)), axis=-1, keepdims=True)
        sel = lane == idx
        idx_out = jnp.where(lane_k == k, idx, idx_out)
        val_out = jnp.where(lane_k == k, m, val_out)
        picked = picked + sel.astype(f32)
        work = jnp.where(sel, -jnp.inf, work)
    e = jnp.exp(val_out - jnp.max(val_out, axis=-1, keepdims=True))
    gate_ref[...] = e / jnp.sum(e, axis=-1, keepdims=True)
    idx_ref[...] = idx_out.astype(i32)

    @pl.when(i == 0)
    def _():
        cnt_ref[...] = jnp.zeros_like(cnt_ref)

    cnt_ref[...] += jnp.sum(picked, axis=0, keepdims=True)


def _out_ln_router(ya, ys, proj, x2, wa, ws, wo, ln_g, ln_b, w_router, b_router):
    def resident(shape):
        return pl.BlockSpec(shape, lambda i: (0,) * len(shape), pipeline_mode=pl.Buffered(1))

    def gate_cols(block):
        return pl.BlockSpec((OL_TM, GATE_TN), lambda i, block=block: (i, block))

    return pl.pallas_call(
        _out_ln_router_kernel,
        grid=(T // OL_TM,),
        in_specs=[pl.BlockSpec((OL_TM, BRANCH_WIDTH), lambda i: (i, 0)),
                  pl.BlockSpec((OL_TM, BRANCH_WIDTH), lambda i: (i, 0)),
                  gate_cols(COL_GA), gate_cols(COL_GA + 1),
                  gate_cols(COL_GS), gate_cols(COL_GS + 1),
                  pl.BlockSpec((OL_TM, D_MODEL), lambda i: (i, 0)),
                  resident((BRANCH_WIDTH, D_MODEL)),
                  resident((BRANCH_WIDTH, D_MODEL)),
                  resident((D_MODEL, D_MODEL)),
                  resident((1, D_MODEL)),
                  resident((1, D_MODEL)),
                  resident((D_MODEL, 2 * N_EXPERTS)),
                  resident((1, N_EXPERTS))],
        out_specs=[pl.BlockSpec((OL_TM, D_MODEL), lambda i: (i, 0)),
                   pl.BlockSpec((OL_TM, TOP_K), lambda i: (i, 0)),
                   pl.BlockSpec((OL_TM, TOP_K), lambda i: (i, 0)),
                   pl.BlockSpec((1, N_EXPERTS), lambda i: (0, 0))],
        out_shape=[jax.ShapeDtypeStruct((T, D_MODEL), f32),
                   jax.ShapeDtypeStruct((T, TOP_K), i32),
                   jax.ShapeDtypeStruct((T, TOP_K), f32),
                   jax.ShapeDtypeStruct((1, N_EXPERTS), f32)],
        compiler_params=pltpu.CompilerParams(
            dimension_semantics=("arbitrary",), vmem_limit_bytes=VMEM_LIMIT_MIX),
        name="out_ln_router",
    )(ya, ys, proj, proj, proj, proj, x2, wa, ws, wo, ln_g, ln_b, w_router, b_router)


RK_TM = 512


def _rank_kernel(idx_ref, pstart_ref, dest_ref, carry_sc):
    i = pl.program_id(0)

    @pl.when(i == 0)
    def _():
        carry_sc[...] = jnp.zeros_like(carry_sc)

    idx = idx_ref[...]
    lane = lax.broadcasted_iota(i32, (RK_TM, N_EXPERTS), 1)
    picked = jnp.zeros((RK_TM, N_EXPERTS), f32)
    for k in range(TOP_K):
        picked = picked + (lane == idx[:, k:k + 1]).astype(f32)
    ri = lax.broadcasted_iota(i32, (RK_TM, RK_TM), 0)
    ci = lax.broadcasted_iota(i32, (RK_TM, RK_TM), 1)
    lower = (ci < ri).astype(bf16)
    before = jnp.dot(lower, picked.astype(bf16), preferred_element_type=f32)
    slot = before + carry_sc[...] + pstart_ref[...]
    lane_k = lax.broadcasted_iota(i32, (RK_TM, TOP_K), 1)
    dest = jnp.zeros((RK_TM, TOP_K), f32)
    for k in range(TOP_K):
        d = jnp.sum(jnp.where(lane == idx[:, k:k + 1], slot, 0.0), axis=-1, keepdims=True)
        dest = jnp.where(lane_k == k, d, dest)
    dest_ref[...] = dest.astype(i32)
    carry_sc[...] += jnp.sum(picked, axis=0, keepdims=True)


def _rank(idx, pstart):
    return pl.pallas_call(
        _rank_kernel,
        grid=(T // RK_TM,),
        in_specs=[pl.BlockSpec((RK_TM, TOP_K), lambda i: (i, 0)),
                  pl.BlockSpec((1, N_EXPERTS), lambda i: (0, 0))],
        out_specs=pl.BlockSpec((RK_TM, TOP_K), lambda i: (i, 0)),
        out_shape=jax.ShapeDtypeStruct((T, TOP_K), i32),
        scratch_shapes=[pltpu.VMEM((1, N_EXPERTS), f32)],
        compiler_params=pltpu.CompilerParams(dimension_semantics=("arbitrary",)),
        name="rank",
    )(idx, pstart)


DP_TM = 256
HALF = D_MODEL // 2


def _pack_bf16_pairs(x):
    bits = lax.bitcast_convert_type(x.astype(bf16).astype(f32), i32)
    return bits[:, HALF:] | lax.shift_right_logical(bits[:, :HALF], 16)


def _unpack_bf16_pairs(w):
    lo = lax.bitcast_convert_type(lax.shift_left(w, 16), f32)
    hi = lax.bitcast_convert_type(w & jnp.int32(-65536), f32)
    return jnp.concatenate([lo, hi], axis=1).astype(bf16)


ZROWS = 256
SUBLANES = 8


def _zero_fill(pstart_ref, plen_ref, meta_ref, xs_hbm, zbuf, zsem, wait):
    def go(cp):
        if wait:
            cp.wait()
        else:
            cp.start()

    def expert(e, carry):
        start, n = pstart_ref[e], plen_ref[e]
        head = n & (SUBLANES - 1)
        for t in range(SUBLANES - 1):
            @pl.when(t < head)
            def _():
                go(pltpu.make_async_copy(zbuf.at[pl.ds(0, 1)], xs_hbm.at[pl.ds(start + t, 1)],
                                         zsem.at[0]))
        b = SUBLANES
        while b <= ZROWS:
            @pl.when((n & b) != 0)
            def _(b=b):
                off = pl.multiple_of(start + head + (n & ~(2 * b - 1)), SUBLANES)
                go(pltpu.make_async_copy(zbuf.at[pl.ds(0, b)], xs_hbm.at[pl.ds(off, b)],
                                         zsem.at[0]))
            b *= 2
        return carry

    lax.fori_loop(0, N_EXPERTS, expert, 0)

    def unowned(blk, carry):
        for part in range(BM // ZROWS):
            off = pl.multiple_of(blk * BM + part * ZROWS, ZROWS)
            go(pltpu.make_async_copy(zbuf, xs_hbm.at[pl.ds(off, ZROWS)], zsem.at[0]))
        return carry

    lax.fori_loop(meta_ref[0], NB, unowned, 0)


def _dispatch_kernel(pstart_ref, plen_ref, meta_ref, dest_ref, h_ref, xs_hbm,
                     pack_sc, zbuf, sem, zsem):
    first = pl.program_id(0) == 0

    @pl.when(first)
    def _():
        zbuf[...] = jnp.zeros_like(zbuf)
        _zero_fill(pstart_ref, plen_ref, meta_ref, xs_hbm, zbuf, zsem, wait=False)

    pack_sc[...] = _pack_bf16_pairs(h_ref[...])

    def issue(t, carry):
        for k in range(TOP_K):
            pltpu.make_async_copy(pack_sc.at[pl.ds(t, 1)],
                                  xs_hbm.at[pl.ds(dest_ref[t * TOP_K + k], 1)],
                                  sem.at[0]).start(priority=k % 2)
        return carry

    lax.fori_loop(0, DP_TM, issue, 0)

    @pl.when(first)
    def _():
        _zero_fill(pstart_ref, plen_ref, meta_ref, xs_hbm, zbuf, zsem, wait=True)

    for _ in range(TOP_K):
        pltpu.make_async_copy(pack_sc, xs_hbm.at[pl.ds(0, DP_TM)], sem.at[0]).wait()


def _dispatch(h, dest_flat, pad_start, pad_len, meta):
    grid_spec = pltpu.PrefetchScalarGridSpec(
        num_scalar_prefetch=3,
        grid=(T // DP_TM,),
        in_specs=[pl.BlockSpec((DP_TM * TOP_K,), lambda i, *_: (i,), memory_space=pltpu.SMEM),
                  pl.BlockSpec((DP_TM, D_MODEL), lambda i, *_: (i, 0))],
        out_specs=pl.BlockSpec(memory_space=pl.ANY),
        scratch_shapes=[pltpu.VMEM((DP_TM, HALF), i32), pltpu.VMEM((ZROWS, HALF), i32),
                        pltpu.SemaphoreType.DMA((1,)), pltpu.SemaphoreType.DMA((1,))],
    )
    return pl.pallas_call(
        _dispatch_kernel,
        grid_spec=grid_spec,
        out_shape=jax.ShapeDtypeStruct((CAP, HALF), i32),
        compiler_params=pltpu.CompilerParams(dimension_semantics=("arbitrary",)),
        name="dispatch",
    )(pad_start, pad_len, meta, dest_flat, h)


GU_TN = 1024
DN_TN = 2048


def _stream_expert_weights(bs_ref, se_ref, meta_ref, w_hbm, col_offsets, tn, wbuf, sem, caches):
    j, i, nj = pl.program_id(0), pl.program_id(1), pl.num_programs(0)
    n_seg = jnp.maximum(meta_ref[1], 1)
    seg = bs_ref[i]
    first_block = (i == 0) | (seg != bs_ref[jnp.maximum(i - 1, 0)])

    def copies(s, jj, slot):
        expert = se_ref[s]
        out = []
        for c, off in enumerate(col_offsets):
            start = off + jj * tn
            if not isinstance(start, int):
                start = pl.multiple_of(start, tn)
            out.append(pltpu.make_async_copy(w_hbm.at[expert, :, pl.ds(start, tn)],
                                             wbuf.at[slot, c], sem.at[slot]))
        return out

    @pl.when(first_block)
    def _():
        seq = j * n_seg + seg
        slot = seq % 2

        @pl.when(seq == 0)
        def _():
            for cp in copies(0, 0, 0):
                cp.start()

        more = seg + 1 < n_seg

        @pl.when(more | (j + 1 < nj))
        def _():
            for cp in copies(jnp.where(more, seg + 1, 0), jnp.where(more, j, j + 1), 1 - slot):
                cp.start()

        for cp in copies(seg, j, slot):
            cp.wait()
        for c, cache in enumerate(caches):
            cache[...] = wbuf[slot, c].astype(bf16)


def _expert_gu_kernel(be_ref, bs_ref, se_ref, meta_ref, x_ref, w_hbm, bg_ref, bl_ref, o_ref,
                      wbuf, wg_sc, wl_sc, sem):
    del be_ref
    i = pl.program_id(1)
    last = jnp.maximum(meta_ref[0], 1) - 1

    @pl.when(i <= last)
    def _():
        _stream_expert_weights(bs_ref, se_ref, meta_ref, w_hbm, (0, D_MODEL), GU_TN,
                               wbuf, sem, (wg_sc, wl_sc))
        x = _unpack_bf16_pairs(x_ref[...])
        glu = jnp.dot(x, wg_sc[...], preferred_element_type=f32) + bg_ref[...]
        lin = jnp.dot(x, wl_sc[...], preferred_element_type=f32) + bl_ref[...]
        glu = jnp.minimum(glu, SWIGLU_LIMIT)
        lin = jnp.clip(lin, -SWIGLU_LIMIT, SWIGLU_LIMIT)
        o_ref[...] = (glu * _sigmoid(SWIGLU_ALPHA * glu) * (lin + 1.0)).astype(o_ref.dtype)

    @pl.when(i > last)
    def _():
        o_ref[...] = jnp.zeros_like(o_ref)


def _used_block(j, i, be, bs, se, meta):
    return jnp.minimum(i, jnp.maximum(meta[0], 1) - 1)


def _expert_gu(layout, xs, w_gu, b_gu):
    nj = D_MODEL // GU_TN
    blk = _used_block
    grid_spec = pltpu.PrefetchScalarGridSpec(
        num_scalar_prefetch=4,
        grid=(nj, NB),
        in_specs=[
            pl.BlockSpec((BM, HALF), lambda *a: (blk(*a), 0)),
            pl.BlockSpec(memory_space=pl.ANY),
            pl.BlockSpec((None, 1, GU_TN), lambda *a: (a[2][blk(*a)], 0, a[0])),
            pl.BlockSpec((None, 1, GU_TN), lambda *a: (a[2][blk(*a)], 0, nj + a[0])),
        ],
        out_specs=pl.BlockSpec((BM, GU_TN), lambda j, i, *_: (i, j)),
        scratch_shapes=[pltpu.VMEM((2, 2, D_MODEL, GU_TN), f32),
                        pltpu.VMEM((D_MODEL, GU_TN), bf16), pltpu.VMEM((D_MODEL, GU_TN), bf16),
                        pltpu.SemaphoreType.DMA((2,))],
    )
    return pl.pallas_call(
        _expert_gu_kernel,
        grid_spec=grid_spec,
        out_shape=jax.ShapeDtypeStruct((CAP, D_MODEL), bf16),
        compiler_params=pltpu.CompilerParams(
            dimension_semantics=("arbitrary", "arbitrary"), vmem_limit_bytes=VMEM_LIMIT),
        name="expert_gu",
    )(*layout, xs, w_gu, b_gu, b_gu)


def _expert_down_kernel(be_ref, bs_ref, se_ref, meta_ref, a_ref, w_hbm, b_ref, o_ref,
                        wbuf, w_sc, sem):
    del be_ref
    i = pl.program_id(1)
    last = jnp.maximum(meta_ref[0], 1) - 1

    @pl.when(i <= last)
    def _():
        _stream_expert_weights(bs_ref, se_ref, meta_ref, w_hbm, (0,), DN_TN, wbuf, sem, (w_sc,))
        o_ref[...] = jnp.dot(a_ref[...], w_sc[...], preferred_element_type=f32) + b_ref[...]

    @pl.when(i > last)
    def _():
        o_ref[...] = jnp.zeros_like(o_ref)


def _expert_down(layout, act, w_down, b_down):
    blk = _used_block
    grid_spec = pltpu.PrefetchScalarGridSpec(
        num_scalar_prefetch=4,
        grid=(D_MODEL // DN_TN, NB),
        in_specs=[
            pl.BlockSpec((BM, D_MODEL), lambda *a: (blk(*a), 0)),
            pl.BlockSpec(memory_space=pl.ANY),
            pl.BlockSpec((None, 1, DN_TN), lambda *a: (a[2][blk(*a)], 0, a[0])),
        ],
        out_specs=pl.BlockSpec((BM, DN_TN), lambda j, i, *_: (i, j)),
        scratch_shapes=[pltpu.VMEM((2, 1, D_MODEL, DN_TN), f32),
                        pltpu.VMEM((D_MODEL, DN_TN), bf16),
                        pltpu.SemaphoreType.DMA((2,))],
    )
    return pl.pallas_call(
        _expert_down_kernel,
        grid_spec=grid_spec,
        out_shape=jax.ShapeDtypeStruct((CAP, D_MODEL), f32),
        compiler_params=pltpu.CompilerParams(
            dimension_semantics=("arbitrary", "arbitrary"), vmem_limit_bytes=VMEM_LIMIT),
        name="expert_down",
    )(*layout, act, w_down, b_down)


CB_TM = 128
CB_SLABS = 4


def _combine_kernel(dcur_ref, dnext_ref, y_hbm, gate_ref, h_ref, g_ref, b_ref, o_ref,
                    buf_a, buf_b, sem):
    i, n = pl.program_id(0), pl.num_programs(0)

    def start_rows(d_ref, buf, s, t):
        for k in range(TOP_K):
            pltpu.make_async_copy(y_hbm.at[pl.ds(d_ref[t * TOP_K + k], 1)],
                                  buf.at[k, pl.ds(t, 1)], sem.at[s]).start(priority=k % 2)

    def wait_tile(buf, s):
        for k in range(TOP_K):
            pltpu.make_async_copy(y_hbm.at[pl.ds(0, CB_TM)], buf.at[k], sem.at[s]).wait()

    @pl.when(i == 0)
    def _():
        def prime(t, carry):
            start_rows(dcur_ref, buf_a, 0, t)
            return carry
        lax.fori_loop(0, CB_TM, prime, 0)

    def step(cur, s_cur, nxt, s_nxt):
        wait_tile(cur, s_cur)

        def slabs(gi, carry):
            for part in range(CB_SLABS):
                t0 = pl.multiple_of((gi * CB_SLABS + part) * SUBLANES, SUBLANES)
                for tt in range(SUBLANES):
                    start_rows(dnext_ref, nxt, s_nxt, t0 + tt)
                rows = pl.ds(t0, SUBLANES)
                gate = gate_ref[rows, :]
                ffn = gate[:, 0:1] * cur[0, rows, :]
                for k in range(1, TOP_K):
                    ffn = ffn + gate[:, k:k + 1] * cur[k, rows, :]
                o_ref[rows, :] = _layer_norm(ALPHA * h_ref[rows, :] + ffn, g_ref[...], b_ref[...])
            return carry

        lax.fori_loop(0, CB_TM // (SUBLANES * CB_SLABS), slabs, 0)

        @pl.when(i == n - 1)
        def _():
            wait_tile(nxt, s_nxt)

    @pl.when(i % 2 == 0)
    def _():
        step(buf_a, 0, buf_b, 1)

    @pl.when(i % 2 == 1)
    def _():
        step(buf_b, 1, buf_a, 0)


def _combine(y, dest_flat, gate, h, ln_g, ln_b):
    n_tiles = T // CB_TM
    return pl.pallas_call(
        _combine_kernel,
        grid=(n_tiles,),
        in_specs=[pl.BlockSpec((CB_TM * TOP_K,), lambda i: (i,), memory_space=pltpu.SMEM),
                  pl.BlockSpec((CB_TM * TOP_K,), lambda i: (jnp.minimum(i + 1, n_tiles - 1),),
                               memory_space=pltpu.SMEM),
                  pl.BlockSpec(memory_space=pl.ANY),
                  pl.BlockSpec((CB_TM, TOP_K), lambda i: (i, 0)),
                  pl.BlockSpec((CB_TM, D_MODEL), lambda i: (i, 0)),
                  pl.BlockSpec((1, D_MODEL), lambda i: (0, 0)),
                  pl.BlockSpec((1, D_MODEL), lambda i: (0, 0))],
        out_specs=pl.BlockSpec((CB_TM, D_MODEL), lambda i: (i, 0)),
        out_shape=jax.ShapeDtypeStruct((T, D_MODEL), f32),
        scratch_shapes=[pltpu.VMEM((TOP_K, CB_TM, D_MODEL), f32),
                        pltpu.VMEM((TOP_K, CB_TM, D_MODEL), f32),
                        pltpu.SemaphoreType.DMA((2,))],
        compiler_params=pltpu.CompilerParams(dimension_semantics=("arbitrary",),
                                             vmem_limit_bytes=VMEM_LIMIT),
        name="combine",
    )(dest_flat, dest_flat, y, gate, h, ln_g, ln_b)


def _expert_layout(counts):
    nblk = (counts + BM - 1) // BM
    bend = jnp.cumsum(nblk)
    pstart = (bend - nblk) * BM
    before = bend[None, :] <= jnp.arange(NB, dtype=i32)[:, None]
    block_e = jnp.minimum(jnp.sum(before.astype(i32), axis=1), N_EXPERTS - 1)
    has = nblk > 0
    block_seg = jnp.sum((before & has[None, :]).astype(i32), axis=1)
    seg_of_e = jnp.cumsum(has.astype(i32)) - 1
    ids = jnp.arange(N_EXPERTS, dtype=i32)
    seg_expert = jnp.sum(jnp.where(has[None, :] & (seg_of_e[None, :] == ids[:, None]),
                                   ids[None, :], 0), axis=1)
    meta = jnp.stack([bend[-1], jnp.sum(has.astype(i32))]).astype(i32)
    padding = ((pstart + counts).astype(i32), (nblk * BM - counts).astype(i32))
    return pstart, padding, (block_e, block_seg.astype(i32), seg_expert.astype(i32), meta)


def kernel(x, w_in, rel_bias, sgu_ln_g, sgu_ln_b, sgu_w, sgu_b, w_branch, w_out, ln1_g, ln1_b,
           w_router, b_router, w_gu, b_gu, w_down, b_down, ln2_g, ln2_b):
    x2 = x.reshape(T, D_MODEL)
    proj = _inproj(x2, w_in[0].astype(bf16))
    y_attn = _attention(proj.reshape(BATCH, SEQ, IN_COLS), _attention_bias(rel_bias))
    y_sgu = _sgu(proj, sgu_ln_g, sgu_ln_b, sgu_w[0], sgu_b[0].T)
    wr_hi = w_router[0].astype(bf16)
    wr_lo = (w_router[0] - wr_hi.astype(f32)).astype(bf16)
    h, idx, gate, counts = _out_ln_router(
        y_attn.reshape(T, BRANCH_WIDTH), y_sgu, proj, x2,
        w_branch[0, 0].astype(bf16), w_branch[0, 1].astype(bf16), w_out[0].astype(bf16),
        ln1_g, ln1_b, jnp.concatenate([wr_hi, wr_lo], axis=1), b_router)
    pstart, padding, layout = _expert_layout(counts[0].astype(i32))
    dest = _rank(idx, pstart.astype(f32)[None, :]).reshape(T * TOP_K)
    xs = _dispatch(h, dest, *padding, layout[-1])
    act = _expert_gu(layout, xs, w_gu.reshape(N_EXPERTS, D_MODEL, 2 * D_MODEL),
                     b_gu.reshape(N_EXPERTS, 1, 2 * D_MODEL))
    ye = _expert_down(layout, act, w_down.reshape(N_EXPERTS, D_MODEL, D_MODEL),
                      b_down.reshape(N_EXPERTS, 1, D_MODEL))
    out = _combine(ye, dest, gate, h, ln2_g, ln2_b)
    return out.reshape(BATCH, SEQ, D_MODEL)
```

```python
import functools
import math

import numpy as np
import jax
import jax.numpy as jnp
from jax import lax
from jax.experimental import pallas as pl
from jax.experimental.pallas import tpu as pltpu

D_MODEL = 2048
BATCH = 8
SEQ = 2048
T = BATCH * SEQ
BRANCH_WIDTH = D_MODEL // 2
HEAD_DIM = 128
ATTN_GROUPS = 3
WINDOWS = (128, 512, 2048)
DILATIONS = (1, 4, 16)
ATTN_HEADS = BRANCH_WIDTH // HEAD_DIM
QKV_WIDTH = ATTN_GROUPS * ATTN_HEADS * HEAD_DIM
ATTN_BLOCK = 128
NUM_BUCKETS = 32
MAX_DISTANCE = 2048
SGU_CHUNK = 128
SGU_GROUPS = BRANCH_WIDTH // SGU_CHUNK
IN_COLS = 3 * QKV_WIDTH + 2 * BRANCH_WIDTH + 2 * D_MODEL
N_EXPERTS = 32
TOP_K = 4
SWIGLU_ALPHA = 1.702
SWIGLU_LIMIT = 7.0
LN_EPS = 1e-5
ALPHA = 2.0 ** 0.25
LOG2E = math.log2(math.e)
ATTN_CHUNK = 16

COL_U = 3 * QKV_WIDTH // 1024
COL_V = COL_U + 1
COL_GA = COL_U + 2
COL_GS = COL_GA + 2

BM = 512
NB = (T * TOP_K + N_EXPERTS * (BM - 1) + BM - 1) // BM
CAP = NB * BM

VMEM_LIMIT = 56 * 1024 * 1024
VMEM_LIMIT_MIX = 61 * 1024 * 1024

f32 = jnp.float32
bf16 = jnp.bfloat16
i32 = jnp.int32


def _sigmoid(x):
    return 0.5 * jnp.tanh(0.5 * x) + 0.5


def _layer_norm(x, g, b):
    mu = jnp.mean(x, axis=-1, keepdims=True)
    xc = x - mu
    var = jnp.mean(xc * xc, axis=-1, keepdims=True)
    return xc * lax.rsqrt(var + LN_EPS) * g + b


IP_TM = 1024
IP_TN = 1536
IP_CHUNK = 256


def _inproj_epilogue(col):
    if col < COL_U * 1024:
        return "plain", lambda a: a
    if col < COL_GA * 1024:
        return "gelu", jax.nn.gelu
    return "sigmoid", _sigmoid


def _inproj_kernel(x_ref, w_ref, o_ref, xb_sc):
    j = pl.program_id(1)

    @pl.when(j == 0)
    def _():
        xb_sc[...] = x_ref[...].astype(bf16)

    patterns = {}
    for jj in range(IN_COLS // IP_TN):
        names = tuple(_inproj_epilogue(jj * IP_TN + c)[0] for c in range(0, IP_TN, IP_CHUNK))
        patterns.setdefault(names, []).append(jj)

    for tiles in patterns.values():
        cond = functools.reduce(lambda a, b: a | b, [j == jj for jj in tiles])

        @pl.when(cond)
        def _(first=tiles[0]):
            for c in range(0, IP_TN, IP_CHUNK):
                acc = jnp.dot(xb_sc[...], w_ref[:, c:c + IP_CHUNK], preferred_element_type=f32)
                o_ref[:, c:c + IP_CHUNK] = _inproj_epilogue(first * IP_TN + c)[1](acc)


def _inproj(x2, wb):
    return pl.pallas_call(
        _inproj_kernel,
        grid=(T // IP_TM, IN_COLS // IP_TN),
        in_specs=[pl.BlockSpec((IP_TM, D_MODEL), lambda i, j: (i, 0)),
                  pl.BlockSpec((D_MODEL, IP_TN), lambda i, j: (0, j))],
        out_specs=pl.BlockSpec((IP_TM, IP_TN), lambda i, j: (i, j)),
        out_shape=jax.ShapeDtypeStruct((T, IN_COLS), f32),
        scratch_shapes=[pltpu.VMEM((IP_TM, D_MODEL), bf16)],
        compiler_params=pltpu.CompilerParams(
            dimension_semantics=("arbitrary", "arbitrary"), vmem_limit_bytes=VMEM_LIMIT),
        name="inproj",
    )(x2, wb)


def _attn_kernel(q0, q1, q2, k0, k1, k2, v0, v1, v2, bias_ref, o_ref,
                 m0, m1, m2, l0, l1, l2, a0, a1, a2, out_sc):
    qs, ks, vs = (q0, q1, q2), (k0, k1, k2), (v0, v1, v2)
    ms, ls, accs = (m0, m1, m2), (l0, l1, l2), (a0, a1, a2)
    scale = HEAD_DIM ** -0.5 * LOG2E
    QB = ATTN_BLOCK

    def rows(start, n, r):
        return pl.ds(start, n, stride=r) if r > 1 else pl.ds(start, n)

    def tiles(g, specs):
        r = DILATIONS[g]
        nk = specs[0][2]
        bias = bias_ref[g, :, 2 * QB - nk:]
        s, v = [], []
        for q_start, k_start, _ in specs:
            q = qs[g][rows(q_start, QB, r), :].astype(bf16)
            k = ks[g][rows(k_start, nk, r), :].astype(bf16)
            v.append(vs[g][rows(k_start, nk, r), :].astype(bf16))
            s.append(lax.dot_general(q, k, (((1,), (1,)), ((), ())), preferred_element_type=f32))
        s = jnp.stack(s) * scale + bias[None]
        m = jnp.max(s, axis=-1, keepdims=True)
        p = jnp.exp2(s - m)
        l = jnp.sum(p, axis=-1, keepdims=True)
        p = p.astype(bf16)
        out = []
        for t in range(len(specs)):
            pv = jnp.dot(p[t], v[t], preferred_element_type=f32)
            out.append((jnp.broadcast_to(m[t], pv.shape), jnp.broadcast_to(l[t], pv.shape), pv))
        return out

    for g in range(ATTN_GROUPS):
        r = DILATIONS[g]
        sub_len = SEQ // r
        first, rest = [], []
        for c in range(r):
            for n in range(sub_len // QB):
                q_start = c + n * QB * r
                dst = pl.ds(c * sub_len + n * QB, QB)
                if n == 0:
                    first.append(((q_start, q_start, QB), dst))
                else:
                    rest.append(((q_start, q_start - QB * r, 2 * QB), dst))
        for todo in (first, rest):
            for at in range(0, len(todo), ATTN_CHUNK):
                chunk = todo[at:at + ATTN_CHUNK]
                for (_, dst), (m, l, pv) in zip(chunk, tiles(g, [spec for spec, _ in chunk])):
                    ms[g][dst, :] = m
                    ls[g][dst, :] = l
                    accs[g][dst, :] = pv

    r_max = DILATIONS[-1]
    for c in range(r_max):
        sel = []
        for g in range(ATTN_GROUPS):
            r = DILATIONS[g]
            sel.append(rows((c % r) * (SEQ // r) + c // r, SEQ // r_max, r_max // r))
        m_g = [ms[g][sel[g], :] for g in range(ATTN_GROUPS)]
        m_all = jnp.maximum(jnp.maximum(m_g[0], m_g[1]), m_g[2])
        w_g = [jnp.exp2(m - m_all) for m in m_g]
        den = sum(w * ls[g][sel[g], :] for g, w in enumerate(w_g))
        num = sum(w * accs[g][sel[g], :] for g, w in enumerate(w_g))
        out_sc[rows(c, SEQ // r_max, r_max), :] = num / den
    o_ref[...] = out_sc[...].astype(o_ref.dtype)


def _attention(proj3, bias):
    hb = HEAD_DIM
    def col(kind, g):
        off = (kind * ATTN_GROUPS + g) * ATTN_HEADS
        return pl.BlockSpec((None, SEQ, hb), lambda b, h, off=off: (b, 0, off + h))
    in_specs = [col(kind, g) for kind in range(3) for g in range(ATTN_GROUPS)]
    in_specs.append(pl.BlockSpec((ATTN_GROUPS, None, ATTN_BLOCK, 2 * ATTN_BLOCK),
                                 lambda b, h: (0, h, 0, 0)))
    return pl.pallas_call(
        _attn_kernel,
        grid=(BATCH, ATTN_HEADS),
        in_specs=in_specs,
        out_specs=pl.BlockSpec((None, SEQ, hb), lambda b, h: (b, 0, h)),
        out_shape=jax.ShapeDtypeStruct((BATCH, SEQ, BRANCH_WIDTH), bf16),
        scratch_shapes=[pltpu.VMEM((SEQ, hb), f32)] * (3 * ATTN_GROUPS + 1),
        compiler_params=pltpu.CompilerParams(
            dimension_semantics=("arbitrary", "arbitrary"), vmem_limit_bytes=VMEM_LIMIT),
        name="attention",
    )(*([proj3] * 9), bias)


def _t5_bucket(dist):
    exact = NUM_BUCKETS // 2
    d = jnp.maximum(dist, 1).astype(f32)
    large = exact + (jnp.log(d / exact) / math.log(MAX_DISTANCE / exact)
                     * (NUM_BUCKETS - exact)).astype(i32)
    large = jnp.minimum(large, NUM_BUCKETS - 1)
    return jnp.where(dist < exact, dist, large)


def _attention_bias(rel_bias):
    qi = np.arange(ATTN_BLOCK, dtype=np.int32)[:, None]
    kj = np.arange(2 * ATTN_BLOCK, dtype=np.int32)[None, :]
    sub = qi + ATTN_BLOCK - kj
    table = rel_bias.reshape(NUM_BUCKETS, ATTN_GROUPS, ATTN_HEADS)
    out = []
    for g in range(ATTN_GROUPS):
        span = WINDOWS[g] // DILATIONS[g]
        band = (sub >= 0) & (sub <= span)
        bucket = _t5_bucket(jnp.asarray(DILATIONS[g] * np.clip(sub, 0, span), i32))
        onehot = jax.nn.one_hot(bucket.reshape(-1), NUM_BUCKETS, dtype=f32)
        b = jnp.dot(onehot, table[:, g], precision=lax.Precision.HIGHEST)
        b = jnp.where(band[:, :, None], b.reshape(ATTN_BLOCK, 2 * ATTN_BLOCK, ATTN_HEADS), -1e30)
        out.append(b.transpose(2, 0, 1))
    return jnp.stack(out, axis=0).astype(f32) * LOG2E


SGU_TM = 512


def _sgu_kernel(u_ref, v_ref, g_ref, b_ref, w_ref, bst_ref, o_ref):
    vn = _layer_norm(v_ref[...], g_ref[...], b_ref[...]).astype(bf16)
    ri = lax.broadcasted_iota(i32, (SGU_CHUNK, SGU_CHUNK), 0)
    ci = lax.broadcasted_iota(i32, (SGU_CHUNK, SGU_CHUNK), 1)
    causal = ci <= ri
    for g in range(SGU_GROUPS):
        wg = jnp.where(causal, w_ref[g], 0.0).astype(bf16)
        bias = bst_ref[:, g:g + 1]
        cs = slice(g * SGU_CHUNK, (g + 1) * SGU_CHUNK)
        for c in range(SGU_TM // SGU_CHUNK):
            rs = slice(c * SGU_CHUNK, (c + 1) * SGU_CHUNK)
            mixed = jnp.dot(wg, vn[rs, cs], preferred_element_type=f32) + bias
            o_ref[rs, cs] = (u_ref[rs, cs] * mixed).astype(o_ref.dtype)


def _sgu(proj, ln_g, ln_b, w_s, b_st):
    return pl.pallas_call(
        _sgu_kernel,
        grid=(T // SGU_TM,),
        in_specs=[pl.BlockSpec((SGU_TM, BRANCH_WIDTH), lambda i: (i, COL_U)),
                  pl.BlockSpec((SGU_TM, BRANCH_WIDTH), lambda i: (i, COL_V)),
                  pl.BlockSpec((1, BRANCH_WIDTH), lambda i: (0, 0)),
                  pl.BlockSpec((1, BRANCH_WIDTH), lambda i: (0, 0)),
                  pl.BlockSpec((SGU_GROUPS, SGU_CHUNK, SGU_CHUNK), lambda i: (0, 0, 0)),
                  pl.BlockSpec((SGU_CHUNK, SGU_GROUPS), lambda i: (0, 0))],
        out_specs=pl.BlockSpec((SGU_TM, BRANCH_WIDTH), lambda i: (i, 0)),
        out_shape=jax.ShapeDtypeStruct((T, BRANCH_WIDTH), bf16),
        compiler_params=pltpu.CompilerParams(
            dimension_semantics=("arbitrary",), vmem_limit_bytes=VMEM_LIMIT),
        name="sgu",
    )(proj, proj, ln_g, ln_b, w_s, b_st)


OL_TM = 512
GATE_TN = 1024


def _out_ln_router_kernel(ya_ref, ys_ref, ga0_ref, ga1_ref, gs0_ref, gs1_ref, x_ref,
                          wa_ref, ws_ref, wo_ref, g_ref, b_ref, wr_ref, br_ref,
                          h_ref, idx_ref, gate_ref, cnt_ref):
    i = pl.program_id(0)
    a = jnp.dot(ya_ref[...], wa_ref[...], preferred_element_type=f32)
    s = jnp.dot(ys_ref[...], ws_ref[...], preferred_element_type=f32)
    y = jnp.concatenate(
        [ga0_ref[...] * a[:, :GATE_TN] + gs0_ref[...] * s[:, :GATE_TN],
         ga1_ref[...] * a[:, GATE_TN:] + gs1_ref[...] * s[:, GATE_TN:]], axis=1).astype(bf16)
    mixed = jnp.dot(y, wo_ref[...], preferred_element_type=f32)
    h = _layer_norm(ALPHA * x_ref[...] + mixed, g_ref[...], b_ref[...])
    h_ref[...] = h
    h_hi = h.astype(bf16)
    h_lo = (h - h_hi.astype(f32)).astype(bf16)
    t = jnp.dot(h_hi, wr_ref[...], preferred_element_type=f32)
    u = jnp.dot(h_lo, wr_ref[:, :N_EXPERTS], preferred_element_type=f32)
    logits = t[:, :N_EXPERTS] + (t[:, N_EXPERTS:] + u) + br_ref[...]
    lane = lax.broadcasted_iota(i32, (OL_TM, N_EXPERTS), 1).astype(f32)
    lane_k = lax.broadcasted_iota(i32, (OL_TM, TOP_K), 1)
    idx_out = jnp.zeros((OL_TM, TOP_K), f32)
    val_out = jnp.zeros((OL_TM, TOP_K), f32)
    picked = jnp.zeros((OL_TM, N_EXPERTS), f32)
    work = logits
    for k in range(TOP_K):
        m = jnp.max(work, axis=-1, keepdims=True)
        idx = jnp.min(jnp.where(work == m, lane, float(N_EXPERTS)), axis=-1, keepdims=True)
        sel = lane == idx
        idx_out = jnp.where(lane_k == k, idx, idx_out)
        val_out = jnp.where(lane_k == k, m, val_out)
        picked = picked + sel.astype(f32)
        work = jnp.where(sel, -jnp.inf, work)
    e = jnp.exp(val_out - jnp.max(val_out, axis=-1, keepdims=True))
    gate_ref[...] = e / jnp.sum(e, axis=-1, keepdims=True)
    idx_ref[...] = idx_out.astype(i32)

    @pl.when(i == 0)
    def _():
        cnt_ref[...] = jnp.zeros_like(cnt_ref)

    cnt_ref[...] += jnp.sum(picked, axis=0, keepdims=True)


def _out_ln_router(ya, ys, proj, x2, wa, ws, wo, ln_g, ln_b, w_router, b_router):
    def resident(shape):
        return pl.BlockSpec(shape, lambda i: (0,) * len(shape), pipeline_mode=pl.Buffered(1))

    def gate_cols(block):
        return pl.BlockSpec((OL_TM, GATE_TN), lambda i, block=block: (i, block))

    return pl.pallas_call(
        _out_ln_router_kernel,
        grid=(T // OL_TM,),
        in_specs=[pl.BlockSpec((OL_TM, BRANCH_WIDTH), lambda i: (i, 0)),
                  pl.BlockSpec((OL_TM, BRANCH_WIDTH), lambda i: (i, 0)),
                  gate_cols(COL_GA), gate_cols(COL_GA + 1),
                  gate_cols(COL_GS), gate_cols(COL_GS + 1),
                  pl.BlockSpec((OL_TM, D_MODEL), lambda i: (i, 0)),
                  resident((BRANCH_WIDTH, D_MODEL)),
                  resident((BRANCH_WIDTH, D_MODEL)),
                  resident((D_MODEL, D_MODEL)),
                  resident((1, D_MODEL)),
                  resident((1, D_MODEL)),
                  resident((D_MODEL, 2 * N_EXPERTS)),
                  resident((1, N_EXPERTS))],
        out_specs=[pl.BlockSpec((OL_TM, D_MODEL), lambda i: (i, 0)),
                   pl.BlockSpec((OL_TM, TOP_K), lambda i: (i, 0)),
                   pl.BlockSpec((OL_TM, TOP_K), lambda i: (i, 0)),
                   pl.BlockSpec((1, N_EXPERTS), lambda i: (0, 0))],
        out_shape=[jax.ShapeDtypeStruct((T, D_MODEL), f32),
                   jax.ShapeDtypeStruct((T, TOP_K), i32),
                   jax.ShapeDtypeStruct((T, TOP_K), f32),
                   jax.ShapeDtypeStruct((1, N_EXPERTS), f32)],
        compiler_params=pltpu.CompilerParams(
            dimension_semantics=("arbitrary",), vmem_limit_bytes=VMEM_LIMIT_MIX),
        name="out_ln_router",
    )(ya, ys, proj, proj, proj, proj, x2, wa, ws, wo, ln_g, ln_b, w_router, b_router)


RK_TM = 512


def _rank_kernel(idx_ref, pstart_ref, dest_ref, carry_sc):
    i = pl.program_id(0)

    @pl.when(i == 0)
    def _():
        carry_sc[...] = jnp.zeros_like(carry_sc)

    idx = idx_ref[...]
    lane = lax.broadcasted_iota(i32, (RK_TM, N_EXPERTS), 1)
    picked = jnp.zeros((RK_TM, N_EXPERTS), f32)
    for k in range(TOP_K):
        picked = picked + (lane == idx[:, k:k + 1]).astype(f32)
    ri = lax.broadcasted_iota(i32, (RK_TM, RK_TM), 0)
    ci = lax.broadcasted_iota(i32, (RK_TM, RK_TM), 1)
    lower = (ci < ri).astype(bf16)
    before = jnp.dot(lower, picked.astype(bf16), preferred_element_type=f32)
    slot = before + carry_sc[...] + pstart_ref[...]
    lane_k = lax.broadcasted_iota(i32, (RK_TM, TOP_K), 1)
    dest = jnp.zeros((RK_TM, TOP_K), f32)
    for k in range(TOP_K):
        d = jnp.sum(jnp.where(lane == idx[:, k:k + 1], slot, 0.0), axis=-1, keepdims=True)
        dest = jnp.where(lane_k == k, d, dest)
    dest_ref[...] = dest.astype(i32)
    carry_sc[...] += jnp.sum(picked, axis=0, keepdims=True)


def _rank(idx, pstart):
    return pl.pallas_call(
        _rank_kernel,
        grid=(T // RK_TM,),
        in_specs=[pl.BlockSpec((RK_TM, TOP_K), lambda i: (i, 0)),
                  pl.BlockSpec((1, N_EXPERTS), lambda i: (0, 0))],
        out_specs=pl.BlockSpec((RK_TM, TOP_K), lambda i: (i, 0)),
        out_shape=jax.ShapeDtypeStruct((T, TOP_K), i32),
        scratch_shapes=[pltpu.VMEM((1, N_EXPERTS), f32)],
        compiler_params=pltpu.CompilerParams(dimension_semantics=("arbitrary",)),
        name="rank",
    )(idx, pstart)


DP_TM = 256
HALF = D_MODEL // 2


def _pack_bf16_pairs(x):
    bits = lax.bitcast_convert_type(x.astype(bf16).astype(f32), i32)
    return bits[:, HALF:] | lax.shift_right_logical(bits[:, :HALF], 16)


def _unpack_bf16_pairs(w):
    lo = lax.bitcast_convert_type(lax.shift_left(w, 16), f32)
    hi = lax.bitcast_convert_type(w & jnp.int32(-65536), f32)
    return jnp.concatenate([lo, hi], axis=1).astype(bf16)


ZROWS = 256
SUBLANES = 8


def _zero_fill(pstart_ref, plen_ref, meta_ref, xs_hbm, zbuf, zsem, wait):
    def go(cp):
        if wait:
            cp.wait()
        else:
            cp.start()

    def expert(e, carry):
        start, n = pstart_ref[e], plen_ref[e]
        head = n & (SUBLANES - 1)
        for t in range(SUBLANES - 1):
            @pl.when(t < head)
            def _():
                go(pltpu.make_async_copy(zbuf.at[pl.ds(0, 1)], xs_hbm.at[pl.ds(start + t, 1)],
                                         zsem.at[0]))
        b = SUBLANES
        while b <= ZROWS:
            @pl.when((n & b) != 0)
            def _(b=b):
                off = pl.multiple_of(start + head + (n & ~(2 * b - 1)), SUBLANES)
                go(pltpu.make_async_copy(zbuf.at[pl.ds(0, b)], xs_hbm.at[pl.ds(off, b)],
                                         zsem.at[0]))
            b *= 2
        return carry

    lax.fori_loop(0, N_EXPERTS, expert, 0)

    def unowned(blk, carry):
        for part in range(BM // ZROWS):
            off = pl.multiple_of(blk * BM + part * ZROWS, ZROWS)
            go(pltpu.make_async_copy(zbuf, xs_hbm.at[pl.ds(off, ZROWS)], zsem.at[0]))
        return carry

    lax.fori_loop(meta_ref[0], NB, unowned, 0)


def _dispatch_kernel(pstart_ref, plen_ref, meta_ref, dest_ref, h_ref, xs_hbm,
                     pack_a, pack_b, zbuf, sem, zsem):
    i, n = pl.program_id(0), pl.num_programs(0)
    first = i == 0

    @pl.when(first)
    def _():
        zbuf[...] = jnp.zeros_like(zbuf)
        _zero_fill(pstart_ref, plen_ref, meta_ref, xs_hbm, zbuf, zsem, wait=False)

    def wait_rows(pack, s):
        for _ in range(TOP_K):
            pltpu.make_async_copy(pack, xs_hbm.at[pl.ds(0, DP_TM)], sem.at[s]).wait()

    def step(pack, s, other, s_other):
        pack[...] = _pack_bf16_pairs(h_ref[...])

        def issue(t, carry):
            for k in range(TOP_K):
                pltpu.make_async_copy(pack.at[pl.ds(t, 1)],
                                      xs_hbm.at[pl.ds(dest_ref[t * TOP_K + k], 1)],
                                      sem.at[s]).start(priority=k % 2)
            return carry

        lax.fori_loop(0, DP_TM, issue, 0)

        @pl.when(i > 0)
        def _():
            wait_rows(other, s_other)

        @pl.when(i == n - 1)
        def _():
            wait_rows(pack, s)

    @pl.when(i % 2 == 0)
    def _():
        step(pack_a, 0, pack_b, 1)

    @pl.when(i % 2 == 1)
    def _():
        step(pack_b, 1, pack_a, 0)

    @pl.when(first)
    def _():
        _zero_fill(pstart_ref, plen_ref, meta_ref, xs_hbm, zbuf, zsem, wait=True)


def _dispatch(h, dest_flat, pad_start, pad_len, meta):
    grid_spec = pltpu.PrefetchScalarGridSpec(
        num_scalar_prefetch=3,
        grid=(T // DP_TM,),
        in_specs=[pl.BlockSpec((DP_TM * TOP_K,), lambda i, *_: (i,), memory_space=pltpu.SMEM),
                  pl.BlockSpec((DP_TM, D_MODEL), lambda i, *_: (i, 0))],
        out_specs=pl.BlockSpec(memory_space=pl.ANY),
        scratch_shapes=[pltpu.VMEM((DP_TM, HALF), i32), pltpu.VMEM((DP_TM, HALF), i32),
                        pltpu.VMEM((ZROWS, HALF), i32),
                        pltpu.SemaphoreType.DMA((2,)), pltpu.SemaphoreType.DMA((1,))],
    )
    return pl.pallas_call(
        _dispatch_kernel,
        grid_spec=grid_spec,
        out_shape=jax.ShapeDtypeStruct((CAP, HALF), i32),
        compiler_params=pltpu.CompilerParams(dimension_semantics=("arbitrary",)),
        name="dispatch",
    )(pad_start, pad_len, meta, dest_flat, h)


GU_TN = 1024
DN_TN = 2048


def _stream_expert_weights(bs_ref, se_ref, meta_ref, w_hbm, col_offsets, tn, wbuf, sem, caches):
    j, i, nj = pl.program_id(0), pl.program_id(1), pl.num_programs(0)
    n_seg = jnp.maximum(meta_ref[1], 1)
    seg = bs_ref[i]
    first_block = (i == 0) | (seg != bs_ref[jnp.maximum(i - 1, 0)])

    def copies(s, jj, slot):
        expert = se_ref[s]
        out = []
        for c, off in enumerate(col_offsets):
            start = off + jj * tn
            if not isinstance(start, int):
                start = pl.multiple_of(start, tn)
            out.append(pltpu.make_async_copy(w_hbm.at[expert, :, pl.ds(start, tn)],
                                             wbuf.at[slot, c], sem.at[slot]))
        return out

    @pl.when(first_block)
    def _():
        seq = j * n_seg + seg
        slot = seq % 2

        @pl.when(seq == 0)
        def _():
            for cp in copies(0, 0, 0):
                cp.start()

        more = seg + 1 < n_seg

        @pl.when(more | (j + 1 < nj))
        def _():
            for cp in copies(jnp.where(more, seg + 1, 0), jnp.where(more, j, j + 1), 1 - slot):
                cp.start()

        for cp in copies(seg, j, slot):
            cp.wait()
        for c, cache in enumerate(caches):
            cache[...] = wbuf[slot, c].astype(bf16)


def _expert_gu_kernel(be_ref, bs_ref, se_ref, meta_ref, x_ref, w_hbm, bg_ref, bl_ref, o_ref,
                      wbuf, wg_sc, wl_sc, sem):
    del be_ref
    i = pl.program_id(1)
    last = jnp.maximum(meta_ref[0], 1) - 1

    @pl.when(i <= last)
    def _():
        _stream_expert_weights(bs_ref, se_ref, meta_ref, w_hbm, (0, D_MODEL), GU_TN,
                               wbuf, sem, (wg_sc, wl_sc))
        x = _unpack_bf16_pairs(x_ref[...])
        glu = jnp.dot(x, wg_sc[...], preferred_element_type=f32) + bg_ref[...]
        lin = jnp.dot(x, wl_sc[...], preferred_element_type=f32) + bl_ref[...]
        glu = jnp.minimum(glu, SWIGLU_LIMIT)
        lin = jnp.clip(lin, -SWIGLU_LIMIT, SWIGLU_LIMIT)
        o_ref[...] = (glu * _sigmoid(SWIGLU_ALPHA * glu) * (lin + 1.0)).astype(o_ref.dtype)

    @pl.when(i > last)
    def _():
        o_ref[...] = jnp.zeros_like(o_ref)


def _used_block(j, i, be, bs, se, meta):
    return jnp.minimum(i, jnp.maximum(meta[0], 1) - 1)


def _expert_gu(layout, xs, w_gu, b_gu):
    nj = D_MODEL // GU_TN
    blk = _used_block
    grid_spec = pltpu.PrefetchScalarGridSpec(
        num_scalar_prefetch=4,
        grid=(nj, NB),
        in_specs=[
            pl.BlockSpec((BM, HALF), lambda *a: (blk(*a), 0)),
            pl.BlockSpec(memory_space=pl.ANY),
            pl.BlockSpec((None, 1, GU_TN), lambda *a: (a[2][blk(*a)], 0, a[0])),
            pl.BlockSpec((None, 1, GU_TN), lambda *a: (a[2][blk(*a)], 0, nj + a[0])),
        ],
        out_specs=pl.BlockSpec((BM, GU_TN), lambda j, i, *_: (i, j)),
        scratch_shapes=[pltpu.VMEM((2, 2, D_MODEL, GU_TN), f32),
                        pltpu.VMEM((D_MODEL, GU_TN), bf16), pltpu.VMEM((D_MODEL, GU_TN), bf16),
                        pltpu.SemaphoreType.DMA((2,))],
    )
    return pl.pallas_call(
        _expert_gu_kernel,
        grid_spec=grid_spec,
        out_shape=jax.ShapeDtypeStruct((CAP, D_MODEL), bf16),
        compiler_params=pltpu.CompilerParams(
            dimension_semantics=("arbitrary", "arbitrary"), vmem_limit_bytes=VMEM_LIMIT),
        name="expert_gu",
    )(*layout, xs, w_gu, b_gu, b_gu)


def _expert_down_kernel(be_ref, bs_ref, se_ref, meta_ref, a_ref, w_hbm, b_ref, o_ref,
                        wbuf, w_sc, sem):
    del be_ref
    i = pl.program_id(1)
    last = jnp.maximum(meta_ref[0], 1) - 1

    @pl.when(i <= last)
    def _():
        _stream_expert_weights(bs_ref, se_ref, meta_ref, w_hbm, (0,), DN_TN, wbuf, sem, (w_sc,))
        o_ref[...] = jnp.dot(a_ref[...], w_sc[...], preferred_element_type=f32) + b_ref[...]

    @pl.when(i > last)
    def _():
        o_ref[...] = jnp.zeros_like(o_ref)


def _expert_down(layout, act, w_down, b_down):
    blk = _used_block
    grid_spec = pltpu.PrefetchScalarGridSpec(
        num_scalar_prefetch=4,
        grid=(D_MODEL // DN_TN, NB),
        in_specs=[
            pl.BlockSpec((BM, D_MODEL), lambda *a: (blk(*a), 0)),
            pl.BlockSpec(memory_space=pl.ANY),
            pl.BlockSpec((None, 1, DN_TN), lambda *a: (a[2][blk(*a)], 0, a[0])),
        ],
        out_specs=pl.BlockSpec((BM, DN_TN), lambda j, i, *_: (i, j)),
        scratch_shapes=[pltpu.VMEM((2, 1, D_MODEL, DN_TN), f32),
                        pltpu.VMEM((D_MODEL, DN_TN), bf16),
                        pltpu.SemaphoreType.DMA((2,))],
    )
    return pl.pallas_call(
        _expert_down_kernel,
        grid_spec=grid_spec,
        out_shape=jax.ShapeDtypeStruct((CAP, D_MODEL), f32),
        compiler_params=pltpu.CompilerParams(
            dimension_semantics=("arbitrary", "arbitrary"), vmem_limit_bytes=VMEM_LIMIT),
        name="expert_down",
    )(*layout, act, w_down, b_down)


CB_TM = 128
CB_SLABS = 4


def _combine_kernel(dcur_ref, dnext_ref, y_hbm, gate_ref, h_ref, g_ref, b_ref, o_ref,
                    buf_a, buf_b, sem):
    i, n = pl.program_id(0), pl.num_programs(0)

    def start_rows(d_ref, buf, s, t):
        for k in range(TOP_K):
            pltpu.make_async_copy(y_hbm.at[pl.ds(d_ref[t * TOP_K + k], 1)],
                                  buf.at[k, pl.ds(t, 1)], sem.at[s]).start(priority=k % 2)

    def wait_tile(buf, s):
        for k in range(TOP_K):
            pltpu.make_async_copy(y_hbm.at[pl.ds(0, CB_TM)], buf.at[k], sem.at[s]).wait()

    @pl.when(i == 0)
    def _():
        def prime(t, carry):
            start_rows(dcur_ref, buf_a, 0, t)
            return carry
        lax.fori_loop(0, CB_TM, prime, 0)

    def step(cur, s_cur, nxt, s_nxt):
        wait_tile(cur, s_cur)

        def slabs(gi, carry):
            for part in range(CB_SLABS):
                t0 = pl.multiple_of((gi * CB_SLABS + part) * SUBLANES, SUBLANES)
                for tt in range(SUBLANES):
                    start_rows(dnext_ref, nxt, s_nxt, t0 + tt)
                rows = pl.ds(t0, SUBLANES)
                gate = gate_ref[rows, :]
                ffn = gate[:, 0:1] * cur[0, rows, :]
                for k in range(1, TOP_K):
                    ffn = ffn + gate[:, k:k + 1] * cur[k, rows, :]
                o_ref[rows, :] = _layer_norm(ALPHA * h_ref[rows, :] + ffn, g_ref[...], b_ref[...])
            return carry

        lax.fori_loop(0, CB_TM // (SUBLANES * CB_SLABS), slabs, 0)

        @pl.when(i == n - 1)
        def _():
            wait_tile(nxt, s_nxt)

    @pl.when(i % 2 == 0)
    def _():
        step(buf_a, 0, buf_b, 1)

    @pl.when(i % 2 == 1)
    def _():
        step(buf_b, 1, buf_a, 0)


def _combine(y, dest_flat, gate, h, ln_g, ln_b):
    n_tiles = T // CB_TM
    return pl.pallas_call(
        _combine_kernel,
        grid=(n_tiles,),
        in_specs=[pl.BlockSpec((CB_TM * TOP_K,), lambda i: (i,), memory_space=pltpu.SMEM),
                  pl.BlockSpec((CB_TM * TOP_K,), lambda i: (jnp.minimum(i + 1, n_tiles - 1),),
                               memory_space=pltpu.SMEM),
                  pl.BlockSpec(memory_space=pl.ANY),
                  pl.BlockSpec((CB_TM, TOP_K), lambda i: (i, 0)),
                  pl.BlockSpec((CB_TM, D_MODEL), lambda i: (i, 0)),
                  pl.BlockSpec((1, D_MODEL), lambda i: (0, 0)),
                  pl.BlockSpec((1, D_MODEL), lambda i: (0, 0))],
        out_specs=pl.BlockSpec((CB_TM, D_MODEL), lambda i: (i, 0)),
        out_shape=jax.ShapeDtypeStruct((T, D_MODEL), f32),
        scratch_shapes=[pltpu.VMEM((TOP_K, CB_TM, D_MODEL), f32),
                        pltpu.VMEM((TOP_K, CB_TM, D_MODEL), f32),
                        pltpu.SemaphoreType.DMA((2,))],
        compiler_params=pltpu.CompilerParams(dimension_semantics=("arbitrary",),
                                             vmem_limit_bytes=VMEM_LIMIT),
        name="combine",
    )(dest_flat, dest_flat, y, gate, h, ln_g, ln_b)


def _expert_layout(counts):
    nblk = (counts + BM - 1) // BM
    bend = jnp.cumsum(nblk)
    pstart = (bend - nblk) * BM
    before = bend[None, :] <= jnp.arange(NB, dtype=i32)[:, None]
    block_e = jnp.minimum(jnp.sum(before.astype(i32), axis=1), N_EXPERTS - 1)
    has = nblk > 0
    block_seg = jnp.sum((before & has[None, :]).astype(i32), axis=1)
    seg_of_e = jnp.cumsum(has.astype(i32)) - 1
    ids = jnp.arange(N_EXPERTS, dtype=i32)
    seg_expert = jnp.sum(jnp.where(has[None, :] & (seg_of_e[None, :] == ids[:, None]),
                                   ids[None, :], 0), axis=1)
    meta = jnp.stack([bend[-1], jnp.sum(has.astype(i32))]).astype(i32)
    padding = ((pstart + counts).astype(i32), (nblk * BM - counts).astype(i32))
    return pstart, padding, (block_e, block_seg.astype(i32), seg_expert.astype(i32), meta)


def kernel(x, w_in, rel_bias, sgu_ln_g, sgu_ln_b, sgu_w, sgu_b, w_branch, w_out, ln1_g, ln1_b,
           w_router, b_router, w_gu, b_gu, w_down, b_down, ln2_g, ln2_b):
    x2 = x.reshape(T, D_MODEL)
    proj = _inproj(x2, w_in[0].astype(bf16))
    y_attn = _attention(proj.reshape(BATCH, SEQ, IN_COLS), _attention_bias(rel_bias))
    y_sgu = _sgu(proj, sgu_ln_g, sgu_ln_b, sgu_w[0], sgu_b[0].T)
    wr_hi = w_router[0].astype(bf16)
    wr_lo = (w_router[0] - wr_hi.astype(f32)).astype(bf16)
    h, idx, gate, counts = _out_ln_router(
        y_attn.reshape(T, BRANCH_WIDTH), y_sgu, proj, x2,
        w_branch[0, 0].astype(bf16), w_branch[0, 1].astype(bf16), w_out[0].astype(bf16),
        ln1_g, ln1_b, jnp.concatenate([wr_hi, wr_lo], axis=1), b_router)
    pstart, padding, layout = _expert_layout(counts[0].astype(i32))
    dest = _rank(idx, pstart.astype(f32)[None, :]).reshape(T * TOP_K)
    xs = _dispatch(h, dest, *padding, layout[-1])
    act = _expert_gu(layout, xs, w_gu.reshape(N_EXPERTS, D_MODEL, 2 * D_MODEL),
                     b_gu.reshape(N_EXPERTS, 1, 2 * D_MODEL))
    ye = _expert_down(layout, act, w_down.reshape(N_EXPERTS, D_MODEL, D_MODEL),
                      b_down.reshape(N_EXPERTS, 1, D_MODEL))
    out = _combine(ye, dest, gate, h, ln2_g, ln2_b)
    return out.reshape(BATCH, SEQ, D_MODEL)
```

```python
import functools
import math

import numpy as np
import jax
import jax.numpy as jnp
from jax import lax
from jax.experimental import pallas as pl
from jax.experimental.pallas import tpu as pltpu

D_MODEL = 2048
BATCH = 8
SEQ = 2048
T = BATCH * SEQ
BRANCH_WIDTH = D_MODEL // 2
HEAD_DIM = 128
ATTN_GROUPS = 3
WINDOWS = (128, 512, 2048)
DILATIONS = (1, 4, 16)
ATTN_HEADS = BRANCH_WIDTH // HEAD_DIM
QKV_WIDTH = ATTN_GROUPS * ATTN_HEADS * HEAD_DIM
ATTN_BLOCK = 128
NUM_BUCKETS = 32
MAX_DISTANCE = 2048
SGU_CHUNK = 128
SGU_GROUPS = BRANCH_WIDTH // SGU_CHUNK
IN_COLS = 3 * QKV_WIDTH + 2 * BRANCH_WIDTH + 2 * D_MODEL
N_EXPERTS = 32
TOP_K = 4
SWIGLU_ALPHA = 1.702
SWIGLU_LIMIT = 7.0
LN_EPS = 1e-5
ALPHA = 2.0 ** 0.25
LOG2E = math.log2(math.e)
ATTN_CHUNK = 16

COL_U = 3 * QKV_WIDTH // 1024
COL_V = COL_U + 1
COL_GA = COL_U + 2
COL_GS = COL_GA + 2

BM = 512
NB = (T * TOP_K + N_EXPERTS * (BM - 1) + BM - 1) // BM
CAP = NB * BM

VMEM_LIMIT = 56 * 1024 * 1024
VMEM_LIMIT_MIX = 61 * 1024 * 1024

f32 = jnp.float32
bf16 = jnp.bfloat16
i32 = jnp.int32


def _sigmoid(x):
    return 0.5 * jnp.tanh(0.5 * x) + 0.5


def _layer_norm(x, g, b):
    mu = jnp.mean(x, axis=-1, keepdims=True)
    xc = x - mu
    var = jnp.mean(xc * xc, axis=-1, keepdims=True)
    return xc * lax.rsqrt(var + LN_EPS) * g + b


IP_TM = 1024
IP_TN = 1536
IP_CHUNK = 256


def _inproj_epilogue(col):
    if col < COL_U * 1024:
        return "plain", lambda a: a
    if col < COL_GA * 1024:
        return "gelu", jax.nn.gelu
    return "sigmoid", _sigmoid


def _inproj_kernel(x_ref, w_ref, o_ref, xb_sc):
    j = pl.program_id(1)

    @pl.when(j == 0)
    def _():
        xb_sc[...] = x_ref[...].astype(bf16)

    patterns = {}
    for jj in range(IN_COLS // IP_TN):
        names = tuple(_inproj_epilogue(jj * IP_TN + c)[0] for c in range(0, IP_TN, IP_CHUNK))
        patterns.setdefault(names, []).append(jj)

    for tiles in patterns.values():
        cond = functools.reduce(lambda a, b: a | b, [j == jj for jj in tiles])

        @pl.when(cond)
        def _(first=tiles[0]):
            for c in range(0, IP_TN, IP_CHUNK):
                acc = jnp.dot(xb_sc[...], w_ref[:, c:c + IP_CHUNK], preferred_element_type=f32)
                o_ref[:, c:c + IP_CHUNK] = _inproj_epilogue(first * IP_TN + c)[1](acc)


def _inproj(x2, wb):
    return pl.pallas_call(
        _inproj_kernel,
        grid=(T // IP_TM, IN_COLS // IP_TN),
        in_specs=[pl.BlockSpec((IP_TM, D_MODEL), lambda i, j: (i, 0)),
                  pl.BlockSpec((D_MODEL, IP_TN), lambda i, j: (0, j))],
        out_specs=pl.BlockSpec((IP_TM, IP_TN), lambda i, j: (i, j)),
        out_shape=jax.ShapeDtypeStruct((T, IN_COLS), f32),
        scratch_shapes=[pltpu.VMEM((IP_TM, D_MODEL), bf16)],
        compiler_params=pltpu.CompilerParams(
            dimension_semantics=("arbitrary", "arbitrary"), vmem_limit_bytes=VMEM_LIMIT),
        name="inproj",
    )(x2, wb)


def _attn_kernel(q0, q1, q2, k0, k1, k2, v0, v1, v2, bias_ref, o_ref,
                 m0, m1, m2, l0, l1, l2, a0, a1, a2, out_sc):
    qs, ks, vs = (q0, q1, q2), (k0, k1, k2), (v0, v1, v2)
    ms, ls, accs = (m0, m1, m2), (l0, l1, l2), (a0, a1, a2)
    scale = HEAD_DIM ** -0.5 * LOG2E
    QB = ATTN_BLOCK

    def rows(start, n, r):
        return pl.ds(start, n, stride=r) if r > 1 else pl.ds(start, n)

    def tiles(g, specs):
        r = DILATIONS[g]
        nk = specs[0][2]
        bias = bias_ref[g, :, 2 * QB - nk:]
        s, v = [], []
        for q_start, k_start, _ in specs:
            q = qs[g][rows(q_start, QB, r), :].astype(bf16)
            k = ks[g][rows(k_start, nk, r), :].astype(bf16)
            v.append(vs[g][rows(k_start, nk, r), :].astype(bf16))
            s.append(lax.dot_general(q, k, (((1,), (1,)), ((), ())), preferred_element_type=f32))
        s = jnp.stack(s) * scale + bias[None]
        m = jnp.max(s, axis=-1, keepdims=True)
        p = jnp.exp2(s - m)
        l = jnp.sum(p, axis=-1, keepdims=True)
        p = p.astype(bf16)
        out = []
        for t in range(len(specs)):
            pv = jnp.dot(p[t], v[t], preferred_element_type=f32)
            out.append((jnp.broadcast_to(m[t], pv.shape), jnp.broadcast_to(l[t], pv.shape), pv))
        return out

    for g in range(ATTN_GROUPS):
        r = DILATIONS[g]
        sub_len = SEQ // r
        first, rest = [], []
        for c in range(r):
            for n in range(sub_len // QB):
                q_start = c + n * QB * r
                dst = pl.ds(c * sub_len + n * QB, QB)
                if n == 0:
                    first.append(((q_start, q_start, QB), dst))
                else:
                    rest.append(((q_start, q_start - QB * r, 2 * QB), dst))
        for todo in (first, rest):
            for at in range(0, len(todo), ATTN_CHUNK):
                chunk = todo[at:at + ATTN_CHUNK]
                for (_, dst), (m, l, pv) in zip(chunk, tiles(g, [spec for spec, _ in chunk])):
                    ms[g][dst, :] = m
                    ls[g][dst, :] = l
                    accs[g][dst, :] = pv

    r_max = DILATIONS[-1]
    for c in range(r_max):
        sel = []
        for g in range(ATTN_GROUPS):
            r = DILATIONS[g]
            sel.append(rows((c % r) * (SEQ // r) + c // r, SEQ // r_max, r_max // r))
        m_g = [ms[g][sel[g], :] for g in range(ATTN_GROUPS)]
        m_all = jnp.maximum(jnp.maximum(m_g[0], m_g[1]), m_g[2])
        w_g = [jnp.exp2(m - m_all) for m in m_g]
        den = sum(w * ls[g][sel[g], :] for g, w in enumerate(w_g))
        num = sum(w * accs[g][sel[g], :] for g, w in enumerate(w_g))
        out_sc[rows(c, SEQ // r_max, r_max), :] = num / den
    o_ref[...] = out_sc[...].astype(o_ref.dtype)


def _attention(proj3, bias):
    hb = HEAD_DIM
    def col(kind, g):
        off = (kind * ATTN_GROUPS + g) * ATTN_HEADS
        return pl.BlockSpec((None, SEQ, hb), lambda b, h, off=off: (b, 0, off + h))
    in_specs = [col(kind, g) for kind in range(3) for g in range(ATTN_GROUPS)]
    in_specs.append(pl.BlockSpec((ATTN_GROUPS, None, ATTN_BLOCK, 2 * ATTN_BLOCK),
                                 lambda b, h: (0, h, 0, 0)))
    return pl.pallas_call(
        _attn_kernel,
        grid=(BATCH, ATTN_HEADS),
        in_specs=in_specs,
        out_specs=pl.BlockSpec((None, SEQ, hb), lambda b, h: (b, 0, h)),
        out_shape=jax.ShapeDtypeStruct((BATCH, SEQ, BRANCH_WIDTH), bf16),
        scratch_shapes=[pltpu.VMEM((SEQ, hb), f32)] * (3 * ATTN_GROUPS + 1),
        compiler_params=pltpu.CompilerParams(
            dimension_semantics=("arbitrary", "arbitrary"), vmem_limit_bytes=VMEM_LIMIT),
        name="attention",
    )(*([proj3] * 9), bias)


def _t5_bucket(dist):
    exact = NUM_BUCKETS // 2
    d = jnp.maximum(dist, 1).astype(f32)
    large = exact + (jnp.log(d / exact) / math.log(MAX_DISTANCE / exact)
                     * (NUM_BUCKETS - exact)).astype(i32)
    large = jnp.minimum(large, NUM_BUCKETS - 1)
    return jnp.where(dist < exact, dist, large)


def _attention_bias(rel_bias):
    qi = np.arange(ATTN_BLOCK, dtype=np.int32)[:, None]
    kj = np.arange(2 * ATTN_BLOCK, dtype=np.int32)[None, :]
    sub = qi + ATTN_BLOCK - kj
    table = rel_bias.reshape(NUM_BUCKETS, ATTN_GROUPS, ATTN_HEADS)
    out = []
    for g in range(ATTN_GROUPS):
        span = WINDOWS[g] // DILATIONS[g]
        band = (sub >= 0) & (sub <= span)
        bucket = _t5_bucket(jnp.asarray(DILATIONS[g] * np.clip(sub, 0, span), i32))
        onehot = jax.nn.one_hot(bucket.reshape(-1), NUM_BUCKETS, dtype=f32)
        b = jnp.dot(onehot, table[:, g], precision=lax.Precision.HIGHEST)
        b = jnp.where(band[:, :, None], b.reshape(ATTN_BLOCK, 2 * ATTN_BLOCK, ATTN_HEADS), -1e30)
        out.append(b.transpose(2, 0, 1))
    return jnp.stack(out, axis=0).astype(f32) * LOG2E


SGU_TM = 512


def _sgu_kernel(u_ref, v_ref, g_ref, b_ref, w_ref, bst_ref, o_ref):
    vn = _layer_norm(v_ref[...], g_ref[...], b_ref[...]).astype(bf16)
    ri = lax.broadcasted_iota(i32, (SGU_CHUNK, SGU_CHUNK), 0)
    ci = lax.broadcasted_iota(i32, (SGU_CHUNK, SGU_CHUNK), 1)
    causal = ci <= ri
    for g in range(SGU_GROUPS):
        wg = jnp.where(causal, w_ref[g], 0.0).astype(bf16)
        bias = bst_ref[:, g:g + 1]
        cs = slice(g * SGU_CHUNK, (g + 1) * SGU_CHUNK)
        for c in range(SGU_TM // SGU_CHUNK):
            rs = slice(c * SGU_CHUNK, (c + 1) * SGU_CHUNK)
            mixed = jnp.dot(wg, vn[rs, cs], preferred_element_type=f32) + bias
            o_ref[rs, cs] = (u_ref[rs, cs] * mixed).astype(o_ref.dtype)


def _sgu(proj, ln_g, ln_b, w_s, b_st):
    return pl.pallas_call(
        _sgu_kernel,
        grid=(T // SGU_TM,),
        in_specs=[pl.BlockSpec((SGU_TM, BRANCH_WIDTH), lambda i: (i, COL_U)),
                  pl.BlockSpec((SGU_TM, BRANCH_WIDTH), lambda i: (i, COL_V)),
                  pl.BlockSpec((1, BRANCH_WIDTH), lambda i: (0, 0)),
                  pl.BlockSpec((1, BRANCH_WIDTH), lambda i: (0, 0)),
                  pl.BlockSpec((SGU_GROUPS, SGU_CHUNK, SGU_CHUNK), lambda i: (0, 0, 0)),
                  pl.BlockSpec((SGU_CHUNK, SGU_GROUPS), lambda i: (0, 0))],
        out_specs=pl.BlockSpec((SGU_TM, BRANCH_WIDTH), lambda i: (i, 0)),
        out_shape=jax.ShapeDtypeStruct((T, BRANCH_WIDTH), bf16),
        compiler_params=pltpu.CompilerParams(
            dimension_semantics=("arbitrary",), vmem_limit_bytes=VMEM_LIMIT),
        name="sgu",
    )(proj, proj, ln_g, ln_b, w_s, b_st)


OL_TM = 512
GATE_TN = 1024


def _out_ln_router_kernel(ya_ref, ys_ref, ga0_ref, ga1_ref, gs0_ref, gs1_ref, x_ref,
                          wa_ref, ws_ref, wo_ref, g_ref, b_ref, wr_ref, br_ref,
                          h_ref, idx_ref, gate_ref, cnt_ref):
    i = pl.program_id(0)
    a = jnp.dot(ya_ref[...], wa_ref[...], preferred_element_type=f32)
    s = jnp.dot(ys_ref[...], ws_ref[...], preferred_element_type=f32)
    y = jnp.concatenate(
        [ga0_ref[...] * a[:, :GATE_TN] + gs0_ref[...] * s[:, :GATE_TN],
         ga1_ref[...] * a[:, GATE_TN:] + gs1_ref[...] * s[:, GATE_TN:]], axis=1).astype(bf16)
    mixed = jnp.dot(y, wo_ref[...], preferred_element_type=f32)
    h = _layer_norm(ALPHA * x_ref[...] + mixed, g_ref[...], b_ref[...])
    h_ref[...] = h
    h_hi = h.astype(bf16)
    h_lo = (h - h_hi.astype(f32)).astype(bf16)
    t = jnp.dot(h_hi, wr_ref[...], preferred_element_type=f32)
    u = jnp.dot(h_lo, wr_ref[:, :N_EXPERTS], preferred_element_type=f32)
    logits = t[:, :N_EXPERTS] + (t[:, N_EXPERTS:] + u) + br_ref[...]
    lane = lax.broadcasted_iota(i32, (OL_TM, N_EXPERTS), 1).astype(f32)
    lane_k = lax.broadcasted_iota(i32, (OL_TM, TOP_K), 1)
    idx_out = jnp.zeros((OL_TM, TOP_K), f32)
    val_out = jnp.zeros((OL_TM, TOP_K), f32)
    picked = jnp.zeros((OL_TM, N_EXPERTS), f32)
    work = logits
    for k in range(TOP_K):
        m = jnp.max(work, axis=-1, keepdims=True)
        idx = jnp.min(jnp.where(work == m, lane, float(N_EXPERTS)), axis=-1, keepdims=True)
        sel = lane == idx
        idx_out = jnp.where(lane_k == k, idx, idx_out)
        val_out = jnp.where(lane_k == k, m, val_out)
        picked = picked + sel.astype(f32)
        work = jnp.where(sel, -jnp.inf, work)
    e = jnp.exp(val_out - jnp.max(val_out, axis=-1, keepdims=True))
    gate_ref[...] = e / jnp.sum(e, axis=-1, keepdims=True)
    idx_ref[...] = idx_out.astype(i32)

    @pl.when(i == 0)
    def _():
        cnt_ref[...] = jnp.zeros_like(cnt_ref)

    cnt_ref[...] += jnp.sum(picked, axis=0, keepdims=True)


def _out_ln_router(ya, ys, proj, x2, wa, ws, wo, ln_g, ln_b, w_router, b_router):
    def resident(shape):
        return pl.BlockSpec(shape, lambda i: (0,) * len(shape), pipeline_mode=pl.Buffered(1))

    def gate_cols(block):
        return pl.BlockSpec((OL_TM, GATE_TN), lambda i, block=block: (i, block))

    return pl.pallas_call(
        _out_ln_router_kernel,
        grid=(T // OL_TM,),
        in_specs=[pl.BlockSpec((OL_TM, BRANCH_WIDTH), lambda i: (i, 0)),
                  pl.BlockSpec((OL_TM, BRANCH_WIDTH), lambda i: (i, 0)),
                  gate_cols(COL_GA), gate_cols(COL_GA + 1),
                  gate_cols(COL_GS), gate_cols(COL_GS + 1),
                  pl.BlockSpec((OL_TM, D_MODEL), lambda i: (i, 0)),
                  resident((BRANCH_WIDTH, D_MODEL)),
                  resident((BRANCH_WIDTH, D_MODEL)),
                  resident((D_MODEL, D_MODEL)),
                  resident((1, D_MODEL)),
                  resident((1, D_MODEL)),
                  resident((D_MODEL, 2 * N_EXPERTS)),
                  resident((1, N_EXPERTS))],
        out_specs=[pl.BlockSpec((OL_TM, D_MODEL), lambda i: (i, 0)),
                   pl.BlockSpec((OL_TM, TOP_K), lambda i: (i, 0)),
                   pl.BlockSpec((OL_TM, TOP_K), lambda i: (i, 0)),
                   pl.BlockSpec((1, N_EXPERTS), lambda i: (0, 0))],
        out_shape=[jax.ShapeDtypeStruct((T, D_MODEL), f32),
                   jax.ShapeDtypeStruct((T, TOP_K), i32),
                   jax.ShapeDtypeStruct((T, TOP_K), f32),
                   jax.ShapeDtypeStruct((1, N_EXPERTS), f32)],
        compiler_params=pltpu.CompilerParams(
            dimension_semantics=("arbitrary",), vmem_limit_bytes=VMEM_LIMIT_MIX),
        name="out_ln_router",
    )(ya, ys, proj, proj, proj, proj, x2, wa, ws, wo, ln_g, ln_b, w_router, b_router)


RK_TM = 512


def _rank_kernel(idx_ref, pstart_ref, dest_ref, carry_sc):
    i = pl.program_id(0)

    @pl.when(i == 0)
    def _():
        carry_sc[...] = jnp.zeros_like(carry_sc)

    idx = idx_ref[...]
    lane = lax.broadcasted_iota(i32, (RK_TM, N_EXPERTS), 1)
    picked = jnp.zeros((RK_TM, N_EXPERTS), f32)
    for k in range(TOP_K):
        picked = picked + (lane == idx[:, k:k + 1]).astype(f32)
    ri = lax.broadcasted_iota(i32, (RK_TM, RK_TM), 0)
    ci = lax.broadcasted_iota(i32, (RK_TM, RK_TM), 1)
    lower = (ci < ri).astype(bf16)
    before = jnp.dot(lower, picked.astype(bf16), preferred_element_type=f32)
    slot = before + carry_sc[...] + pstart_ref[...]
    lane_k = lax.broadcasted_iota(i32, (RK_TM, TOP_K), 1)
    dest = jnp.zeros((RK_TM, TOP_K), f32)
    for k in range(TOP_K):
        d = jnp.sum(jnp.where(lane == idx[:, k:k + 1], slot, 0.0), axis=-1, keepdims=True)
        dest = jnp.where(lane_k == k, d, dest)
    dest_ref[...] = dest.astype(i32)
    carry_sc[...] += jnp.sum(picked, axis=0, keepdims=True)


def _rank(idx, pstart):
    return pl.pallas_call(
        _rank_kernel,
        grid=(T // RK_TM,),
        in_specs=[pl.BlockSpec((RK_TM, TOP_K), lambda i: (i, 0)),
                  pl.BlockSpec((1, N_EXPERTS), lambda i: (0, 0))],
        out_specs=pl.BlockSpec((RK_TM, TOP_K), lambda i: (i, 0)),
        out_shape=jax.ShapeDtypeStruct((T, TOP_K), i32),
        scratch_shapes=[pltpu.VMEM((1, N_EXPERTS), f32)],
        compiler_params=pltpu.CompilerParams(dimension_semantics=("arbitrary",)),
        name="rank",
    )(idx, pstart)


DP_TM = 256
HALF = D_MODEL // 2


def _pack_bf16_pairs(x):
    bits = lax.bitcast_convert_type(x.astype(bf16).astype(f32), i32)
    return bits[:, HALF:] | lax.shift_right_logical(bits[:, :HALF], 16)


def _unpack_bf16_pairs(w):
    lo = lax.bitcast_convert_type(lax.shift_left(w, 16), f32)
    hi = lax.bitcast_convert_type(w & jnp.int32(-65536), f32)
    return jnp.concatenate([lo, hi], axis=1).astype(bf16)


ZROWS = 256
SUBLANES = 8


def _zero_fill(pstart_ref, plen_ref, meta_ref, xs_hbm, zbuf, zsem, wait):
    def go(cp):
        if wait:
            cp.wait()
        else:
            cp.start()

    def expert(e, carry):
        start, n = pstart_ref[e], plen_ref[e]
        head = n & (SUBLANES - 1)
        for t in range(SUBLANES - 1):
            @pl.when(t < head)
            def _():
                go(pltpu.make_async_copy(zbuf.at[pl.ds(0, 1)], xs_hbm.at[pl.ds(start + t, 1)],
                                         zsem.at[0]))
        b = SUBLANES
        while b <= ZROWS:
            @pl.when((n & b) != 0)
            def _(b=b):
                off = pl.multiple_of(start + head + (n & ~(2 * b - 1)), SUBLANES)
                go(pltpu.make_async_copy(zbuf.at[pl.ds(0, b)], xs_hbm.at[pl.ds(off, b)],
                                         zsem.at[0]))
            b *= 2
        return carry

    lax.fori_loop(0, N_EXPERTS, expert, 0)

    def unowned(blk, carry):
        for part in range(BM // ZROWS):
            off = pl.multiple_of(blk * BM + part * ZROWS, ZROWS)
            go(pltpu.make_async_copy(zbuf, xs_hbm.at[pl.ds(off, ZROWS)], zsem.at[0]))
        return carry

    lax.fori_loop(meta_ref[0], NB, unowned, 0)


def _dispatch_kernel(pstart_ref, plen_ref, meta_ref, dest_ref, h_ref, xs_hbm,
                     pack_a, pack_b, zbuf, sem, zsem):
    i, n = pl.program_id(0), pl.num_programs(0)
    first = i == 0

    @pl.when(first)
    def _():
        zbuf[...] = jnp.zeros_like(zbuf)
        _zero_fill(pstart_ref, plen_ref, meta_ref, xs_hbm, zbuf, zsem, wait=False)

    def wait_rows(pack, s):
        for _ in range(TOP_K):
            pltpu.make_async_copy(pack, xs_hbm.at[pl.ds(0, DP_TM)], sem.at[s]).wait()

    def step(pack, s, other, s_other):
        pack[...] = _pack_bf16_pairs(h_ref[...])

        def issue(t, carry):
            for k in range(TOP_K):
                pltpu.make_async_copy(pack.at[pl.ds(t, 1)],
                                      xs_hbm.at[pl.ds(dest_ref[t * TOP_K + k], 1)],
                                      sem.at[s]).start()
            return carry

        lax.fori_loop(0, DP_TM, issue, 0)

        @pl.when(i > 0)
        def _():
            wait_rows(other, s_other)

        @pl.when(i == n - 1)
        def _():
            wait_rows(pack, s)

    @pl.when(i % 2 == 0)
    def _():
        step(pack_a, 0, pack_b, 1)

    @pl.when(i % 2 == 1)
    def _():
        step(pack_b, 1, pack_a, 0)

    @pl.when(first)
    def _():
        _zero_fill(pstart_ref, plen_ref, meta_ref, xs_hbm, zbuf, zsem, wait=True)


def _dispatch(h, dest_flat, pad_start, pad_len, meta):
    grid_spec = pltpu.PrefetchScalarGridSpec(
        num_scalar_prefetch=3,
        grid=(T // DP_TM,),
        in_specs=[pl.BlockSpec((DP_TM * TOP_K,), lambda i, *_: (i,), memory_space=pltpu.SMEM),
                  pl.BlockSpec((DP_TM, D_MODEL), lambda i, *_: (i, 0))],
        out_specs=pl.BlockSpec(memory_space=pl.ANY),
        scratch_shapes=[pltpu.VMEM((DP_TM, HALF), i32), pltpu.VMEM((DP_TM, HALF), i32),
                        pltpu.VMEM((ZROWS, HALF), i32),
                        pltpu.SemaphoreType.DMA((2,)), pltpu.SemaphoreType.DMA((1,))],
    )
    return pl.pallas_call(
        _dispatch_kernel,
        grid_spec=grid_spec,
        out_shape=jax.ShapeDtypeStruct((CAP, HALF), i32),
        compiler_params=pltpu.CompilerParams(dimension_semantics=("arbitrary",)),
        name="dispatch",
    )(pad_start, pad_len, meta, dest_flat, h)


GU_TN = 1024
DN_TN = 2048


def _stream_expert_weights(bs_ref, se_ref, meta_ref, w_hbm, col_offsets, tn, wbuf, sem, caches):
    j, i, nj = pl.program_id(0), pl.program_id(1), pl.num_programs(0)
    n_seg = jnp.maximum(meta_ref[1], 1)
    seg = bs_ref[i]
    first_block = (i == 0) | (seg != bs_ref[jnp.maximum(i - 1, 0)])

    def copies(s, jj, slot):
        expert = se_ref[s]
        out = []
        for c, off in enumerate(col_offsets):
            start = off + jj * tn
            if not isinstance(start, int):
                start = pl.multiple_of(start, tn)
            out.append(pltpu.make_async_copy(w_hbm.at[expert, :, pl.ds(start, tn)],
                                             wbuf.at[slot, c], sem.at[slot]))
        return out

    @pl.when(first_block)
    def _():
        seq = j * n_seg + seg
        slot = seq % 2

        @pl.when(seq == 0)
        def _():
            for cp in copies(0, 0, 0):
                cp.start()

        more = seg + 1 < n_seg

        @pl.when(more | (j + 1 < nj))
        def _():
            for cp in copies(jnp.where(more, seg + 1, 0), jnp.where(more, j, j + 1), 1 - slot):
                cp.start()

        for cp in copies(seg, j, slot):
            cp.wait()
        for c, cache in enumerate(caches):
            cache[...] = wbuf[slot, c].astype(bf16)


def _expert_gu_kernel(be_ref, bs_ref, se_ref, meta_ref, x_ref, w_hbm, bg_ref, bl_ref, o_ref,
                      wbuf, wg_sc, wl_sc, sem):
    del be_ref
    i = pl.program_id(1)
    last = jnp.maximum(meta_ref[0], 1) - 1

    @pl.when(i <= last)
    def _():
        _stream_expert_weights(bs_ref, se_ref, meta_ref, w_hbm, (0, D_MODEL), GU_TN,
                               wbuf, sem, (wg_sc, wl_sc))
        x = _unpack_bf16_pairs(x_ref[...])
        glu = jnp.dot(x, wg_sc[...], preferred_element_type=f32) + bg_ref[...]
        lin = jnp.dot(x, wl_sc[...], preferred_element_type=f32) + bl_ref[...]
        glu = jnp.minimum(glu, SWIGLU_LIMIT)
        lin = jnp.clip(lin, -SWIGLU_LIMIT, SWIGLU_LIMIT)
        o_ref[...] = (glu * _sigmoid(SWIGLU_ALPHA * glu) * (lin + 1.0)).astype(o_ref.dtype)

    @pl.when(i > last)
    def _():
        o_ref[...] = jnp.zeros_like(o_ref)


def _used_block(j, i, be, bs, se, meta):
    return jnp.minimum(i, jnp.maximum(meta[0], 1) - 1)


def _expert_gu(layout, xs, w_gu, b_gu):
    nj = D_MODEL // GU_TN
    blk = _used_block
    grid_spec = pltpu.PrefetchScalarGridSpec(
        num_scalar_prefetch=4,
        grid=(nj, NB),
        in_specs=[
            pl.BlockSpec((BM, HALF), lambda *a: (blk(*a), 0)),
            pl.BlockSpec(memory_space=pl.ANY),
            pl.BlockSpec((None, 1, GU_TN), lambda *a: (a[2][blk(*a)], 0, a[0])),
            pl.BlockSpec((None, 1, GU_TN), lambda *a: (a[2][blk(*a)], 0, nj + a[0])),
        ],
        out_specs=pl.BlockSpec((BM, GU_TN), lambda j, i, *_: (i, j)),
        scratch_shapes=[pltpu.VMEM((2, 2, D_MODEL, GU_TN), f32),
                        pltpu.VMEM((D_MODEL, GU_TN), bf16), pltpu.VMEM((D_MODEL, GU_TN), bf16),
                        pltpu.SemaphoreType.DMA((2,))],
    )
    return pl.pallas_call(
        _expert_gu_kernel,
        grid_spec=grid_spec,
        out_shape=jax.ShapeDtypeStruct((CAP, D_MODEL), bf16),
        compiler_params=pltpu.CompilerParams(
            dimension_semantics=("arbitrary", "arbitrary"), vmem_limit_bytes=VMEM_LIMIT),
        name="expert_gu",
    )(*layout, xs, w_gu, b_gu, b_gu)


def _expert_down_kernel(be_ref, bs_ref, se_ref, meta_ref, a_ref, w_hbm, b_ref, o_ref,
                        wbuf, w_sc, sem):
    del be_ref
    i = pl.program_id(1)
    last = jnp.maximum(meta_ref[0], 1) - 1

    @pl.when(i <= last)
    def _():
        _stream_expert_weights(bs_ref, se_ref, meta_ref, w_hbm, (0,), DN_TN, wbuf, sem, (w_sc,))
        o_ref[...] = jnp.dot(a_ref[...], w_sc[...], preferred_element_type=f32) + b_ref[...]

    @pl.when(i > last)
    def _():
        o_ref[...] = jnp.zeros_like(o_ref)


def _expert_down(layout, act, w_down, b_down):
    blk = _used_block
    grid_spec = pltpu.PrefetchScalarGridSpec(
        num_scalar_prefetch=4,
        grid=(D_MODEL // DN_TN, NB),
        in_specs=[
            pl.BlockSpec((BM, D_MODEL), lambda *a: (blk(*a), 0)),
            pl.BlockSpec(memory_space=pl.ANY),
            pl.BlockSpec((None, 1, DN_TN), lambda *a: (a[2][blk(*a)], 0, a[0])),
        ],
        out_specs=pl.BlockSpec((BM, DN_TN), lambda j, i, *_: (i, j)),
        scratch_shapes=[pltpu.VMEM((2, 1, D_MODEL, DN_TN), f32),
                        pltpu.VMEM((D_MODEL, DN_TN), bf16),
                        pltpu.SemaphoreType.DMA((2,))],
    )
    return pl.pallas_call(
        _expert_down_kernel,
        grid_spec=grid_spec,
        out_shape=jax.ShapeDtypeStruct((CAP, D_MODEL), f32),
        compiler_params=pltpu.CompilerParams(
            dimension_semantics=("arbitrary", "arbitrary"), vmem_limit_bytes=VMEM_LIMIT),
        name="expert_down",
    )(*layout, act, w_down, b_down)


CB_TM = 128
CB_SLABS = 4


def _combine_kernel(dcur_ref, dnext_ref, y_hbm, gate_ref, h_ref, g_ref, b_ref, o_ref,
                    buf_a, buf_b, sem):
    i, n = pl.program_id(0), pl.num_programs(0)

    def start_rows(d_ref, buf, s, t):
        for k in range(TOP_K):
            pltpu.make_async_copy(y_hbm.at[pl.ds(d_ref[t * TOP_K + k], 1)],
                                  buf.at[k, pl.ds(t, 1)], sem.at[s]).start()

    def wait_tile(buf, s):
        for k in range(TOP_K):
            pltpu.make_async_copy(y_hbm.at[pl.ds(0, CB_TM)], buf.at[k], sem.at[s]).wait()

    @pl.when(i == 0)
    def _():
        def prime(t, carry):
            start_rows(dcur_ref, buf_a, 0, t)
            return carry
        lax.fori_loop(0, CB_TM, prime, 0)

    def step(cur, s_cur, nxt, s_nxt):
        wait_tile(cur, s_cur)

        def slabs(gi, carry):
            for part in range(CB_SLABS):
                t0 = pl.multiple_of((gi * CB_SLABS + part) * SUBLANES, SUBLANES)
                for tt in range(SUBLANES):
                    start_rows(dnext_ref, nxt, s_nxt, t0 + tt)
                rows = pl.ds(t0, SUBLANES)
                gate = gate_ref[rows, :]
                ffn = gate[:, 0:1] * cur[0, rows, :]
                for k in range(1, TOP_K):
                    ffn = ffn + gate[:, k:k + 1] * cur[k, rows, :]
                o_ref[rows, :] = _layer_norm(ALPHA * h_ref[rows, :] + ffn, g_ref[...], b_ref[...])
            return carry

        lax.fori_loop(0, CB_TM // (SUBLANES * CB_SLABS), slabs, 0)

        @pl.when(i == n - 1)
        def _():
            wait_tile(nxt, s_nxt)

    @pl.when(i % 2 == 0)
    def _():
        step(buf_a, 0, buf_b, 1)

    @pl.when(i % 2 == 1)
    def _():
        step(buf_b, 1, buf_a, 0)


def _combine(y, dest_flat, gate, h, ln_g, ln_b):
    n_tiles = T // CB_TM
    return pl.pallas_call(
        _combine_kernel,
        grid=(n_tiles,),
        in_specs=[pl.BlockSpec((CB_TM * TOP_K,), lambda i: (i,), memory_space=pltpu.SMEM),
                  pl.BlockSpec((CB_TM * TOP_K,), lambda i: (jnp.minimum(i + 1, n_tiles - 1),),
                               memory_space=pltpu.SMEM),
                  pl.BlockSpec(memory_space=pl.ANY),
                  pl.BlockSpec((CB_TM, TOP_K), lambda i: (i, 0)),
                  pl.BlockSpec((CB_TM, D_MODEL), lambda i: (i, 0)),
                  pl.BlockSpec((1, D_MODEL), lambda i: (0, 0)),
                  pl.BlockSpec((1, D_MODEL), lambda i: (0, 0))],
        out_specs=pl.BlockSpec((CB_TM, D_MODEL), lambda i: (i, 0)),
        out_shape=jax.ShapeDtypeStruct((T, D_MODEL), f32),
        scratch_shapes=[pltpu.VMEM((TOP_K, CB_TM, D_MODEL), f32),
                        pltpu.VMEM((TOP_K, CB_TM, D_MODEL), f32),
                        pltpu.SemaphoreType.DMA((2,))],
        compiler_params=pltpu.CompilerParams(dimension_semantics=("arbitrary",),
                                             vmem_limit_bytes=VMEM_LIMIT),
        name="combine",
    )(dest_flat, dest_flat, y, gate, h, ln_g, ln_b)


def _expert_layout(counts):
    nblk = (counts + BM - 1) // BM
    bend = jnp.cumsum(nblk)
    pstart = (bend - nblk) * BM
    before = bend[None, :] <= jnp.arange(NB, dtype=i32)[:, None]
    block_e = jnp.minimum(jnp.sum(before.astype(i32), axis=1), N_EXPERTS - 1)
    has = nblk > 0
    block_seg = jnp.sum((before & has[None, :]).astype(i32), axis=1)
    seg_of_e = jnp.cumsum(has.astype(i32)) - 1
    ids = jnp.arange(N_EXPERTS, dtype=i32)
    seg_expert = jnp.sum(jnp.where(has[None, :] & (seg_of_e[None, :] == ids[:, None]),
                                   ids[None, :], 0), axis=1)
    meta = jnp.stack([bend[-1], jnp.sum(has.astype(i32))]).astype(i32)
    padding = ((pstart + counts).astype(i32), (nblk * BM - counts).astype(i32))
    return pstart, padding, (block_e, block_seg.astype(i32), seg_expert.astype(i32), meta)


def kernel(x, w_in, rel_bias, sgu_ln_g, sgu_ln_b, sgu_w, sgu_b, w_branch, w_out, ln1_g, ln1_b,
           w_router, b_router, w_gu, b_gu, w_down, b_down, ln2_g, ln2_b):
    x2 = x.reshape(T, D_MODEL)
    proj = _inproj(x2, w_in[0].astype(bf16))
    y_attn = _attention(proj.reshape(BATCH, SEQ, IN_COLS), _attention_bias(rel_bias))
    y_sgu = _sgu(proj, sgu_ln_g, sgu_ln_b, sgu_w[0], sgu_b[0].T)
    wr_hi = w_router[0].astype(bf16)
    wr_lo = (w_router[0] - wr_hi.astype(f32)).astype(bf16)
    h, idx, gate, counts = _out_ln_router(
        y_attn.reshape(T, BRANCH_WIDTH), y_sgu, proj, x2,
        w_branch[0, 0].astype(bf16), w_branch[0, 1].astype(bf16), w_out[0].astype(bf16),
        ln1_g, ln1_b, jnp.concatenate([wr_hi, wr_lo], axis=1), b_router)
    pstart, padding, layout = _expert_layout(counts[0].astype(i32))
    dest = _rank(idx, pstart.astype(f32)[None, :]).reshape(T * TOP_K)
    xs = _dispatch(h, dest, *padding, layout[-1])
    act = _expert_gu(layout, xs, w_gu.reshape(N_EXPERTS, D_MODEL, 2 * D_MODEL),
                     b_gu.reshape(N_EXPERTS, 1, 2 * D_MODEL))
    ye = _expert_down(layout, act, w_down.reshape(N_EXPERTS, D_MODEL, D_MODEL),
                      b_down.reshape(N_EXPERTS, 1, D_MODEL))
    out = _combine(ye, dest, gate, h, ln2_g, ln2_b)
    return out.reshape(BATCH, SEQ, D_MODEL)
```

```python
import functools
import math

import numpy as np
import jax
import jax.numpy as jnp
from jax import lax
from jax.experimental import pallas as pl
from jax.experimental.pallas import tpu as pltpu

D_MODEL = 2048
BATCH = 8
SEQ = 2048
T = BATCH * SEQ
BRANCH_WIDTH = D_MODEL // 2
HEAD_DIM = 128
ATTN_GROUPS = 3
WINDOWS = (128, 512, 2048)
DILATIONS = (1, 4, 16)
ATTN_HEADS = BRANCH_WIDTH // HEAD_DIM
QKV_WIDTH = ATTN_GROUPS * ATTN_HEADS * HEAD_DIM
ATTN_BLOCK = 128
NUM_BUCKETS = 32
MAX_DISTANCE = 2048
SGU_CHUNK = 128
SGU_GROUPS = BRANCH_WIDTH // SGU_CHUNK
IN_COLS = 3 * QKV_WIDTH + 2 * BRANCH_WIDTH + 2 * D_MODEL
N_EXPERTS = 32
TOP_K = 4
SWIGLU_ALPHA = 1.702
SWIGLU_LIMIT = 7.0
LN_EPS = 1e-5
ALPHA = 2.0 ** 0.25
LOG2E = math.log2(math.e)
ATTN_CHUNK = 16

COL_U = 3 * QKV_WIDTH // 1024
COL_V = COL_U + 1
COL_GA = COL_U + 2
COL_GS = COL_GA + 2

BM = 512
NB = (T * TOP_K + N_EXPERTS * (BM - 1) + BM - 1) // BM
CAP = NB * BM

VMEM_LIMIT = 56 * 1024 * 1024
VMEM_LIMIT_MIX = 61 * 1024 * 1024

f32 = jnp.float32
bf16 = jnp.bfloat16
i32 = jnp.int32


def _sigmoid(x):
    return 0.5 * jnp.tanh(0.5 * x) + 0.5


def _layer_norm(x, g, b):
    mu = jnp.mean(x, axis=-1, keepdims=True)
    xc = x - mu
    var = jnp.mean(xc * xc, axis=-1, keepdims=True)
    return xc * lax.rsqrt(var + LN_EPS) * g + b


IP_TM = 1024
IP_TN = 1536
IP_CHUNK = 256


def _inproj_epilogue(col):
    if col < COL_U * 1024:
        return "plain", lambda a: a
    if col < COL_GA * 1024:
        return "gelu", jax.nn.gelu
    return "sigmoid", _sigmoid


def _inproj_kernel(x_ref, w_ref, o_ref, xb_sc):
    j = pl.program_id(1)

    @pl.when(j == 0)
    def _():
        xb_sc[...] = x_ref[...].astype(bf16)

    patterns = {}
    for jj in range(IN_COLS // IP_TN):
        names = tuple(_inproj_epilogue(jj * IP_TN + c)[0] for c in range(0, IP_TN, IP_CHUNK))
        patterns.setdefault(names, []).append(jj)

    for tiles in patterns.values():
        cond = functools.reduce(lambda a, b: a | b, [j == jj for jj in tiles])

        @pl.when(cond)
        def _(first=tiles[0]):
            for c in range(0, IP_TN, IP_CHUNK):
                acc = jnp.dot(xb_sc[...], w_ref[:, c:c + IP_CHUNK], preferred_element_type=f32)
                o_ref[:, c:c + IP_CHUNK] = _inproj_epilogue(first * IP_TN + c)[1](acc)


def _inproj(x2, wb):
    return pl.pallas_call(
        _inproj_kernel,
        grid=(T // IP_TM, IN_COLS // IP_TN),
        in_specs=[pl.BlockSpec((IP_TM, D_MODEL), lambda i, j: (i, 0)),
                  pl.BlockSpec((D_MODEL, IP_TN), lambda i, j: (0, j))],
        out_specs=pl.BlockSpec((IP_TM, IP_TN), lambda i, j: (i, j)),
        out_shape=jax.ShapeDtypeStruct((T, IN_COLS), f32),
        scratch_shapes=[pltpu.VMEM((IP_TM, D_MODEL), bf16)],
        compiler_params=pltpu.CompilerParams(
            dimension_semantics=("arbitrary", "arbitrary"), vmem_limit_bytes=VMEM_LIMIT),
        name="inproj",
    )(x2, wb)


def _attn_kernel(q0, q1, q2, k0, k1, k2, v0, v1, v2, bias_ref, o_ref,
                 m0, m1, m2, l0, l1, l2, a0, a1, a2, out_sc, q2d, k2d, v2d, m0d, l0d, a0d):
    ms, ls, accs = (m0, m1, m2), (l0, l1, l2), (a0, a1, a2)
    scale = HEAD_DIM ** -0.5 * LOG2E
    QB = ATTN_BLOCK

    def rows(start, n, r):
        return pl.ds(start, n, stride=r) if r > 1 else pl.ds(start, n)

    PRE = 4
    for src, dst in ((q2, q2d), (k2, k2d), (v2, v2d)):
        for c in range(PRE):
            dst[pl.ds(c * (SEQ // PRE), SEQ // PRE), :] = src[rows(c, SEQ // PRE, PRE), :]
    qs, ks, vs = (q0, q1, q2d), (k0, k1, k2d), (v0, v1, v2d)

    def src_rows(g, start, n):
        r = DILATIONS[g]
        if g < ATTN_GROUPS - 1:
            return rows(start, n, r)
        return rows((start % PRE) * (SEQ // PRE) + start // PRE, n, r // PRE)

    def tiles(g, specs):
        r = DILATIONS[g]
        nk = specs[0][2]
        bias = bias_ref[g, :, 2 * QB - nk:]
        s, v = [], []
        for q_start, k_start, _ in specs:
            q = qs[g][src_rows(g, q_start, QB), :].astype(bf16)
            k = ks[g][src_rows(g, k_start, nk), :].astype(bf16)
            v.append(vs[g][src_rows(g, k_start, nk), :].astype(bf16))
            s.append(lax.dot_general(q, k, (((1,), (1,)), ((), ())), preferred_element_type=f32))
        s = jnp.stack(s) * scale + bias[None]
        m = jnp.max(s, axis=-1, keepdims=True)
        p = jnp.exp2(s - m)
        l = jnp.sum(p, axis=-1, keepdims=True)
        p = p.astype(bf16)
        out = []
        for t in range(len(specs)):
            pv = jnp.dot(p[t], v[t], preferred_element_type=f32)
            out.append((jnp.broadcast_to(m[t], pv.shape), jnp.broadcast_to(l[t], pv.shape), pv))
        return out

    for g in range(ATTN_GROUPS):
        r = DILATIONS[g]
        sub_len = SEQ // r
        first, rest = [], []
        for c in range(r):
            for n in range(sub_len // QB):
                q_start = c + n * QB * r
                dst = pl.ds(c * sub_len + n * QB, QB)
                if n == 0:
                    first.append(((q_start, q_start, QB), dst))
                else:
                    rest.append(((q_start, q_start - QB * r, 2 * QB), dst))
        for todo in (first, rest):
            for at in range(0, len(todo), ATTN_CHUNK):
                chunk = todo[at:at + ATTN_CHUNK]
                for (_, dst), (m, l, pv) in zip(chunk, tiles(g, [spec for spec, _ in chunk])):
                    ms[g][dst, :] = m
                    ls[g][dst, :] = l
                    accs[g][dst, :] = pv

    for src, dst in ((m0, m0d), (l0, l0d), (a0, a0d)):
        for c in range(PRE):
            dst[pl.ds(c * (SEQ // PRE), SEQ // PRE), :] = src[rows(c, SEQ // PRE, PRE), :]
    ms, ls, accs = (m0d, m1, m2), (l0d, l1, l2), (a0d, a1, a2)
    order = (PRE,) + DILATIONS[1:]

    r_max = DILATIONS[-1]
    for c in range(r_max):
        sel = []
        for g in range(ATTN_GROUPS):
            r = order[g]
            sel.append(rows((c % r) * (SEQ // r) + c // r, SEQ // r_max, r_max // r))
        m_g = [ms[g][sel[g], :] for g in range(ATTN_GROUPS)]
        m_all = jnp.maximum(jnp.maximum(m_g[0], m_g[1]), m_g[2])
        w_g = [jnp.exp2(m - m_all) for m in m_g]
        den = sum(w * ls[g][sel[g], :] for g, w in enumerate(w_g))
        num = sum(w * accs[g][sel[g], :] for g, w in enumerate(w_g))
        out_sc[rows(c, SEQ // r_max, r_max), :] = num / den
    o_ref[...] = out_sc[...].astype(o_ref.dtype)


def _attention(proj3, bias):
    hb = HEAD_DIM
    def col(kind, g):
        off = (kind * ATTN_GROUPS + g) * ATTN_HEADS
        return pl.BlockSpec((None, SEQ, hb), lambda b, h, off=off: (b, 0, off + h))
    in_specs = [col(kind, g) for kind in range(3) for g in range(ATTN_GROUPS)]
    in_specs.append(pl.BlockSpec((ATTN_GROUPS, None, ATTN_BLOCK, 2 * ATTN_BLOCK),
                                 lambda b, h: (0, h, 0, 0)))
    return pl.pallas_call(
        _attn_kernel,
        grid=(BATCH, ATTN_HEADS),
        in_specs=in_specs,
        out_specs=pl.BlockSpec((None, SEQ, hb), lambda b, h: (b, 0, h)),
        out_shape=jax.ShapeDtypeStruct((BATCH, SEQ, BRANCH_WIDTH), bf16),
        scratch_shapes=[pltpu.VMEM((SEQ, hb), f32)] * (3 * ATTN_GROUPS + 1 + 3 + 3),
        compiler_params=pltpu.CompilerParams(
            dimension_semantics=("arbitrary", "arbitrary"), vmem_limit_bytes=VMEM_LIMIT),
        name="attention",
    )(*([proj3] * 9), bias)


def _t5_bucket(dist):
    exact = NUM_BUCKETS // 2
    d = jnp.maximum(dist, 1).astype(f32)
    large = exact + (jnp.log(d / exact) / math.log(MAX_DISTANCE / exact)
                     * (NUM_BUCKETS - exact)).astype(i32)
    large = jnp.minimum(large, NUM_BUCKETS - 1)
    return jnp.where(dist < exact, dist, large)


def _attention_bias(rel_bias):
    qi = np.arange(ATTN_BLOCK, dtype=np.int32)[:, None]
    kj = np.arange(2 * ATTN_BLOCK, dtype=np.int32)[None, :]
    sub = qi + ATTN_BLOCK - kj
    table = rel_bias.reshape(NUM_BUCKETS, ATTN_GROUPS, ATTN_HEADS)
    out = []
    for g in range(ATTN_GROUPS):
        span = WINDOWS[g] // DILATIONS[g]
        band = (sub >= 0) & (sub <= span)
        bucket = _t5_bucket(jnp.asarray(DILATIONS[g] * np.clip(sub, 0, span), i32))
        onehot = jax.nn.one_hot(bucket.reshape(-1), NUM_BUCKETS, dtype=f32)
        b = jnp.dot(onehot, table[:, g], precision=lax.Precision.HIGHEST)
        b = jnp.where(band[:, :, None], b.reshape(ATTN_BLOCK, 2 * ATTN_BLOCK, ATTN_HEADS), -1e30)
        out.append(b.transpose(2, 0, 1))
    return jnp.stack(out, axis=0).astype(f32) * LOG2E


SGU_TM = 512


def _sgu_kernel(u_ref, v_ref, g_ref, b_ref, w_ref, bst_ref, o_ref):
    vn = _layer_norm(v_ref[...], g_ref[...], b_ref[...]).astype(bf16)
    ri = lax.broadcasted_iota(i32, (SGU_CHUNK, SGU_CHUNK), 0)
    ci = lax.broadcasted_iota(i32, (SGU_CHUNK, SGU_CHUNK), 1)
    causal = ci <= ri
    for g in range(SGU_GROUPS):
        wg = jnp.where(causal, w_ref[g], 0.0).astype(bf16)
        bias = bst_ref[:, g:g + 1]
        cs = slice(g * SGU_CHUNK, (g + 1) * SGU_CHUNK)
        for c in range(SGU_TM // SGU_CHUNK):
            rs = slice(c * SGU_CHUNK, (c + 1) * SGU_CHUNK)
            mixed = jnp.dot(wg, vn[rs, cs], preferred_element_type=f32) + bias
            o_ref[rs, cs] = (u_ref[rs, cs] * mixed).astype(o_ref.dtype)


def _sgu(proj, ln_g, ln_b, w_s, b_st):
    return pl.pallas_call(
        _sgu_kernel,
        grid=(T // SGU_TM,),
        in_specs=[pl.BlockSpec((SGU_TM, BRANCH_WIDTH), lambda i: (i, COL_U)),
                  pl.BlockSpec((SGU_TM, BRANCH_WIDTH), lambda i: (i, COL_V)),
                  pl.BlockSpec((1, BRANCH_WIDTH), lambda i: (0, 0)),
                  pl.BlockSpec((1, BRANCH_WIDTH), lambda i: (0, 0)),
                  pl.BlockSpec((SGU_GROUPS, SGU_CHUNK, SGU_CHUNK), lambda i: (0, 0, 0)),
                  pl.BlockSpec((SGU_CHUNK, SGU_GROUPS), lambda i: (0, 0))],
        out_specs=pl.BlockSpec((SGU_TM, BRANCH_WIDTH), lambda i: (i, 0)),
        out_shape=jax.ShapeDtypeStruct((T, BRANCH_WIDTH), bf16),
        compiler_params=pltpu.CompilerParams(
            dimension_semantics=("arbitrary",), vmem_limit_bytes=VMEM_LIMIT),
        name="sgu",
    )(proj, proj, ln_g, ln_b, w_s, b_st)


OL_TM = 512
GATE_TN = 1024


def _out_ln_router_kernel(ya_ref, ys_ref, ga0_ref, ga1_ref, gs0_ref, gs1_ref, x_ref,
                          wa_ref, ws_ref, wo_ref, g_ref, b_ref, wr_ref, br_ref,
                          h_ref, idx_ref, gate_ref, cnt_ref):
    i = pl.program_id(0)
    a = jnp.dot(ya_ref[...], wa_ref[...], preferred_element_type=f32)
    s = jnp.dot(ys_ref[...], ws_ref[...], preferred_element_type=f32)
    y = jnp.concatenate(
        [ga0_ref[...] * a[:, :GATE_TN] + gs0_ref[...] * s[:, :GATE_TN],
         ga1_ref[...] * a[:, GATE_TN:] + gs1_ref[...] * s[:, GATE_TN:]], axis=1).astype(bf16)
    mixed = jnp.dot(y, wo_ref[...], preferred_element_type=f32)
    h = _layer_norm(ALPHA * x_ref[...] + mixed, g_ref[...], b_ref[...])
    h_ref[...] = h
    h_hi = h.astype(bf16)
    h_lo = (h - h_hi.astype(f32)).astype(bf16)
    t = jnp.dot(h_hi, wr_ref[...], preferred_element_type=f32)
    u = jnp.dot(h_lo, wr_ref[:, :N_EXPERTS], preferred_element_type=f32)
    logits = t[:, :N_EXPERTS] + (t[:, N_EXPERTS:] + u) + br_ref[...]
    lane = lax.broadcasted_iota(i32, (OL_TM, N_EXPERTS), 1).astype(f32)
    lane_k = lax.broadcasted_iota(i32, (OL_TM, TOP_K), 1)
    idx_out = jnp.zeros((OL_TM, TOP_K), f32)
    val_out = jnp.zeros((OL_TM, TOP_K), f32)
    picked = jnp.zeros((OL_TM, N_EXPERTS), f32)
    work = logits
    for k in range(TOP_K):
        m = jnp.max(work, axis=-1, keepdims=True)
        idx = jnp.min(jnp.where(work == m, lane, float(N_EXPERTS)), axis=-1, keepdims=True)
        sel = lane == idx
        idx_out = jnp.where(lane_k == k, idx, idx_out)
        val_out = jnp.where(lane_k == k, m, val_out)
        picked = picked + sel.astype(f32)
        work = jnp.where(sel, -jnp.inf, work)
    e = jnp.exp(val_out - jnp.max(val_out, axis=-1, keepdims=True))
    gate_ref[...] = e / jnp.sum(e, axis=-1, keepdims=True)
    idx_ref[...] = idx_out.astype(i32)

    @pl.when(i == 0)
    def _():
        cnt_ref[...] = jnp.zeros_like(cnt_ref)

    cnt_ref[...] += jnp.sum(picked, axis=0, keepdims=True)


def _out_ln_router(ya, ys, proj, x2, wa, ws, wo, ln_g, ln_b, w_router, b_router):
    def resident(shape):
        return pl.BlockSpec(shape, lambda i: (0,) * len(shape), pipeline_mode=pl.Buffered(1))

    def gate_cols(block):
        return pl.BlockSpec((OL_TM, GATE_TN), lambda i, block=block: (i, block))

    return pl.pallas_call(
        _out_ln_router_kernel,
        grid=(T // OL_TM,),
        in_specs=[pl.BlockSpec((OL_TM, BRANCH_WIDTH), lambda i: (i, 0)),
                  pl.BlockSpec((OL_TM, BRANCH_WIDTH), lambda i: (i, 0)),
                  gate_cols(COL_GA), gate_cols(COL_GA + 1),
                  gate_cols(COL_GS), gate_cols(COL_GS + 1),
                  pl.BlockSpec((OL_TM, D_MODEL), lambda i: (i, 0)),
                  resident((BRANCH_WIDTH, D_MODEL)),
                  resident((BRANCH_WIDTH, D_MODEL)),
                  resident((D_MODEL, D_MODEL)),
                  resident((1, D_MODEL)),
                  resident((1, D_MODEL)),
                  resident((D_MODEL, 2 * N_EXPERTS)),
                  resident((1, N_EXPERTS))],
        out_specs=[pl.BlockSpec((OL_TM, D_MODEL), lambda i: (i, 0)),
                   pl.BlockSpec((OL_TM, TOP_K), lambda i: (i, 0)),
                   pl.BlockSpec((OL_TM, TOP_K), lambda i: (i, 0)),
                   pl.BlockSpec((1, N_EXPERTS), lambda i: (0, 0))],
        out_shape=[jax.ShapeDtypeStruct((T, D_MODEL), f32),
                   jax.ShapeDtypeStruct((T, TOP_K), i32),
                   jax.ShapeDtypeStruct((T, TOP_K), f32),
                   jax.ShapeDtypeStruct((1, N_EXPERTS), f32)],
        compiler_params=pltpu.CompilerParams(
            dimension_semantics=("arbitrary",), vmem_limit_bytes=VMEM_LIMIT_MIX),
        name="out_ln_router",
    )(ya, ys, proj, proj, proj, proj, x2, wa, ws, wo, ln_g, ln_b, w_router, b_router)


RK_TM = 512


def _rank_kernel(idx_ref, pstart_ref, dest_ref, carry_sc):
    i = pl.program_id(0)

    @pl.when(i == 0)
    def _():
        carry_sc[...] = jnp.zeros_like(carry_sc)

    idx = idx_ref[...]
    lane = lax.broadcasted_iota(i32, (RK_TM, N_EXPERTS), 1)
    picked = jnp.zeros((RK_TM, N_EXPERTS), f32)
    for k in range(TOP_K):
        picked = picked + (lane == idx[:, k:k + 1]).astype(f32)
    ri = lax.broadcasted_iota(i32, (RK_TM, RK_TM), 0)
    ci = lax.broadcasted_iota(i32, (RK_TM, RK_TM), 1)
    lower = (ci < ri).astype(bf16)
    before = jnp.dot(lower, picked.astype(bf16), preferred_element_type=f32)
    slot = before + carry_sc[...] + pstart_ref[...]
    lane_k = lax.broadcasted_iota(i32, (RK_TM, TOP_K), 1)
    dest = jnp.zeros((RK_TM, TOP_K), f32)
    for k in range(TOP_K):
        d = jnp.sum(jnp.where(lane == idx[:, k:k + 1], slot, 0.0), axis=-1, keepdims=True)
        dest = jnp.where(lane_k == k, d, dest)
    dest_ref[...] = dest.astype(i32)
    carry_sc[...] += jnp.sum(picked, axis=0, keepdims=True)


def _rank(idx, pstart):
    return pl.pallas_call(
        _rank_kernel,
        grid=(T // RK_TM,),
        in_specs=[pl.BlockSpec((RK_TM, TOP_K), lambda i: (i, 0)),
                  pl.BlockSpec((1, N_EXPERTS), lambda i: (0, 0))],
        out_specs=pl.BlockSpec((RK_TM, TOP_K), lambda i: (i, 0)),
        out_shape=jax.ShapeDtypeStruct((T, TOP_K), i32),
        scratch_shapes=[pltpu.VMEM((1, N_EXPERTS), f32)],
        compiler_params=pltpu.CompilerParams(dimension_semantics=("arbitrary",)),
        name="rank",
    )(idx, pstart)


DP_TM = 256
HALF = D_MODEL // 2


def _pack_bf16_pairs(x):
    bits = lax.bitcast_convert_type(x.astype(bf16).astype(f32), i32)
    return bits[:, HALF:] | lax.shift_right_logical(bits[:, :HALF], 16)


def _unpack_bf16_pairs(w):
    lo = lax.bitcast_convert_type(lax.shift_left(w, 16), f32)
    hi = lax.bitcast_convert_type(w & jnp.int32(-65536), f32)
    return jnp.concatenate([lo, hi], axis=1).astype(bf16)


ZROWS = 256
SUBLANES = 8


def _zero_fill(pstart_ref, plen_ref, meta_ref, xs_hbm, zbuf, zsem, wait):
    def go(cp):
        if wait:
            cp.wait()
        else:
            cp.start()

    def expert(e, carry):
        start, n = pstart_ref[e], plen_ref[e]
        head = n & (SUBLANES - 1)
        for t in range(SUBLANES - 1):
            @pl.when(t < head)
            def _():
                go(pltpu.make_async_copy(zbuf.at[pl.ds(0, 1)], xs_hbm.at[pl.ds(start + t, 1)],
                                         zsem.at[0]))
        b = SUBLANES
        while b <= ZROWS:
            @pl.when((n & b) != 0)
            def _(b=b):
                off = pl.multiple_of(start + head + (n & ~(2 * b - 1)), SUBLANES)
                go(pltpu.make_async_copy(zbuf.at[pl.ds(0, b)], xs_hbm.at[pl.ds(off, b)],
                                         zsem.at[0]))
            b *= 2
        return carry

    lax.fori_loop(0, N_EXPERTS, expert, 0)

    def unowned(blk, carry):
        for part in range(BM // ZROWS):
            off = pl.multiple_of(blk * BM + part * ZROWS, ZROWS)
            go(pltpu.make_async_copy(zbuf, xs_hbm.at[pl.ds(off, ZROWS)], zsem.at[0]))
        return carry

    lax.fori_loop(meta_ref[0], NB, unowned, 0)


def _dispatch_kernel(pstart_ref, plen_ref, meta_ref, dest_ref, h_ref, xs_hbm,
                     pack_a, pack_b, zbuf, sem, zsem):
    i, n = pl.program_id(0), pl.num_programs(0)
    first = i == 0

    @pl.when(first)
    def _():
        zbuf[...] = jnp.zeros_like(zbuf)
        _zero_fill(pstart_ref, plen_ref, meta_ref, xs_hbm, zbuf, zsem, wait=False)

    def wait_rows(pack, s):
        for _ in range(TOP_K):
            pltpu.make_async_copy(pack, xs_hbm.at[pl.ds(0, DP_TM)], sem.at[s]).wait()

    def step(pack, s, other, s_other):
        pack[...] = _pack_bf16_pairs(h_ref[...])

        def issue(t, carry):
            for k in range(TOP_K):
                pltpu.make_async_copy(pack.at[pl.ds(t, 1)],
                                      xs_hbm.at[pl.ds(dest_ref[t * TOP_K + k], 1)],
                                      sem.at[s]).start()
            return carry

        lax.fori_loop(0, DP_TM, issue, 0)

        @pl.when(i > 0)
        def _():
            wait_rows(other, s_other)

        @pl.when(i == n - 1)
        def _():
            wait_rows(pack, s)

    @pl.when(i % 2 == 0)
    def _():
        step(pack_a, 0, pack_b, 1)

    @pl.when(i % 2 == 1)
    def _():
        step(pack_b, 1, pack_a, 0)

    @pl.when(first)
    def _():
        _zero_fill(pstart_ref, plen_ref, meta_ref, xs_hbm, zbuf, zsem, wait=True)


def _dispatch(h, dest_flat, pad_start, pad_len, meta):
    grid_spec = pltpu.PrefetchScalarGridSpec(
        num_scalar_prefetch=3,
        grid=(T // DP_TM,),
        in_specs=[pl.BlockSpec((DP_TM * TOP_K,), lambda i, *_: (i,), memory_space=pltpu.SMEM),
                  pl.BlockSpec((DP_TM, D_MODEL), lambda i, *_: (i, 0))],
        out_specs=pl.BlockSpec(memory_space=pl.ANY),
        scratch_shapes=[pltpu.VMEM((DP_TM, HALF), i32), pltpu.VMEM((DP_TM, HALF), i32),
                        pltpu.VMEM((ZROWS, HALF), i32),
                        pltpu.SemaphoreType.DMA((2,)), pltpu.SemaphoreType.DMA((1,))],
    )
    return pl.pallas_call(
        _dispatch_kernel,
        grid_spec=grid_spec,
        out_shape=jax.ShapeDtypeStruct((CAP, HALF), i32),
        compiler_params=pltpu.CompilerParams(dimension_semantics=("arbitrary",)),
        name="dispatch",
    )(pad_start, pad_len, meta, dest_flat, h)


GU_TN = 1024
DN_TN = 2048


def _stream_expert_weights(bs_ref, se_ref, meta_ref, w_hbm, col_offsets, tn, wbuf, sem, caches):
    j, i, nj = pl.program_id(0), pl.program_id(1), pl.num_programs(0)
    n_seg = jnp.maximum(meta_ref[1], 1)
    seg = bs_ref[i]
    first_block = (i == 0) | (seg != bs_ref[jnp.maximum(i - 1, 0)])

    def copies(s, jj, slot):
        expert = se_ref[s]
        out = []
        for c, off in enumerate(col_offsets):
            start = off + jj * tn
            if not isinstance(start, int):
                start = pl.multiple_of(start, tn)
            out.append(pltpu.make_async_copy(w_hbm.at[expert, :, pl.ds(start, tn)],
                                             wbuf.at[slot, c], sem.at[slot]))
        return out

    @pl.when(first_block)
    def _():
        seq = j * n_seg + seg
        slot = seq % 2

        @pl.when(seq == 0)
        def _():
            for cp in copies(0, 0, 0):
                cp.start()

        more = seg + 1 < n_seg

        @pl.when(more | (j + 1 < nj))
        def _():
            for cp in copies(jnp.where(more, seg + 1, 0), jnp.where(more, j, j + 1), 1 - slot):
                cp.start()

        for cp in copies(seg, j, slot):
            cp.wait()
        for c, cache in enumerate(caches):
            cache[...] = wbuf[slot, c].astype(bf16)


def _expert_gu_kernel(be_ref, bs_ref, se_ref, meta_ref, x_ref, w_hbm, bg_ref, bl_ref, o_ref,
                      wbuf, wg_sc, wl_sc, sem):
    del be_ref
    i = pl.program_id(1)
    last = jnp.maximum(meta_ref[0], 1) - 1

    @pl.when(i <= last)
    def _():
        _stream_expert_weights(bs_ref, se_ref, meta_ref, w_hbm, (0, D_MODEL), GU_TN,
                               wbuf, sem, (wg_sc, wl_sc))
        x = _unpack_bf16_pairs(x_ref[...])
        glu = jnp.dot(x, wg_sc[...], preferred_element_type=f32) + bg_ref[...]
        lin = jnp.dot(x, wl_sc[...], preferred_element_type=f32) + bl_ref[...]
        glu = jnp.minimum(glu, SWIGLU_LIMIT)
        lin = jnp.clip(lin, -SWIGLU_LIMIT, SWIGLU_LIMIT)
        o_ref[...] = (glu * _sigmoid(SWIGLU_ALPHA * glu) * (lin + 1.0)).astype(o_ref.dtype)

    @pl.when(i > last)
    def _():
        o_ref[...] = jnp.zeros_like(o_ref)


def _used_block(j, i, be, bs, se, meta):
    return jnp.minimum(i, jnp.maximum(meta[0], 1) - 1)


def _expert_gu(layout, xs, w_gu, b_gu):
    nj = D_MODEL // GU_TN
    blk = _used_block
    grid_spec = pltpu.PrefetchScalarGridSpec(
        num_scalar_prefetch=4,
        grid=(nj, NB),
        in_specs=[
            pl.BlockSpec((BM, HALF), lambda *a: (blk(*a), 0)),
            pl.BlockSpec(memory_space=pl.ANY),
            pl.BlockSpec((None, 1, GU_TN), lambda *a: (a[2][blk(*a)], 0, a[0])),
            pl.BlockSpec((None, 1, GU_TN), lambda *a: (a[2][blk(*a)], 0, nj + a[0])),
        ],
        out_specs=pl.BlockSpec((BM, GU_TN), lambda j, i, *_: (i, j)),
        scratch_shapes=[pltpu.VMEM((2, 2, D_MODEL, GU_TN), f32),
                        pltpu.VMEM((D_MODEL, GU_TN), bf16), pltpu.VMEM((D_MODEL, GU_TN), bf16),
                        pltpu.SemaphoreType.DMA((2,))],
    )
    return pl.pallas_call(
        _expert_gu_kernel,
        grid_spec=grid_spec,
        out_shape=jax.ShapeDtypeStruct((CAP, D_MODEL), bf16),
        compiler_params=pltpu.CompilerParams(
            dimension_semantics=("arbitrary", "arbitrary"), vmem_limit_bytes=VMEM_LIMIT),
        name="expert_gu",
    )(*layout, xs, w_gu, b_gu, b_gu)


def _expert_down_kernel(be_ref, bs_ref, se_ref, meta_ref, a_ref, w_hbm, b_ref, o_ref,
                        wbuf, w_sc, sem):
    del be_ref
    i = pl.program_id(1)
    last = jnp.maximum(meta_ref[0], 1) - 1

    @pl.when(i <= last)
    def _():
        _stream_expert_weights(bs_ref, se_ref, meta_ref, w_hbm, (0,), DN_TN, wbuf, sem, (w_sc,))
        o_ref[...] = jnp.dot(a_ref[...], w_sc[...], preferred_element_type=f32) + b_ref[...]

    @pl.when(i > last)
    def _():
        o_ref[...] = jnp.zeros_like(o_ref)


def _expert_down(layout, act, w_down, b_down):
    blk = _used_block
    grid_spec = pltpu.PrefetchScalarGridSpec(
        num_scalar_prefetch=4,
        grid=(D_MODEL // DN_TN, NB),
        in_specs=[
            pl.BlockSpec((BM, D_MODEL), lambda *a: (blk(*a), 0)),
            pl.BlockSpec(memory_space=pl.ANY),
            pl.BlockSpec((None, 1, DN_TN), lambda *a: (a[2][blk(*a)], 0, a[0])),
        ],
        out_specs=pl.BlockSpec((BM, DN_TN), lambda j, i, *_: (i, j)),
        scratch_shapes=[pltpu.VMEM((2, 1, D_MODEL, DN_TN), f32),
                        pltpu.VMEM((D_MODEL, DN_TN), bf16),
                        pltpu.SemaphoreType.DMA((2,))],
    )
    return pl.pallas_call(
        _expert_down_kernel,
        grid_spec=grid_spec,
        out_shape=jax.ShapeDtypeStruct((CAP, D_MODEL), f32),
        compiler_params=pltpu.CompilerParams(
            dimension_semantics=("arbitrary", "arbitrary"), vmem_limit_bytes=VMEM_LIMIT),
        name="expert_down",
    )(*layout, act, w_down, b_down)


CB_TM = 128
CB_SLABS = 4


def _combine_kernel(dcur_ref, dnext_ref, y_hbm, gate_ref, h_ref, g_ref, b_ref, o_ref,
                    buf_a, buf_b, sem):
    i, n = pl.program_id(0), pl.num_programs(0)

    def start_rows(d_ref, buf, s, t):
        for k in range(TOP_K):
            pltpu.make_async_copy(y_hbm.at[pl.ds(d_ref[t * TOP_K + k], 1)],
                                  buf.at[k, pl.ds(t, 1)], sem.at[s]).start()

    def wait_tile(buf, s):
        for k in range(TOP_K):
            pltpu.make_async_copy(y_hbm.at[pl.ds(0, CB_TM)], buf.at[k], sem.at[s]).wait()

    @pl.when(i == 0)
    def _():
        def prime(t, carry):
            start_rows(dcur_ref, buf_a, 0, t)
            return carry
        lax.fori_loop(0, CB_TM, prime, 0)

    def step(cur, s_cur, nxt, s_nxt):
        wait_tile(cur, s_cur)

        def slabs(gi, carry):
            for part in range(CB_SLABS):
                t0 = pl.multiple_of((gi * CB_SLABS + part) * SUBLANES, SUBLANES)
                for tt in range(SUBLANES):
                    start_rows(dnext_ref, nxt, s_nxt, t0 + tt)
                rows = pl.ds(t0, SUBLANES)
                gate = gate_ref[rows, :]
                ffn = gate[:, 0:1] * cur[0, rows, :]
                for k in range(1, TOP_K):
                    ffn = ffn + gate[:, k:k + 1] * cur[k, rows, :]
                o_ref[rows, :] = _layer_norm(ALPHA * h_ref[rows, :] + ffn, g_ref[...], b_ref[...])
            return carry

        lax.fori_loop(0, CB_TM // (SUBLANES * CB_SLABS), slabs, 0)

        @pl.when(i == n - 1)
        def _():
            wait_tile(nxt, s_nxt)

    @pl.when(i % 2 == 0)
    def _():
        step(buf_a, 0, buf_b, 1)

    @pl.when(i % 2 == 1)
    def _():
        step(buf_b, 1, buf_a, 0)


def _combine(y, dest_flat, gate, h, ln_g, ln_b):
    n_tiles = T // CB_TM
    return pl.pallas_call(
        _combine_kernel,
        grid=(n_tiles,),
        in_specs=[pl.BlockSpec((CB_TM * TOP_K,), lambda i: (i,), memory_space=pltpu.SMEM),
                  pl.BlockSpec((CB_TM * TOP_K,), lambda i: (jnp.minimum(i + 1, n_tiles - 1),),
                               memory_space=pltpu.SMEM),
                  pl.BlockSpec(memory_space=pl.ANY),
                  pl.BlockSpec((CB_TM, TOP_K), lambda i: (i, 0)),
                  pl.BlockSpec((CB_TM, D_MODEL), lambda i: (i, 0)),
                  pl.BlockSpec((1, D_MODEL), lambda i: (0, 0)),
                  pl.BlockSpec((1, D_MODEL), lambda i: (0, 0))],
        out_specs=pl.BlockSpec((CB_TM, D_MODEL), lambda i: (i, 0)),
        out_shape=jax.ShapeDtypeStruct((T, D_MODEL), f32),
        scratch_shapes=[pltpu.VMEM((TOP_K, CB_TM, D_MODEL), f32),
                        pltpu.VMEM((TOP_K, CB_TM, D_MODEL), f32),
                        pltpu.SemaphoreType.DMA((2,))],
        compiler_params=pltpu.CompilerParams(dimension_semantics=("arbitrary",),
                                             vmem_limit_bytes=VMEM_LIMIT),
        name="combine",
    )(dest_flat, dest_flat, y, gate, h, ln_g, ln_b)


def _expert_layout(counts):
    nblk = (counts + BM - 1) // BM
    bend = jnp.cumsum(nblk)
    pstart = (bend - nblk) * BM
    before = bend[None, :] <= jnp.arange(NB, dtype=i32)[:, None]
    block_e = jnp.minimum(jnp.sum(before.astype(i32), axis=1), N_EXPERTS - 1)
    has = nblk > 0
    block_seg = jnp.sum((before & has[None, :]).astype(i32), axis=1)
    seg_of_e = jnp.cumsum(has.astype(i32)) - 1
    ids = jnp.arange(N_EXPERTS, dtype=i32)
    seg_expert = jnp.sum(jnp.where(has[None, :] & (seg_of_e[None, :] == ids[:, None]),
                                   ids[None, :], 0), axis=1)
    meta = jnp.stack([bend[-1], jnp.sum(has.astype(i32))]).astype(i32)
    padding = ((pstart + counts).astype(i32), (nblk * BM - counts).astype(i32))
    return pstart, padding, (block_e, block_seg.astype(i32), seg_expert.astype(i32), meta)


def kernel(x, w_in, rel_bias, sgu_ln_g, sgu_ln_b, sgu_w, sgu_b, w_branch, w_out, ln1_g, ln1_b,
           w_router, b_router, w_gu, b_gu, w_down, b_down, ln2_g, ln2_b):
    x2 = x.reshape(T, D_MODEL)
    proj = _inproj(x2, w_in[0].astype(bf16))
    y_attn = _attention(proj.reshape(BATCH, SEQ, IN_COLS), _attention_bias(rel_bias))
    y_sgu = _sgu(proj, sgu_ln_g, sgu_ln_b, sgu_w[0], sgu_b[0].T)
    wr_hi = w_router[0].astype(bf16)
    wr_lo = (w_router[0] - wr_hi.astype(f32)).astype(bf16)
    h, idx, gate, counts = _out_ln_router(
        y_attn.reshape(T, BRANCH_WIDTH), y_sgu, proj, x2,
        w_branch[0, 0].astype(bf16), w_branch[0, 1].astype(bf16), w_out[0].astype(bf16),
        ln1_g, ln1_b, jnp.concatenate([wr_hi, wr_lo], axis=1), b_router)
    pstart, padding, layout = _expert_layout(counts[0].astype(i32))
    dest = _rank(idx, pstart.astype(f32)[None, :]).reshape(T * TOP_K)
    xs = _dispatch(h, dest, *padding, layout[-1])
    act = _expert_gu(layout, xs, w_gu.reshape(N_EXPERTS, D_MODEL, 2 * D_MODEL),
                     b_gu.reshape(N_EXPERTS, 1, 2 * D_MODEL))
    ye = _expert_down(layout, act, w_down.reshape(N_EXPERTS, D_MODEL, D_MODEL),
                      b_down.reshape(N_EXPERTS, 1, D_MODEL))
    out = _combine(ye, dest, gate, h, ln2_g, ln2_b)
    return out.reshape(BATCH, SEQ, D_MODEL)
```
